```python
import math
import jax, jax.numpy as jnp
from jax import lax
import numpy as np

D_MODEL = 4096
BATCH = 4
SEQ = 4096
DEPTH = 1

D_FF = int(round(8 * D_MODEL / 3 / 256)) * 256
GLA_HEADS = max(4, D_MODEL // 512)
GLA_KEY_W = D_MODEL // 4
GLA_VAL_W = D_MODEL // 2
GLA_DK = GLA_KEY_W // GLA_HEADS
GLA_DV = GLA_VAL_W // GLA_HEADS
GLA_GATE_RANK = 16
GLA_GATE_NORM = 16.0
GLA_CHUNK = 64
RWKV_HEAD = 64
RWKV_W = D_MODEL // 2
RWKV_HEADS = RWKV_W // RWKV_HEAD
RWKV_DECAY_LORA = max(32, int(round(math.sqrt(D_MODEL) * 1.8 / 32)) * 32)
RWKV_AAA_LORA = max(32, int(round(math.sqrt(D_MODEL) * 1.8 / 32)) * 32)
RWKV_GATE_LORA = max(32, int(round(D_MODEL ** 0.8 * 0.6 / 32)) * 32)
RWKV_LN_EPS = 64e-5
NORM_EPS = 1e-6
MACARON_WEIGHT = 0.5

GLA_SPLITS = (GLA_KEY_W, GLA_KEY_W, GLA_VAL_W, GLA_GATE_RANK, GLA_VAL_W)
RWKV_SPLITS = (RWKV_W, RWKV_DECAY_LORA, RWKV_W, RWKV_W, RWKV_AAA_LORA, RWKV_GATE_LORA)
GLA_COLS = sum(GLA_SPLITS)
RWKV_COLS = sum(RWKV_SPLITS)
W_IN_COLS = GLA_COLS + RWKV_COLS + 2 * D_MODEL

kernel_name = "hybrid_gla_rwkv7_macaron_sandwich"


def _split(t, sizes):
    idx = np.cumsum(sizes)[:-1].tolist()
    return jnp.split(t, idx, axis=-1)


def rms_norm(x, g):
    xf = x.astype(jnp.float32)
    y = xf * lax.rsqrt(jnp.mean(xf * xf, axis=-1, keepdims=True) + NORM_EPS)
    return (y * g.astype(jnp.float32)).astype(x.dtype)


def swiglu(x, w_gate, w_up, w_down):
    return (jax.nn.silu(x @ w_gate) * (x @ w_up)) @ w_down


def gla_chunked(q, k, v, log_a):
    B, S, H, dk = q.shape
    dv = v.shape[-1]
    C = GLA_CHUNK
    N = S // C
    f32 = jnp.float32

    def to_chunks(t):
        return t.astype(f32).reshape(B, N, C, H, t.shape[-1]).transpose(1, 0, 3, 2, 4)

    xs = (to_chunks(q), to_chunks(k), to_chunks(v), to_chunks(log_a))
    causal = jnp.tril(jnp.ones((C, C), dtype=bool))[None, None, :, :, None]

    def step(state, inp):
        qb, kb, vb, gb = inp
        b = jnp.cumsum(gb, axis=2)
        diff = jnp.where(causal, b[:, :, :, None, :] - b[:, :, None, :, :], -jnp.inf)
        attn = jnp.einsum('bhid,bhjd,bhijd->bhij', qb, kb, jnp.exp(diff))
        o = jnp.einsum('bhij,bhjv->bhiv', attn, vb) + jnp.einsum('bhid,bhdv->bhiv', qb * jnp.exp(b), state)
        b_last = b[:, :, -1, :]
        state = state * jnp.exp(b_last)[..., None] + jnp.einsum(
            'bhjd,bhjv->bhdv', kb * jnp.exp(b_last[:, :, None, :] - b), vb)
        return state, o

    init = jnp.zeros((B, H, dk, dv), f32)
    _, o = lax.scan(step, init, xs)
    return o.transpose(1, 0, 3, 2, 4).reshape(B, S, H, dv)


def gla_branch(p, gate_up, gate_bias, out_norm):
    B, S, _ = p.shape
    q, k, v, g_down, out_gate = _split(p, GLA_SPLITS)
    q = (q * (GLA_DK ** -0.5)).reshape(B, S, GLA_HEADS, GLA_DK)
    k = k.reshape(B, S, GLA_HEADS, GLA_DK)
    v = v.reshape(B, S, GLA_HEADS, GLA_DV)
    log_a = jax.nn.log_sigmoid((g_down @ gate_up + gate_bias).astype(jnp.float32)) / GLA_GATE_NORM
    log_a = log_a.reshape(B, S, GLA_HEADS, GLA_DK)
    o = gla_chunked(q, k, v, log_a)
    o = rms_norm(o, out_norm).reshape(B, S, GLA_VAL_W)
    return (o * jax.nn.silu(out_gate.astype(jnp.float32))).astype(p.dtype)


def rwkv7_scan(r, decay, k, v, kk, a):
    B, S, H, N = r.shape

    def step(state, inp):
        r_t, w_t, k_t, v_t, kk_t, a_t = inp
        sa = jnp.einsum('bhvk,bhk->bhv', state, kk_t)
        state = (state * w_t[:, :, None, :] - sa[..., None] * (kk_t * a_t)[:, :, None, :]
                 + v_t[..., None] * k_t[:, :, None, :])
        return state, jnp.einsum('bhvk,bhk->bhv', state, r_t)

    xs = tuple(t.astype(jnp.float32).transpose(1, 0, 2, 3) for t in (r, decay, k, v, kk, a))
    init = jnp.zeros((B, H, N, N), jnp.float32)
    _, y = lax.scan(step, init, xs)
    return y.transpose(1, 0, 2, 3)


def rwkv7_branch(p, shift_mix, w0, w2, a0, a2, g2, k_k, k_a, r_k, ln_w, ln_b):
    B, S, _ = p.shape
    f32 = jnp.float32
    p_prev = jnp.pad(p[:, :-1], ((0, 0), (1, 0), (0, 0)))
    p = p + (p_prev - p) * shift_mix
    r, w_down, k, v, a_down, g_down = _split(p, RWKV_SPLITS)
    w = -jax.nn.softplus(-(w0 + jnp.tanh(w_down) @ w2).astype(f32)) - 0.5
    decay = jnp.exp(-jnp.exp(w))
    a = jax.nn.sigmoid((a0 + a_down @ a2).astype(f32))
    g = jax.nn.sigmoid(g_down) @ g2

    def heads(t):
        return t.reshape(t.shape[:-1] + (RWKV_HEADS, RWKV_HEAD))

    r, k, v, decay, a = heads(r), heads(k), heads(v), heads(decay), heads(a)
    kk = (k * heads(k_k)).astype(f32)
    kk = kk / jnp.maximum(jnp.sqrt(jnp.sum(kk * kk, axis=-1, keepdims=True)), 1e-12)
    k = k * (1.0 + (a - 1.0) * heads(k_a))
    y = rwkv7_scan(r, decay, k, v, kk, a)
    mu = jnp.mean(y, axis=-1, keepdims=True)
    var = jnp.mean(jnp.square(y - mu), axis=-1, keepdims=True)
    y = ((y - mu) * lax.rsqrt(var + RWKV_LN_EPS)).reshape(B, S, RWKV_W) * ln_w + ln_b
    bonus = jnp.sum(r * k * r_k, axis=-1, keepdims=True) * v
    y = (y + bonus.reshape(B, S, RWKV_W)) * g
    return y.astype(p.dtype)


def setup_inputs(seed: int = 0) -> dict:
    key = jax.random.key(seed)
    ks = iter(jax.random.split(key, 48))
    L = DEPTH
    f32 = jnp.float32

    def nrm(shape, scale):
        return jax.random.normal(next(ks), shape, f32) * scale

    def gain(n):
        return 1.0 + nrm((L, n), 0.05)

    pos = jnp.arange(RWKV_W, dtype=f32) / (RWKV_W - 1)
    return {
        "x": nrm((BATCH, SEQ, D_MODEL), 1.0),
        "ffn1_pre_norm": gain(D_MODEL),
        "ffn1_w_gate": nrm((L, D_MODEL, D_FF), D_MODEL ** -0.5),
        "ffn1_w_up": nrm((L, D_MODEL, D_FF), D_MODEL ** -0.5),
        "ffn1_w_down": nrm((L, D_FF, D_MODEL), D_FF ** -0.5),
        "ffn1_post_norm": gain(D_MODEL),
        "mix_pre_norm": gain(D_MODEL),
        "w_in": nrm((L, D_MODEL, W_IN_COLS), D_MODEL ** -0.5),
        "gla_gate_up": nrm((L, GLA_GATE_RANK, GLA_KEY_W), GLA_GATE_RANK ** -0.5),
        "gla_gate_bias": nrm((L, GLA_KEY_W), 0.5) + 2.0,
        "gla_out_norm": gain(GLA_DV),
        "rwkv_shift_mix": jax.random.uniform(next(ks), (L, RWKV_COLS), f32),
        "rwkv_w0": (-6.0 + 5.0 * pos ** 0.85)[None] + nrm((L, RWKV_W), 0.1),
        "rwkv_w2": nrm((L, RWKV_DECAY_LORA, RWKV_W), 0.1 * RWKV_DECAY_LORA ** -0.5),
        "rwkv_a0": nrm((L, RWKV_W), 0.1),
        "rwkv_a2": nrm((L, RWKV_AAA_LORA, RWKV_W), 0.1 * RWKV_AAA_LORA ** -0.5),
        "rwkv_g2": nrm((L, RWKV_GATE_LORA, RWKV_W), RWKV_GATE_LORA ** -0.5),
        "rwkv_k_k": 0.85 + nrm((L, RWKV_W), 0.05),
        "rwkv_k_a": 1.0 + nrm((L, RWKV_W), 0.05),
        "rwkv_r_k": nrm((L, RWKV_HEADS, RWKV_HEAD), 0.1),
        "rwkv_ln_w": gain(RWKV_W),
        "rwkv_ln_b": nrm((L, RWKV_W), 0.02),
        "w_up_gla": nrm((L, GLA_VAL_W, D_MODEL), GLA_VAL_W ** -0.5),
        "w_up_rwkv": nrm((L, RWKV_W, D_MODEL), RWKV_W ** -0.5),
        "w_out": nrm((L, D_MODEL, D_MODEL), D_MODEL ** -0.5),
        "mix_post_norm": gain(D_MODEL),
        "ffn2_pre_norm": gain(D_MODEL),
        "ffn2_w_gate": nrm((L, D_MODEL, D_FF), D_MODEL ** -0.5),
        "ffn2_w_up": nrm((L, D_MODEL, D_FF), D_MODEL ** -0.5),
        "ffn2_w_down": nrm((L, D_FF, D_MODEL), D_FF ** -0.5),
        "ffn2_post_norm": gain(D_MODEL),
    }


def reference(x, ffn1_pre_norm, ffn1_w_gate, ffn1_w_up, ffn1_w_down, ffn1_post_norm,
              mix_pre_norm, w_in, gla_gate_up, gla_gate_bias, gla_out_norm,
              rwkv_shift_mix, rwkv_w0, rwkv_w2, rwkv_a0, rwkv_a2, rwkv_g2, rwkv_k_k, rwkv_k_a,
              rwkv_r_k, rwkv_ln_w, rwkv_ln_b, w_up_gla, w_up_rwkv, w_out, mix_post_norm,
              ffn2_pre_norm, ffn2_w_gate, ffn2_w_up, ffn2_w_down, ffn2_post_norm):
    h = x
    for l in range(DEPTH):
        f = swiglu(rms_norm(h, ffn1_pre_norm[l]), ffn1_w_gate[l], ffn1_w_up[l], ffn1_w_down[l])
        h = h + MACARON_WEIGHT * rms_norm(f, ffn1_post_norm[l])

        u = rms_norm(h, mix_pre_norm[l])
        proj = u @ w_in[l]
        p_gla = proj[..., :GLA_COLS]
        p_rwkv = proj[..., GLA_COLS:GLA_COLS + RWKV_COLS]
        gate_gla, gate_rwkv = _split(proj[..., GLA_COLS + RWKV_COLS:], (D_MODEL, D_MODEL))
        y_gla = gla_branch(p_gla, gla_gate_up[l], gla_gate_bias[l], gla_out_norm[l]) @ w_up_gla[l]
        y_rwkv = rwkv7_branch(p_rwkv, rwkv_shift_mix[l], rwkv_w0[l], rwkv_w2[l], rwkv_a0[l], rwkv_a2[l],
                              rwkv_g2[l], rwkv_k_k[l], rwkv_k_a[l], rwkv_r_k[l], rwkv_ln_w[l],
                              rwkv_ln_b[l]) @ w_up_rwkv[l]
        merged = jax.nn.sigmoid(gate_gla) * y_gla + jax.nn.sigmoid(gate_rwkv) * y_rwkv
        h = h + rms_norm(merged @ w_out[l], mix_post_norm[l])

        f = swiglu(rms_norm(h, ffn2_pre_norm[l]), ffn2_w_gate[l], ffn2_w_up[l], ffn2_w_down[l])
        h = h + MACARON_WEIGHT * rms_norm(f, ffn2_post_norm[l])
    return h
```

```python
import functools

import jax
import jax.numpy as jnp
from jax import lax
from jax.experimental import pallas as pl
from jax.experimental.pallas import tpu as pltpu

F32 = jnp.float32
BF16 = jnp.bfloat16

NORM_EPS = 1e-6
RWKV_LN_EPS = 64e-5
MACARON_WEIGHT = 0.5
GLA_GATE_NORM = 16.0
GLA_GATE_RANK = 16
RWKV_HEAD = 64
RWKV_GATE_LORA = 480

LANES = 128
CHUNK = 64
SUB = 16
RWKV_GROUP = 4
VMEM_LIMIT = 56 * 1024 * 1024


def _cparams(sem):
    return pltpu.CompilerParams(dimension_semantics=sem, vmem_limit_bytes=VMEM_LIMIT)


def _dot(a, b):
    return jnp.dot(a.astype(BF16), b.astype(BF16), preferred_element_type=F32)


def _dot_nt(a, b):
    return lax.dot_general(a.astype(BF16), b.astype(BF16), (((1,), (1,)), ((), ())),
                           preferred_element_type=F32)


def _dot_tn(a, b):
    return lax.dot_general(a.astype(BF16), b.astype(BF16), (((0,), (0,)), ((), ())),
                           preferred_element_type=F32)


def _dot_split(m01, x):
    hi = x.astype(BF16)
    lo = (x - hi.astype(F32)).astype(BF16)
    return (jnp.dot(m01, hi, preferred_element_type=F32)
            + jnp.dot(m01, lo, preferred_element_type=F32))


def _rms(x, g):
    ms = jnp.mean(x * x, axis=-1, keepdims=True)
    return x * lax.rsqrt(ms + NORM_EPS) * g


def _sigmoid(x):
    return 1.0 / (1.0 + jnp.exp(-x))


def _softplus(x):
    return jnp.maximum(x, 0.0) + jnp.log(1.0 + jnp.exp(-jnp.abs(x)))


def _norm_cast_kernel(x_ref, g_ref, o_ref):
    o_ref[...] = _rms(x_ref[...], g_ref[...]).astype(o_ref.dtype)


def _norm_cast(x, g, bt):
    t, d = x.shape
    return pl.pallas_call(
        _norm_cast_kernel,
        grid=(t // bt,),
        in_specs=[pl.BlockSpec((bt, d), lambda i: (i, 0)),
                  pl.BlockSpec((1, d), lambda i: (0, 0))],
        out_specs=pl.BlockSpec((bt, d), lambda i: (i, 0)),
        out_shape=jax.ShapeDtypeStruct((t, d), BF16),
        compiler_params=_cparams(("parallel",)),
    )(x, g)


def _post_pre_kernel(h_ref, f_ref, gpost_ref, gpre_ref, hout_ref, u_ref, *, alpha):
    h = h_ref[...] + alpha * _rms(f_ref[...], gpost_ref[...])
    hout_ref[...] = h
    u_ref[...] = _rms(h, gpre_ref[...]).astype(u_ref.dtype)


def _post_pre(h, f, g_post, g_pre, alpha, bt):
    t, d = h.shape
    row = pl.BlockSpec((bt, d), lambda i: (i, 0))
    vec = pl.BlockSpec((1, d), lambda i: (0, 0))
    return pl.pallas_call(
        functools.partial(_post_pre_kernel, alpha=alpha),
        grid=(t // bt,),
        in_specs=[row, row, vec, vec],
        out_specs=[row, row],
        out_shape=[jax.ShapeDtypeStruct((t, d), F32), jax.ShapeDtypeStruct((t, d), BF16)],
        compiler_params=_cparams(("parallel",)),
    )(h, f, g_post, g_pre)


def _post_kernel(h_ref, f_ref, gpost_ref, hout_ref, *, alpha):
    hout_ref[...] = h_ref[...] + alpha * _rms(f_ref[...], gpost_ref[...])


def _post(h, f, g_post, alpha, bt):
    t, d = h.shape
    row = pl.BlockSpec((bt, d), lambda i: (i, 0))
    vec = pl.BlockSpec((1, d), lambda i: (0, 0))
    return pl.pallas_call(
        functools.partial(_post_kernel, alpha=alpha),
        grid=(t // bt,),
        in_specs=[row, row, vec],
        out_specs=row,
        out_shape=jax.ShapeDtypeStruct((t, d), F32),
        compiler_params=_cparams(("parallel",)),
    )(h, f, g_post)


def _matmul_kernel(x_ref, w_ref, o_ref):
    o_ref[...] = jnp.dot(x_ref[...], w_ref[...], preferred_element_type=F32).astype(o_ref.dtype)


def _matmul(x, w, out_dtype, bm, bn):
    m, k = x.shape
    n = w.shape[1]
    return pl.pallas_call(
        _matmul_kernel,
        grid=(m // bm, pl.cdiv(n, bn)),
        in_specs=[pl.BlockSpec((bm, k), lambda i, j: (i, 0)),
                  pl.BlockSpec((k, bn), lambda i, j: (0, j))],
        out_specs=pl.BlockSpec((bm, bn), lambda i, j: (i, j)),
        out_shape=jax.ShapeDtypeStruct((m, n), out_dtype),
        compiler_params=_cparams(("parallel", "arbitrary")),
    )(x, w)


def _gate_up_kernel(x_ref, wg_ref, wu_ref, o_ref):
    x = x_ref[...]
    g = jnp.dot(x, wg_ref[...], preferred_element_type=F32)
    u = jnp.dot(x, wu_ref[...], preferred_element_type=F32)
    o_ref[...] = (g * _sigmoid(g) * u).astype(o_ref.dtype)


def _gate_up(x, wg, wu, bm, bn):
    m, k = x.shape
    n = wg.shape[1]
    wspec = pl.BlockSpec((k, bn), lambda i, j: (0, j))
    return pl.pallas_call(
        _gate_up_kernel,
        grid=(m // bm, pl.cdiv(n, bn)),
        in_specs=[pl.BlockSpec((bm, k), lambda i, j: (i, 0)), wspec, wspec],
        out_specs=pl.BlockSpec((bm, bn), lambda i, j: (i, j)),
        out_shape=jax.ShapeDtypeStruct((m, n), BF16),
        compiler_params=_cparams(("parallel", "arbitrary")),
    )(x, wg, wu)


def _merge_kernel(og_ref, or_ref, wg_ref, wr_ref, gg_ref, gr_ref, o_ref):
    yg = jnp.dot(og_ref[...], wg_ref[...], preferred_element_type=F32)
    yr = jnp.dot(or_ref[...], wr_ref[...], preferred_element_type=F32)
    o_ref[...] = (_sigmoid(gg_ref[...]) * yg + _sigmoid(gr_ref[...]) * yr).astype(o_ref.dtype)


def _merge(o_gla, o_rwkv, w_up_gla, w_up_rwkv, proj, col_gate_gla, col_gate_rwkv, bm, bn):
    m, k = o_gla.shape
    n = w_up_gla.shape[1]
    xspec = pl.BlockSpec((bm, k), lambda i, j: (i, 0))
    wspec = pl.BlockSpec((k, bn), lambda i, j: (0, j))
    ga, gb = col_gate_gla // bn, col_gate_rwkv // bn
    return pl.pallas_call(
        _merge_kernel,
        grid=(m // bm, n // bn),
        in_specs=[xspec, xspec, wspec, wspec,
                  pl.BlockSpec((bm, bn), lambda i, j: (i, ga + j)),
                  pl.BlockSpec((bm, bn), lambda i, j: (i, gb + j))],
        out_specs=pl.BlockSpec((bm, bn), lambda i, j: (i, j)),
        out_shape=jax.ShapeDtypeStruct((m, n), BF16),
        compiler_params=_cparams(("parallel", "arbitrary")),
    )(o_gla, o_rwkv, w_up_gla, w_up_rwkv, proj, proj)


def _gla_kernel(q_ref, k_ref, v_ref, og_ref, gd_ref, gup_ref, gbias_ref, onorm_ref, tri_ref,
                o_ref, state_ref, *, dk):
    c, nsub = CHUNK, CHUNK // SUB

    @pl.when(pl.program_id(2) == 0)
    def _():
        state_ref[...] = jnp.zeros_like(state_ref)

    x = _dot(gd_ref[...], gup_ref[...]) + gbias_ref[...]
    log_a = -_softplus(-x) * (1.0 / GLA_GATE_NORM)
    b = _dot_split(tri_ref[...], log_a)

    q = q_ref[...] * (dk ** -0.5)
    k = k_ref[...]
    v = v_ref[...]
    st = state_ref[...]

    row = lax.broadcasted_iota(jnp.int32, (c, c), 0)
    lane = lax.broadcasted_iota(jnp.int32, (c, c), 1)
    blk0 = row & -SUB
    d = lane - blk0
    dc = jnp.where(d >= 0, jnp.where(d <= (row & (SUB - 1)), d, -1), -1)
    mask_off = (lane < blk0).astype(F32)

    bref = jnp.concatenate(
        [jnp.zeros((SUB, dk), F32)]
        + [jnp.broadcast_to(b[i * SUB - 1:i * SUB, :], (SUB, dk)) for i in range(1, nsub)], axis=0)
    qhat = q * jnp.exp(b - bref)
    blocks = [jnp.zeros((SUB, c), F32)]
    for i in range(1, nsub):
        khat = k * jnp.exp(jnp.minimum(b[i * SUB - 1:i * SUB, :] - b, 0.0))
        blocks.append(_dot_nt(qhat[i * SUB:(i + 1) * SUB, :], khat))
    attn = jnp.concatenate(blocks, axis=0) * mask_off

    k3 = k.reshape(nsub, SUB, dk)
    b3 = b.reshape(nsub, SUB, dk)
    for j in range(SUB):
        kj = jnp.broadcast_to(k3[:, j:j + 1, :], (nsub, SUB, dk)).reshape(c, dk)
        bj = jnp.broadcast_to(b3[:, j:j + 1, :], (nsub, SUB, dk)).reshape(c, dk)
        col = jnp.sum(q * kj * jnp.exp(jnp.minimum(b - bj, 0.0)), axis=-1, keepdims=True)
        attn = jnp.where(dc == j, col, attn)

    o = _dot(attn, v) + _dot_nt(q * jnp.exp(b), st)
    b_end = b[c - 1:c, :]
    state_ref[...] = st * jnp.exp(b_end) + _dot_tn(v, k * jnp.exp(b_end - b))

    o = _rms(o, onorm_ref[...])
    og = og_ref[...]
    o_ref[...] = (o * (og * _sigmoid(og))).astype(o_ref.dtype)


def _gla(proj, gate_up, gate_bias, out_norm, tri, cols, batch, seq, heads, dk, dv):
    t = batch * seq
    nc = seq // CHUNK

    def rows(b, h, c):
        return b * nc + c

    cq, ck, cv, cog, cgd = (cols[n] for n in ("gla_q", "gla_k", "gla_v", "gla_og", "gla_gd"))
    return pl.pallas_call(
        functools.partial(_gla_kernel, dk=dk),
        grid=(batch, heads, nc),
        in_specs=[
            pl.BlockSpec((CHUNK, dk), lambda b, h, c: (rows(b, h, c), cq // dk + h)),
            pl.BlockSpec((CHUNK, dk), lambda b, h, c: (rows(b, h, c), ck // dk + h)),
            pl.BlockSpec((CHUNK, dv), lambda b, h, c: (rows(b, h, c), cv // dv + h)),
            pl.BlockSpec((CHUNK, dv), lambda b, h, c: (rows(b, h, c), cog // dv + h)),
            pl.BlockSpec((CHUNK, LANES), lambda b, h, c: (rows(b, h, c), cgd // LANES)),
            pl.BlockSpec((LANES, dk), lambda b, h, c: (0, h)),
            pl.BlockSpec((1, dk), lambda b, h, c: (0, h)),
            pl.BlockSpec((1, dv), lambda b, h, c: (0, 0)),
            pl.BlockSpec((CHUNK, CHUNK), lambda b, h, c: (0, 0)),
        ],
        out_specs=pl.BlockSpec((CHUNK, dv), lambda b, h, c: (rows(b, h, c), h)),
        out_shape=jax.ShapeDtypeStruct((t, heads * dv), BF16),
        scratch_shapes=[pltpu.VMEM((dv, dk), F32)],
        compiler_params=_cparams(("parallel", "parallel", "arbitrary")),
    )(proj, proj, proj, proj, proj, gate_up, gate_bias, out_norm, tri)


def _rwkv_kernel(r_ref, k_ref, v_ref, gd_ref, wa_ref, rp_ref, kp_ref, vp_ref, gdp_ref, wap_ref,
                 mur_ref, muk_ref, muv_ref, mugd_ref, muwa_ref,
                 w0_ref, w2_ref, a0_ref, a2_ref, g2_ref, kk_ref, ka_ref, rk_ref, lnw_ref, lnb_ref,
                 tri_ref, ones_ref, mstrict_ref, mincl_ref, mblk_ref,
                 o_ref, state_ref, *, lora):
    c, g = CHUNK, RWKV_GROUP
    first = pl.program_id(2) == 0

    @pl.when(first)
    def _():
        state_ref[...] = jnp.zeros_like(state_ref)

    keep_prev = jnp.where(first, 0.0, 1.0)

    def shifted(cur_ref, prev_ref, mu_ref):
        p = cur_ref[...]
        prev_row = prev_ref[7:8, :] * keep_prev
        is_row0 = lax.broadcasted_iota(jnp.int32, p.shape, 0) == 0
        p_prev = jnp.where(is_row0, prev_row, pltpu.roll(p, 1, axis=0))
        return p + (p_prev - p) * mu_ref[...]

    r = shifted(r_ref, rp_ref, mur_ref)
    k = shifted(k_ref, kp_ref, muk_ref)
    v = shifted(v_ref, vp_ref, muv_ref)
    gd = shifted(gd_ref, gdp_ref, mugd_ref)
    wa = shifted(wa_ref, wap_ref, muwa_ref)

    w = -_softplus(-(w0_ref[...] + _dot(jnp.tanh(wa[:, :lora]), w2_ref[...]))) - 0.5
    ld = -jnp.exp(w)
    a = _sigmoid(a0_ref[...] + _dot(wa[:, lora:], a2_ref[...]))
    gate = _dot(_sigmoid(gd), g2_ref[...])

    ones_bd = ones_ref[...]
    m_blk = mblk_ref[...]
    kkr = k * kk_ref[...]
    kk = kkr * lax.rsqrt(jnp.maximum(_dot(kkr * kkr, ones_bd), 1e-24))
    k2 = k * (1.0 + (a - 1.0) * ka_ref[...])
    bw = a * kk

    cum = _dot_split(tri_ref[...], ld)
    c_end = cum[c - 1:c, :]
    e_neg = jnp.exp(-cum)
    e_end = jnp.exp(c_end - cum)
    rt = r * jnp.exp(cum)
    kkt = kk * jnp.exp(cum - ld)
    km = k2 * e_neg
    bm = bw * e_neg

    def stack(x):
        return jnp.concatenate([x] * g, axis=0) * m_blk

    def tile(x):
        return jnp.concatenate([x] * g, axis=0)

    def fold(x):
        out = x[0:c]
        for i in range(1, g):
            out = out + x[i * c:(i + 1) * c]
        return out

    n = g * c
    kkt_st, rt_st = stack(kkt), stack(rt)
    aa = _dot_nt(jnp.concatenate([kkt_st, rt_st], axis=0),
                 jnp.concatenate([tile(bm), tile(km)], axis=0))
    m_strict, m_incl = mstrict_ref[...], mincl_ref[...]
    a_ab = aa[:n, :n] * m_strict
    a_ak = aa[:n, n:] * m_strict
    a_rb = aa[n:, :n] * m_incl
    a_rk = aa[n:, n:] * m_incl

    x = (m_incl - m_strict) - a_ab
    p = _dot(a_ab, a_ab)
    span = 2
    while span < c:
        x = x + _dot(x, p)
        span *= 2
        if span < c:
            p = _dot(p, p)

    st = state_ref[...]
    v_t = tile(v)
    u = _dot(x, -(_dot_nt(kkt_st, st) + _dot(a_ak, v_t) * m_blk))
    y = fold(_dot_nt(rt_st, st) + _dot(a_rb, u) + _dot(a_rk, v_t) * m_blk)
    state_ref[...] = st * jnp.exp(c_end) + (_dot_tn(fold(u), bw * e_end) + _dot_tn(v, k2 * e_end)) * m_blk

    inv_n = 1.0 / RWKV_HEAD
    mu = _dot(y, ones_bd) * inv_n
    dy = y - mu
    var = _dot(dy * dy, ones_bd) * inv_n
    yn = dy * lax.rsqrt(var + RWKV_LN_EPS) * lnw_ref[...] + lnb_ref[...]
    bonus = _dot(r * k2 * rk_ref[...], ones_bd) * v
    o_ref[...] = ((yn + bonus) * gate).astype(o_ref.dtype)


def _rwkv(proj, mus, w0, w2, a0, a2, g2, k_k, k_a, r_k, ln_w, ln_b, consts, cols, batch, seq, width, lora):
    t = batch * seq
    nc = seq // CHUNK
    gw = RWKV_GROUP * RWKV_HEAD
    groups = width // gw
    gdw = g2.shape[0]
    waw = 2 * lora

    def rows(b, q, c):
        return b * nc + c

    def prev_rows(b, q, c):
        return jnp.maximum((b * seq + c * CHUNK) // 8 - 1, 0)

    def cur(w_, col, per_group):
        if per_group:
            return pl.BlockSpec((CHUNK, w_), lambda b, q, c: (rows(b, q, c), col // w_ + q))
        return pl.BlockSpec((CHUNK, w_), lambda b, q, c: (rows(b, q, c), col // w_))

    def prev(w_, col, per_group):
        if per_group:
            return pl.BlockSpec((8, w_), lambda b, q, c: (prev_rows(b, q, c), col // w_ + q))
        return pl.BlockSpec((8, w_), lambda b, q, c: (prev_rows(b, q, c), col // w_))

    def gvec():
        return pl.BlockSpec((1, gw), lambda b, q, c: (0, q))

    def full(shape):
        return pl.BlockSpec(shape, lambda b, q, c: (0, 0))

    n = RWKV_GROUP * CHUNK
    in_specs = [
        cur(gw, cols["rw_r"], True), cur(gw, cols["rw_k"], True), cur(gw, cols["rw_v"], True),
        cur(gdw, cols["rw_gd"], False), cur(waw, cols["rw_wa"], False),
        prev(gw, cols["rw_r"], True), prev(gw, cols["rw_k"], True), prev(gw, cols["rw_v"], True),
        prev(gdw, cols["rw_gd"], False), prev(waw, cols["rw_wa"], False),
        gvec(), gvec(), gvec(), full((1, gdw)), full((1, waw)),
        gvec(), pl.BlockSpec((lora, gw), lambda b, q, c: (0, q)),
        gvec(), pl.BlockSpec((lora, gw), lambda b, q, c: (0, q)),
        pl.BlockSpec((gdw, gw), lambda b, q, c: (0, q)),
        gvec(), gvec(), gvec(), gvec(), gvec(),
        full((CHUNK, CHUNK)), full((gw, gw)), full((n, n)), full((n, n)), full((n, gw)),
    ]
    return pl.pallas_call(
        functools.partial(_rwkv_kernel, lora=lora),
        grid=(batch, groups, nc),
        in_specs=in_specs,
        out_specs=pl.BlockSpec((CHUNK, gw), lambda b, q, c: (rows(b, q, c), q)),
        out_shape=jax.ShapeDtypeStruct((t, width), BF16),
        scratch_shapes=[pltpu.VMEM((gw, gw), F32)],
        compiler_params=_cparams(("parallel", "parallel", "arbitrary")),
    )(proj, proj, proj, proj, proj, proj, proj, proj, proj, proj,
      mus["r"], mus["k"], mus["v"], mus["gd"], mus["wa"],
      w0, w2, a0, a2, g2, k_k, k_a, r_k, ln_w, ln_b,
      consts["tri"], consts["ones_bd"], consts["m_strict"], consts["m_incl"], consts["m_blk"])


def _pad_cols(w, n):
    return jnp.pad(w, ((0, 0), (0, n - w.shape[1])))


def _pad_rows(w, n):
    return jnp.pad(w, ((0, n - w.shape[0]), (0, 0)))


def _layout(d_model):
    kw, vw, rw = d_model // 4, d_model // 2, d_model // 2
    lora = 128
    gdw = 512
    names = [("gla_q", kw), ("gla_k", kw), ("gla_v", vw), ("gla_og", vw),
             ("rw_r", rw), ("rw_k", rw), ("rw_v", rw),
             ("gate_gla", d_model), ("gate_rwkv", d_model),
             ("rw_gd", gdw), ("rw_wa", 2 * lora), ("gla_gd", LANES)]
    cols, off = {}, 0
    for name, w in names:
        assert off % w == 0, (name, off, w)
        cols[name] = off
        off += w
    return cols, off


def _chunk_consts():
    c, g, hn = CHUNK, RWKV_GROUP, RWKV_HEAD
    n = g * c
    i = jnp.arange(n)
    same = (i[:, None] // c) == (i[None, :] // c)
    ti, tj = i[:, None] % c, i[None, :] % c
    lane = jnp.arange(g * hn)
    return {
        "tri": jnp.tril(jnp.ones((c, c), F32)).astype(BF16),
        "ones_bd": ((lane[:, None] // hn) == (lane[None, :] // hn)).astype(BF16),
        "m_strict": (same & (ti > tj)).astype(F32),
        "m_incl": (same & (ti >= tj)).astype(F32),
        "m_blk": ((i[:, None] // c) == (lane[None, :] // hn)).astype(F32),
    }


def kernel(x, ffn1_pre_norm, ffn1_w_gate, ffn1_w_up, ffn1_w_down, ffn1_post_norm, mix_pre_norm, w_in, gla_gate_up, gla_gate_bias, gla_out_norm, rwkv_shift_mix, rwkv_w0, rwkv_w2, rwkv_a0, rwkv_a2, rwkv_g2, rwkv_k_k, rwkv_k_a, rwkv_r_k, rwkv_ln_w, rwkv_ln_b, w_up_gla, w_up_rwkv, w_out, mix_post_norm, ffn2_pre_norm, ffn2_w_gate, ffn2_w_up, ffn2_w_down, ffn2_post_norm):
    batch, seq, d_model = x.shape
    depth = ffn1_pre_norm.shape[0]
    t = batch * seq
    assert seq % CHUNK == 0 and d_model % 1024 == 0

    gla_kw, gla_vw, rw_w = d_model // 4, d_model // 2, d_model // 2
    gla_heads = max(4, d_model // 512)
    dk, dv = gla_kw // gla_heads, gla_vw // gla_heads
    lora = rwkv_w2.shape[1]
    g_lora = rwkv_g2.shape[1]
    assert lora == 128 and g_lora <= 512 and dk % LANES == 0

    cols, n_proj = _layout(d_model)
    n_proj_pad = -(-n_proj // 1024) * 1024
    consts = _chunk_consts()

    bm = min(1024, t)
    bm_down = min(512, t)
    bt = min(256, t)

    def row(v):
        return v.reshape(1, -1).astype(F32)

    h = x.reshape(t, d_model)
    for l in range(depth):
        xn = _norm_cast(h, row(ffn1_pre_norm[l]), bt)
        hid = _gate_up(xn, ffn1_w_gate[l].astype(BF16), ffn1_w_up[l].astype(BF16), bm, 512)
        f = _matmul(hid, ffn1_w_down[l].astype(BF16), F32, bm_down, 512)
        h, u = _post_pre(h, f, row(ffn1_post_norm[l]), row(mix_pre_norm[l]), MACARON_WEIGHT, bt)

        wi = w_in[l]
        o_gla, o_rw, o_gate = 0, 2 * gla_kw + 2 * gla_vw + GLA_GATE_RANK, None
        gq, gk, gv, ggd, gog = jnp.split(wi[:, :o_rw], [gla_kw, 2 * gla_kw, 2 * gla_kw + gla_vw,
                                                       2 * gla_kw + gla_vw + GLA_GATE_RANK], axis=1)
        rw_sizes = [rw_w, lora, rw_w, rw_w, lora, g_lora]
        rw_off = [o_rw]
        for s in rw_sizes:
            rw_off.append(rw_off[-1] + s)
        rr, rwd, rk, rv, rad, rgd = (wi[:, rw_off[i]:rw_off[i + 1]] for i in range(6))
        gates = wi[:, rw_off[-1]:]
        w_proj = jnp.concatenate(
            [gq, gk, gv, gog, rr, rk, rv, gates, _pad_cols(rgd, 512), rwd, rad, _pad_cols(ggd, LANES),
             jnp.zeros((d_model, n_proj_pad - n_proj), F32)], axis=1).astype(BF16)
        proj = _matmul(u, w_proj, F32, bm, 1024)

        mu = rwkv_shift_mix[l]
        mu_off = [o - o_rw for o in rw_off]
        mu_r, mu_wd, mu_k, mu_v, mu_ad, mu_gd = (mu[mu_off[i]:mu_off[i + 1]] for i in range(6))
        mus = {"r": row(mu_r), "k": row(mu_k), "v": row(mu_v),
               "gd": row(jnp.pad(mu_gd, (0, 512 - g_lora))),
               "wa": row(jnp.concatenate([mu_wd, mu_ad]))}

        o_gla = _gla(proj, _pad_rows(gla_gate_up[l], LANES).astype(BF16), row(gla_gate_bias[l]),
                     row(gla_out_norm[l]), consts["tri"], cols, batch, seq, gla_heads, dk, dv)
        o_rwkv = _rwkv(proj, mus, row(rwkv_w0[l]), rwkv_w2[l].astype(BF16), row(rwkv_a0[l]),
                       rwkv_a2[l].astype(BF16), _pad_rows(rwkv_g2[l], 512).astype(BF16),
                       row(rwkv_k_k[l]), row(rwkv_k_a[l]), row(rwkv_r_k[l]), row(rwkv_ln_w[l]),
                       row(rwkv_ln_b[l]), consts, cols, batch, seq, rw_w, lora)

        merged = _merge(o_gla, o_rwkv, w_up_gla[l].astype(BF16), w_up_rwkv[l].astype(BF16), proj,
                        cols["gate_gla"], cols["gate_rwkv"], bm, 512)
        mixed = _matmul(merged, w_out[l].astype(BF16), F32, bm, 1024)
        h, u = _post_pre(h, mixed, row(mix_post_norm[l]), row(ffn2_pre_norm[l]), 1.0, bt)

        hid = _gate_up(u, ffn2_w_gate[l].astype(BF16), ffn2_w_up[l].astype(BF16), bm, 512)
        f = _matmul(hid, ffn2_w_down[l].astype(BF16), F32, bm_down, 512)
        h = _post(h, f, row(ffn2_post_norm[l]), MACARON_WEIGHT, bt)
    return h.reshape(batch, seq, d_model)
```

```python
import functools

import jax
import jax.numpy as jnp
from jax import lax
from jax.experimental import pallas as pl
from jax.experimental.pallas import tpu as pltpu

F32 = jnp.float32
BF16 = jnp.bfloat16

NORM_EPS = 1e-6
RWKV_LN_EPS = 64e-5
MACARON_WEIGHT = 0.5
GLA_GATE_NORM = 16.0
GLA_GATE_RANK = 16
RWKV_HEAD = 64
RWKV_GATE_LORA = 480

LANES = 128
CHUNK = 64
SUB_LOG2 = 4
SUB = 1 << SUB_LOG2
RWKV_GROUP = 4
VMEM_LIMIT = 56 * 1024 * 1024


def _cparams(sem):
    return pltpu.CompilerParams(dimension_semantics=sem, vmem_limit_bytes=VMEM_LIMIT)


def _dot(a, b):
    return jnp.dot(a.astype(BF16), b.astype(BF16), preferred_element_type=F32)


def _dot_nt(a, b):
    return lax.dot_general(a.astype(BF16), b.astype(BF16), (((1,), (1,)), ((), ())),
                           preferred_element_type=F32)


def _dot_tn(a, b):
    return lax.dot_general(a.astype(BF16), b.astype(BF16), (((0,), (0,)), ((), ())),
                           preferred_element_type=F32)


def _dot_split(m01, x):
    hi = x.astype(BF16)
    lo = (x - hi.astype(F32)).astype(BF16)
    return (jnp.dot(m01, hi, preferred_element_type=F32)
            + jnp.dot(m01, lo, preferred_element_type=F32))


def _rms(x, g):
    ms = jnp.mean(x * x, axis=-1, keepdims=True)
    return x * lax.rsqrt(ms + NORM_EPS) * g


def _sigmoid(x):
    return 1.0 / (1.0 + jnp.exp(-x))


def _softplus(x):
    return jnp.maximum(x, 0.0) + jnp.log(1.0 + jnp.exp(-jnp.abs(x)))


def _norm_cast_kernel(x_ref, g_ref, o_ref):
    o_ref[...] = _rms(x_ref[...], g_ref[...]).astype(o_ref.dtype)


def _norm_cast(x, g, bt):
    t, d = x.shape
    return pl.pallas_call(
        _norm_cast_kernel,
        grid=(t // bt,),
        in_specs=[pl.BlockSpec((bt, d), lambda i: (i, 0)),
                  pl.BlockSpec((1, d), lambda i: (0, 0))],
        out_specs=pl.BlockSpec((bt, d), lambda i: (i, 0)),
        out_shape=jax.ShapeDtypeStruct((t, d), BF16),
        compiler_params=_cparams(("parallel",)),
    )(x, g)


def _post_pre_kernel(h_ref, f_ref, gpost_ref, gpre_ref, hout_ref, u_ref, *, alpha):
    h = h_ref[...] + alpha * _rms(f_ref[...], gpost_ref[...])
    hout_ref[...] = h
    u_ref[...] = _rms(h, gpre_ref[...]).astype(u_ref.dtype)


def _post_pre(h, f, g_post, g_pre, alpha, bt):
    t, d = h.shape
    row = pl.BlockSpec((bt, d), lambda i: (i, 0))
    vec = pl.BlockSpec((1, d), lambda i: (0, 0))
    return pl.pallas_call(
        functools.partial(_post_pre_kernel, alpha=alpha),
        grid=(t // bt,),
        in_specs=[row, row, vec, vec],
        out_specs=[row, row],
        out_shape=[jax.ShapeDtypeStruct((t, d), F32), jax.ShapeDtypeStruct((t, d), BF16)],
        compiler_params=_cparams(("parallel",)),
    )(h, f, g_post, g_pre)


def _post_kernel(h_ref, f_ref, gpost_ref, hout_ref, *, alpha):
    hout_ref[...] = h_ref[...] + alpha * _rms(f_ref[...], gpost_ref[...])


def _post(h, f, g_post, alpha, bt):
    t, d = h.shape
    row = pl.BlockSpec((bt, d), lambda i: (i, 0))
    vec = pl.BlockSpec((1, d), lambda i: (0, 0))
    return pl.pallas_call(
        functools.partial(_post_kernel, alpha=alpha),
        grid=(t // bt,),
        in_specs=[row, row, vec],
        out_specs=row,
        out_shape=jax.ShapeDtypeStruct((t, d), F32),
        compiler_params=_cparams(("parallel",)),
    )(h, f, g_post)


def _matmul_kernel(x_ref, w_ref, o_ref):
    o_ref[...] = jnp.dot(x_ref[...], w_ref[...], preferred_element_type=F32).astype(o_ref.dtype)


def _matmul(x, w, out_dtype, bm, bn):
    m, k = x.shape
    n = w.shape[1]
    return pl.pallas_call(
        _matmul_kernel,
        grid=(m // bm, pl.cdiv(n, bn)),
        in_specs=[pl.BlockSpec((bm, k), lambda i, j: (i, 0)),
                  pl.BlockSpec((k, bn), lambda i, j: (0, j))],
        out_specs=pl.BlockSpec((bm, bn), lambda i, j: (i, j)),
        out_shape=jax.ShapeDtypeStruct((m, n), out_dtype),
        compiler_params=_cparams(("parallel", "arbitrary")),
    )(x, w)


def _gate_up_kernel(x_ref, wg_ref, wu_ref, o_ref):
    x = x_ref[...]
    g = jnp.dot(x, wg_ref[...], preferred_element_type=F32)
    u = jnp.dot(x, wu_ref[...], preferred_element_type=F32)
    o_ref[...] = (g * _sigmoid(g) * u).astype(o_ref.dtype)


def _gate_up(x, wg, wu, bm, bn):
    m, k = x.shape
    n = wg.shape[1]
    wspec = pl.BlockSpec((k, bn), lambda i, j: (0, j))
    return pl.pallas_call(
        _gate_up_kernel,
        grid=(m // bm, pl.cdiv(n, bn)),
        in_specs=[pl.BlockSpec((bm, k), lambda i, j: (i, 0)), wspec, wspec],
        out_specs=pl.BlockSpec((bm, bn), lambda i, j: (i, j)),
        out_shape=jax.ShapeDtypeStruct((m, n), BF16),
        compiler_params=_cparams(("parallel", "arbitrary")),
    )(x, wg, wu)


def _merge_kernel(og_ref, or_ref, wg_ref, wr_ref, gg_ref, gr_ref, o_ref):
    yg = jnp.dot(og_ref[...], wg_ref[...], preferred_element_type=F32)
    yr = jnp.dot(or_ref[...], wr_ref[...], preferred_element_type=F32)
    o_ref[...] = (_sigmoid(gg_ref[...]) * yg + _sigmoid(gr_ref[...]) * yr).astype(o_ref.dtype)


def _merge(o_gla, o_rwkv, w_up_gla, w_up_rwkv, proj, col_gate_gla, col_gate_rwkv, bm, bn):
    m, k = o_gla.shape
    n = w_up_gla.shape[1]
    xspec = pl.BlockSpec((bm, k), lambda i, j: (i, 0))
    wspec = pl.BlockSpec((k, bn), lambda i, j: (0, j))
    ga, gb = col_gate_gla // bn, col_gate_rwkv // bn
    return pl.pallas_call(
        _merge_kernel,
        grid=(m // bm, n // bn),
        in_specs=[xspec, xspec, wspec, wspec,
                  pl.BlockSpec((bm, bn), lambda i, j: (i, ga + j)),
                  pl.BlockSpec((bm, bn), lambda i, j: (i, gb + j))],
        out_specs=pl.BlockSpec((bm, bn), lambda i, j: (i, j)),
        out_shape=jax.ShapeDtypeStruct((m, n), BF16),
        compiler_params=_cparams(("parallel", "arbitrary")),
    )(o_gla, o_rwkv, w_up_gla, w_up_rwkv, proj, proj)


def _gla_kernel(q_ref, k_ref, v_ref, og_ref, gd_ref, gup_ref, gbias_ref, onorm_ref, tri_ref,
                o_ref, state_ref, *, dk, bb):
    c, nsub = CHUNK, CHUNK // SUB
    rows = bb * c

    @pl.when(pl.program_id(2) == 0)
    def _():
        state_ref[...] = jnp.zeros_like(state_ref)

    def merged(ref):
        return ref[...].reshape(rows, ref.shape[-1])

    def per_row(x):
        return [x[i * c:(i + 1) * c] for i in range(bb)]

    def chunk_row(x, r):
        return jnp.concatenate(
            [jnp.broadcast_to(x[i * c + r:i * c + r + 1, :], (c, x.shape[-1])) for i in range(bb)], axis=0)

    x = _dot(merged(gd_ref), gup_ref[...]) + gbias_ref[...]
    log_a = -_softplus(-x) * (1.0 / GLA_GATE_NORM)
    b = _dot_split(tri_ref[...], log_a)
    q = merged(q_ref) * (dk ** -0.5)
    k = merged(k_ref)

    row = lax.broadcasted_iota(jnp.int32, (rows, c), 0) & (c - 1)
    lane = lax.broadcasted_iota(jnp.int32, (rows, c), 1)
    blk0 = row & -SUB
    d = lane - blk0
    dc = jnp.where(d >= 0, jnp.where(d <= (row & (SUB - 1)), d, -1), -1)
    mask_off = (lane < blk0).astype(F32)

    refs = [chunk_row(b, i * SUB - 1) for i in range(1, nsub)]
    sub_id = (lax.broadcasted_iota(jnp.int32, (rows, dk), 0) & (c - 1)) >> SUB_LOG2
    bref = jnp.zeros((rows, dk), F32)
    for i in range(1, nsub):
        bref = jnp.where(sub_id == i, refs[i - 1], bref)
    qhat = per_row(q * jnp.exp(b - bref))
    khat = [per_row(k * jnp.exp(jnp.minimum(ref - b, 0.0))) for ref in refs]
    attn = jnp.concatenate(
        [jnp.concatenate(
            [jnp.zeros((SUB, c), F32)]
            + [_dot_nt(qhat[r][i * SUB:(i + 1) * SUB, :], khat[i - 1][r]) for i in range(1, nsub)], axis=0)
         for r in range(bb)], axis=0) * mask_off

    k3 = k.reshape(bb * nsub, SUB, dk)
    b3 = b.reshape(bb * nsub, SUB, dk)
    for j in range(SUB):
        kj = jnp.broadcast_to(k3[:, j:j + 1, :], (bb * nsub, SUB, dk)).reshape(rows, dk)
        bj = jnp.broadcast_to(b3[:, j:j + 1, :], (bb * nsub, SUB, dk)).reshape(rows, dk)
        col = jnp.sum(q * kj * jnp.exp(jnp.minimum(b - bj, 0.0)), axis=-1, keepdims=True)
        attn = jnp.where(dc == j, col, attn)

    b_end = chunk_row(b, c - 1)
    attn = per_row(attn)
    qe = per_row(q * jnp.exp(b))
    k_end = per_row(k * jnp.exp(b_end - b))
    st = [state_ref[i] for i in range(bb)]
    o = [_dot(attn[i], v_ref[i]) + _dot_nt(qe[i], st[i]) for i in range(bb)]
    for i in range(bb):
        state_ref[i] = st[i] * jnp.exp(b_end[i * c:i * c + 1, :]) + _dot_tn(v_ref[i], k_end[i])

    o = _rms(jnp.concatenate(o, axis=0), onorm_ref[...])
    og = merged(og_ref)
    o_ref[...] = (o * (og * _sigmoid(og))).astype(o_ref.dtype).reshape(o_ref.shape)


def _gla(proj, gate_up, gate_bias, out_norm, tri, cols, bb, heads, dk, dv):
    batch, seq, _ = proj.shape
    nc = seq // CHUNK
    cq, ck, cv, cog, cgd = (cols[n] for n in ("gla_q", "gla_k", "gla_v", "gla_og", "gla_gd"))
    return pl.pallas_call(
        functools.partial(_gla_kernel, dk=dk, bb=bb),
        grid=(batch // bb, heads, nc),
        in_specs=[
            pl.BlockSpec((bb, CHUNK, dk), lambda b, h, c: (b, c, cq // dk + h)),
            pl.BlockSpec((bb, CHUNK, dk), lambda b, h, c: (b, c, ck // dk + h)),
            pl.BlockSpec((bb, CHUNK, dv), lambda b, h, c: (b, c, cv // dv + h)),
            pl.BlockSpec((bb, CHUNK, dv), lambda b, h, c: (b, c, cog // dv + h)),
            pl.BlockSpec((bb, CHUNK, LANES), lambda b, h, c: (b, c, cgd // LANES)),
            pl.BlockSpec((LANES, dk), lambda b, h, c: (0, h)),
            pl.BlockSpec((1, dk), lambda b, h, c: (0, h)),
            pl.BlockSpec((1, dv), lambda b, h, c: (0, 0)),
            pl.BlockSpec((bb * CHUNK, bb * CHUNK), lambda b, h, c: (0, 0)),
        ],
        out_specs=pl.BlockSpec((bb, CHUNK, dv), lambda b, h, c: (b, c, h)),
        out_shape=jax.ShapeDtypeStruct((batch, seq, heads * dv), BF16),
        scratch_shapes=[pltpu.VMEM((bb, dv, dk), F32)],
        compiler_params=_cparams(("parallel", "parallel", "arbitrary")),
    )(proj, proj, proj, proj, proj, gate_up, gate_bias, out_norm, tri)


def _rwkv_kernel(r_ref, k_ref, v_ref, gd_ref, wa_ref, rp_ref, kp_ref, vp_ref, gdp_ref, wap_ref,
                 mur_ref, muk_ref, muv_ref, mugd_ref, muwa_ref,
                 w0_ref, w2_ref, a0_ref, a2_ref, g2_ref, kk_ref, ka_ref, rk_ref, lnw_ref, lnb_ref,
                 tri_ref, ones_ref, mstrict_ref, mincl_ref, mblk_ref,
                 o_ref, state_ref, *, lora, bb):
    c, g = CHUNK, RWKV_GROUP
    n, rows = g * c, bb * c
    first = pl.program_id(2) == 0

    @pl.when(first)
    def _():
        state_ref[...] = jnp.zeros_like(state_ref)

    keep_prev = jnp.where(first, 0.0, 1.0)

    def per_row(x):
        return [x[i * c:(i + 1) * c] for i in range(bb)]

    def shifted(cur_ref, prev_ref, mu_ref):
        width = cur_ref.shape[-1]
        p = cur_ref[...].reshape(rows, width)
        prev = jnp.concatenate(
            [jnp.broadcast_to(prev_ref[i, 7:8, :] * keep_prev, (c, width)) for i in range(bb)], axis=0)
        is_row0 = (lax.broadcasted_iota(jnp.int32, p.shape, 0) & (c - 1)) == 0
        p_prev = jnp.where(is_row0, prev, pltpu.roll(p, 1, axis=0))
        return p + (p_prev - p) * mu_ref[...]

    r = shifted(r_ref, rp_ref, mur_ref)
    k = shifted(k_ref, kp_ref, muk_ref)
    v = shifted(v_ref, vp_ref, muv_ref)
    gd = shifted(gd_ref, gdp_ref, mugd_ref)
    wa = shifted(wa_ref, wap_ref, muwa_ref)

    w = -_softplus(-(w0_ref[...] + _dot(jnp.tanh(wa[:, :lora]), w2_ref[...]))) - 0.5
    ld = -jnp.exp(w)
    a = _sigmoid(a0_ref[...] + _dot(wa[:, lora:], a2_ref[...]))
    gate = _dot(_sigmoid(gd), g2_ref[...])

    ones_bd = ones_ref[...]
    m_blk = mblk_ref[...]
    kkr = k * kk_ref[...]
    kk = kkr * lax.rsqrt(jnp.maximum(_dot(kkr * kkr, ones_bd), 1e-24))
    k2 = k * (1.0 + (a - 1.0) * ka_ref[...])
    bw = a * kk

    cum = _dot_split(tri_ref[...], ld)
    c_end = [cum[(i + 1) * c - 1:(i + 1) * c, :] for i in range(bb)]
    c_end_rows = jnp.concatenate([jnp.broadcast_to(e, (c, e.shape[-1])) for e in c_end], axis=0)
    e_neg = jnp.exp(-cum)
    e_end = jnp.exp(c_end_rows - cum)
    rt = per_row(r * jnp.exp(cum))
    kkt = per_row(kk * jnp.exp(cum - ld))
    km = per_row(k2 * e_neg)
    bm = per_row(bw * e_neg)
    k_end = per_row(k2 * e_end)
    b_end = per_row(bw * e_end)
    v_rows = per_row(v)

    def stack(x):
        return jnp.concatenate([x] * g, axis=0) * m_blk

    def tile(x):
        return jnp.concatenate([x] * g, axis=0)

    def fold(x):
        out = x[0:c]
        for i in range(1, g):
            out = out + x[i * c:(i + 1) * c]
        return out

    m_strict, m_incl = mstrict_ref[...], mincl_ref[...]
    lhs = [jnp.concatenate([stack(kq), stack(rq)], axis=0).astype(BF16) for kq, rq in zip(kkt, rt)]
    aa = [_dot_nt(l, jnp.concatenate([tile(b_), tile(k_)], axis=0))
          for l, b_, k_ in zip(lhs, bm, km)]
    a_ab = [t[:n, :n] * m_strict for t in aa]
    a_kr = [jnp.concatenate([t[:n, n:] * m_strict, t[n:, n:] * m_incl], axis=0) for t in aa]
    a_rb = [t[n:, :n] * m_incl for t in aa]

    eye = m_incl - m_strict
    x = [eye - t for t in a_ab]
    p = [_dot(t, t) for t in a_ab]
    span = 2
    while span < c:
        span *= 2
        if span < c:
            xp = [_dot(jnp.concatenate([xi, pi], axis=0), pi) for xi, pi in zip(x, p)]
            x = [xi + t[:n] for xi, t in zip(x, xp)]
            p = [t[n:] for t in xp]
        else:
            x = [xi + _dot(xi, pi) for xi, pi in zip(x, p)]

    st = [state_ref[i] for i in range(bb)]
    m_blk2 = jnp.concatenate([m_blk, m_blk], axis=0)
    sk = [_dot_nt(l, s) for l, s in zip(lhs, st)]
    av = [_dot(t, tile(vi)) * m_blk2 for t, vi in zip(a_kr, v_rows)]
    u = [_dot(xi, -(s[:n] + t[:n])) for xi, s, t in zip(x, sk, av)]
    y = [fold(s[n:] + _dot(t, ui) + w_[n:]) for s, t, ui, w_ in zip(sk, a_rb, u, av)]
    for i in range(bb):
        state_ref[i] = st[i] * jnp.exp(c_end[i]) + (
            _dot_tn(fold(u[i]), b_end[i]) + _dot_tn(v_rows[i], k_end[i])) * m_blk

    y = jnp.concatenate(y, axis=0)
    inv_n = 1.0 / RWKV_HEAD
    mu = _dot(y, ones_bd) * inv_n
    dy = y - mu
    var = _dot(dy * dy, ones_bd) * inv_n
    yn = dy * lax.rsqrt(var + RWKV_LN_EPS) * lnw_ref[...] + lnb_ref[...]
    bonus = _dot(r * k2 * rk_ref[...], ones_bd) * v
    o_ref[...] = ((yn + bonus) * gate).astype(o_ref.dtype).reshape(o_ref.shape)


def _rwkv(proj, mus, w0, w2, a0, a2, g2, k_k, k_a, r_k, ln_w, ln_b, consts, cols, bb, width, lora):
    batch, seq, _ = proj.shape
    nc = seq // CHUNK
    gw = RWKV_GROUP * RWKV_HEAD
    groups = width // gw
    gdw = g2.shape[0]
    waw = 2 * lora

    def prev_rows(c):
        return jnp.maximum(c * (CHUNK // 8) - 1, 0)

    def cur(w_, col, per_group):
        if per_group:
            return pl.BlockSpec((bb, CHUNK, w_), lambda b, q, c: (b, c, col // w_ + q))
        return pl.BlockSpec((bb, CHUNK, w_), lambda b, q, c: (b, c, col // w_))

    def prev(w_, col, per_group):
        if per_group:
            return pl.BlockSpec((bb, 8, w_), lambda b, q, c: (b, prev_rows(c), col // w_ + q))
        return pl.BlockSpec((bb, 8, w_), lambda b, q, c: (b, prev_rows(c), col // w_))

    def gvec():
        return pl.BlockSpec((1, gw), lambda b, q, c: (0, q))

    def full(shape):
        return pl.BlockSpec(shape, lambda b, q, c: (0, 0))

    n = RWKV_GROUP * CHUNK
    in_specs = [
        cur(gw, cols["rw_r"], True), cur(gw, cols["rw_k"], True), cur(gw, cols["rw_v"], True),
        cur(gdw, cols["rw_gd"], False), cur(waw, cols["rw_wa"], False),
        prev(gw, cols["rw_r"], True), prev(gw, cols["rw_k"], True), prev(gw, cols["rw_v"], True),
        prev(gdw, cols["rw_gd"], False), prev(waw, cols["rw_wa"], False),
        gvec(), gvec(), gvec(), full((1, gdw)), full((1, waw)),
        gvec(), pl.BlockSpec((lora, gw), lambda b, q, c: (0, q)),
        gvec(), pl.BlockSpec((lora, gw), lambda b, q, c: (0, q)),
        pl.BlockSpec((gdw, gw), lambda b, q, c: (0, q)),
        gvec(), gvec(), gvec(), gvec(), gvec(),
        full((bb * CHUNK, bb * CHUNK)), full((gw, gw)), full((n, n)), full((n, n)), full((n, gw)),
    ]
    return pl.pallas_call(
        functools.partial(_rwkv_kernel, lora=lora, bb=bb),
        grid=(batch // bb, groups, nc),
        in_specs=in_specs,
        out_specs=pl.BlockSpec((bb, CHUNK, gw), lambda b, q, c: (b, c, q)),
        out_shape=jax.ShapeDtypeStruct((batch, seq, width), BF16),
        scratch_shapes=[pltpu.VMEM((bb, gw, gw), F32)],
        compiler_params=_cparams(("parallel", "parallel", "arbitrary")),
    )(proj, proj, proj, proj, proj, proj, proj, proj, proj, proj,
      mus["r"], mus["k"], mus["v"], mus["gd"], mus["wa"],
      w0, w2, a0, a2, g2, k_k, k_a, r_k, ln_w, ln_b,
      consts["tri_rows"], consts["ones_bd"], consts["m_strict"], consts["m_incl"], consts["m_blk"])


def _pad_cols(w, n):
    return jnp.pad(w, ((0, 0), (0, n - w.shape[1])))


def _pad_rows(w, n):
    return jnp.pad(w, ((0, n - w.shape[0]), (0, 0)))


def _layout(d_model):
    kw, vw, rw = d_model // 4, d_model // 2, d_model // 2
    lora = 128
    gdw = 512
    names = [("gla_q", kw), ("gla_k", kw), ("gla_v", vw), ("gla_og", vw),
             ("rw_r", rw), ("rw_k", rw), ("rw_v", rw),
             ("gate_gla", d_model), ("gate_rwkv", d_model),
             ("rw_gd", gdw), ("rw_wa", 2 * lora), ("gla_gd", LANES)]
    cols, off = {}, 0
    for name, w in names:
        assert off % w == 0, (name, off, w)
        cols[name] = off
        off += w
    return cols, off


def _chunk_consts(bb):
    c, g, hn = CHUNK, RWKV_GROUP, RWKV_HEAD
    assert c == hn
    n = g * c
    i = jnp.arange(n)
    same = (i[:, None] // c) == (i[None, :] // c)
    ti, tj = i[:, None] % c, i[None, :] % c
    lane = jnp.arange(g * hn)
    return {
        "tri": jnp.tril(jnp.ones((c, c), F32)).astype(BF16),
        "tri_rows": jnp.kron(jnp.eye(bb, dtype=F32), jnp.tril(jnp.ones((c, c), F32))).astype(BF16),
        "ones_bd": ((lane[:, None] // hn) == (lane[None, :] // hn)).astype(BF16),
        "m_strict": (same & (ti > tj)).astype(F32),
        "m_incl": (same & (ti >= tj)).astype(F32),
        "m_blk": ((i[:, None] // c) == (lane[None, :] // hn)).astype(F32),
    }


def kernel(x, ffn1_pre_norm, ffn1_w_gate, ffn1_w_up, ffn1_w_down, ffn1_post_norm, mix_pre_norm, w_in, gla_gate_up, gla_gate_bias, gla_out_norm, rwkv_shift_mix, rwkv_w0, rwkv_w2, rwkv_a0, rwkv_a2, rwkv_g2, rwkv_k_k, rwkv_k_a, rwkv_r_k, rwkv_ln_w, rwkv_ln_b, w_up_gla, w_up_rwkv, w_out, mix_post_norm, ffn2_pre_norm, ffn2_w_gate, ffn2_w_up, ffn2_w_down, ffn2_post_norm):
    batch, seq, d_model = x.shape
    depth = ffn1_pre_norm.shape[0]
    t = batch * seq
    assert seq % CHUNK == 0 and d_model % 1024 == 0

    gla_kw, gla_vw, rw_w = d_model // 4, d_model // 2, d_model // 2
    gla_heads = max(4, d_model // 512)
    dk, dv = gla_kw // gla_heads, gla_vw // gla_heads
    lora = rwkv_w2.shape[1]
    g_lora = rwkv_g2.shape[1]
    assert lora == 128 and g_lora <= 512 and dk % LANES == 0

    cols, n_proj = _layout(d_model)
    n_proj_pad = -(-n_proj // 1024) * 1024
    bm = min(1024, t)
    bm_down = min(512, t)
    bt = min(256, t)
    bb = next(n for n in (4, 2, 1) if batch % n == 0)
    consts = _chunk_consts(bb)

    def row(v):
        return v.reshape(1, -1).astype(F32)

    h = x.reshape(t, d_model)
    for l in range(depth):
        xn = _norm_cast(h, row(ffn1_pre_norm[l]), bt)
        hid = _gate_up(xn, ffn1_w_gate[l].astype(BF16), ffn1_w_up[l].astype(BF16), bm, 512)
        f = _matmul(hid, ffn1_w_down[l].astype(BF16), F32, bm_down, 512)
        h, u = _post_pre(h, f, row(ffn1_post_norm[l]), row(mix_pre_norm[l]), MACARON_WEIGHT, bt)

        wi = w_in[l]
        o_gla, o_rw, o_gate = 0, 2 * gla_kw + 2 * gla_vw + GLA_GATE_RANK, None
        gq, gk, gv, ggd, gog = jnp.split(wi[:, :o_rw], [gla_kw, 2 * gla_kw, 2 * gla_kw + gla_vw,
                                                       2 * gla_kw + gla_vw + GLA_GATE_RANK], axis=1)
        rw_sizes = [rw_w, lora, rw_w, rw_w, lora, g_lora]
        rw_off = [o_rw]
        for s in rw_sizes:
            rw_off.append(rw_off[-1] + s)
        rr, rwd, rk, rv, rad, rgd = (wi[:, rw_off[i]:rw_off[i + 1]] for i in range(6))
        gates = wi[:, rw_off[-1]:]
        w_proj = jnp.concatenate(
            [gq, gk, gv, gog, rr, rk, rv, gates, _pad_cols(rgd, 512), rwd, rad, _pad_cols(ggd, LANES),
             jnp.zeros((d_model, n_proj_pad - n_proj), F32)], axis=1).astype(BF16)
        proj = _matmul(u, w_proj, F32, bm, 1024)

        mu = rwkv_shift_mix[l]
        mu_off = [o - o_rw for o in rw_off]
        mu_r, mu_wd, mu_k, mu_v, mu_ad, mu_gd = (mu[mu_off[i]:mu_off[i + 1]] for i in range(6))
        mus = {"r": row(mu_r), "k": row(mu_k), "v": row(mu_v),
               "gd": row(jnp.pad(mu_gd, (0, 512 - g_lora))),
               "wa": row(jnp.concatenate([mu_wd, mu_ad]))}

        proj3 = proj.reshape(batch, seq, n_proj_pad)
        o_gla = _gla(proj3, _pad_rows(gla_gate_up[l], LANES).astype(BF16), row(gla_gate_bias[l]),
                     row(gla_out_norm[l]), consts["tri_rows"], cols, bb, gla_heads, dk, dv)
        o_rwkv = _rwkv(proj3, mus, row(rwkv_w0[l]), rwkv_w2[l].astype(BF16), row(rwkv_a0[l]),
                       rwkv_a2[l].astype(BF16), _pad_rows(rwkv_g2[l], 512).astype(BF16),
                       row(rwkv_k_k[l]), row(rwkv_k_a[l]), row(rwkv_r_k[l]), row(rwkv_ln_w[l]),
                       row(rwkv_ln_b[l]), consts, cols, bb, rw_w, lora)

        merged = _merge(o_gla.reshape(t, gla_vw), o_rwkv.reshape(t, rw_w),
                        w_up_gla[l].astype(BF16), w_up_rwkv[l].astype(BF16), proj,
                        cols["gate_gla"], cols["gate_rwkv"], bm, 512)
        mixed = _matmul(merged, w_out[l].astype(BF16), F32, bm, 1024)
        h, u = _post_pre(h, mixed, row(mix_post_norm[l]), row(ffn2_pre_norm[l]), 1.0, bt)

        hid = _gate_up(u, ffn2_w_gate[l].astype(BF16), ffn2_w_up[l].astype(BF16), bm, 512)
        f = _matmul(hid, ffn2_w_down[l].astype(BF16), F32, bm_down, 512)
        h = _post(h, f, row(ffn2_post_norm[l]), MACARON_WEIGHT, bt)
    return h.reshape(batch, seq, d_model)
```

```python
import functools

import jax
import jax.numpy as jnp
from jax import lax
from jax.experimental import pallas as pl
from jax.experimental.pallas import tpu as pltpu

F32 = jnp.float32
BF16 = jnp.bfloat16

NORM_EPS = 1e-6
RWKV_LN_EPS = 64e-5
MACARON_WEIGHT = 0.5
GLA_GATE_NORM = 16.0
GLA_GATE_RANK = 16
RWKV_HEAD = 64
RWKV_GATE_LORA = 480

LANES = 128
CHUNK = 64
SUB_LOG2 = 4
SUB = 1 << SUB_LOG2
RWKV_GROUP = 4
VMEM_LIMIT = 56 * 1024 * 1024
VMEM_LIMIT_MAX = 60 * 1024 * 1024


def _cparams(sem):
    return pltpu.CompilerParams(dimension_semantics=sem, vmem_limit_bytes=VMEM_LIMIT)


def _dot(a, b):
    return jnp.dot(a.astype(BF16), b.astype(BF16), preferred_element_type=F32)


def _dot_nt(a, b):
    return lax.dot_general(a.astype(BF16), b.astype(BF16), (((1,), (1,)), ((), ())),
                           preferred_element_type=F32)


def _dot_tn(a, b):
    return lax.dot_general(a.astype(BF16), b.astype(BF16), (((0,), (0,)), ((), ())),
                           preferred_element_type=F32)


def _dot_split(m01, x):
    hi = x.astype(BF16)
    lo = (x - hi.astype(F32)).astype(BF16)
    return (jnp.dot(m01, hi, preferred_element_type=F32)
            + jnp.dot(m01, lo, preferred_element_type=F32))


def _rms(x, g):
    ms = jnp.mean(x * x, axis=-1, keepdims=True)
    return x * lax.rsqrt(ms + NORM_EPS) * g


def _sigmoid(x):
    return 1.0 / (1.0 + jnp.exp(-x))


def _softplus(x):
    return jnp.maximum(x, 0.0) + jnp.log(1.0 + jnp.exp(-jnp.abs(x)))


def _norm_cast_kernel(x_ref, g_ref, o_ref):
    o_ref[...] = _rms(x_ref[...], g_ref[...]).astype(o_ref.dtype)


def _norm_cast(x, g, bt):
    t, d = x.shape
    return pl.pallas_call(
        _norm_cast_kernel,
        grid=(t // bt,),
        in_specs=[pl.BlockSpec((bt, d), lambda i: (i, 0)),
                  pl.BlockSpec((1, d), lambda i: (0, 0))],
        out_specs=pl.BlockSpec((bt, d), lambda i: (i, 0)),
        out_shape=jax.ShapeDtypeStruct((t, d), BF16),
        compiler_params=_cparams(("parallel",)),
    )(x, g)


def _post_pre_kernel(h_ref, f_ref, gpost_ref, gpre_ref, hout_ref, u_ref, *, alpha):
    h = h_ref[...] + alpha * _rms(f_ref[...], gpost_ref[...])
    hout_ref[...] = h
    u_ref[...] = _rms(h, gpre_ref[...]).astype(u_ref.dtype)


def _post_pre(h, f, g_post, g_pre, alpha, bt):
    t, d = h.shape
    row = pl.BlockSpec((bt, d), lambda i: (i, 0))
    vec = pl.BlockSpec((1, d), lambda i: (0, 0))
    return pl.pallas_call(
        functools.partial(_post_pre_kernel, alpha=alpha),
        grid=(t // bt,),
        in_specs=[row, row, vec, vec],
        out_specs=[row, row],
        out_shape=[jax.ShapeDtypeStruct((t, d), F32), jax.ShapeDtypeStruct((t, d), BF16)],
        compiler_params=_cparams(("parallel",)),
    )(h, f, g_post, g_pre)


def _post_kernel(h_ref, f_ref, gpost_ref, hout_ref, *, alpha):
    hout_ref[...] = h_ref[...] + alpha * _rms(f_ref[...], gpost_ref[...])


def _post(h, f, g_post, alpha, bt):
    t, d = h.shape
    row = pl.BlockSpec((bt, d), lambda i: (i, 0))
    vec = pl.BlockSpec((1, d), lambda i: (0, 0))
    return pl.pallas_call(
        functools.partial(_post_kernel, alpha=alpha),
        grid=(t // bt,),
        in_specs=[row, row, vec],
        out_specs=row,
        out_shape=jax.ShapeDtypeStruct((t, d), F32),
        compiler_params=_cparams(("parallel",)),
    )(h, f, g_post)


def _matmul_kernel(x_ref, w_ref, o_ref):
    o_ref[...] = jnp.dot(x_ref[...], w_ref[...], preferred_element_type=F32).astype(o_ref.dtype)


def _matmul(x, w, out_dtype, bm, bn):
    m, k = x.shape
    n = w.shape[1]
    return pl.pallas_call(
        _matmul_kernel,
        grid=(m // bm, pl.cdiv(n, bn)),
        in_specs=[pl.BlockSpec((bm, k), lambda i, j: (i, 0)),
                  pl.BlockSpec((k, bn), lambda i, j: (0, j))],
        out_specs=pl.BlockSpec((bm, bn), lambda i, j: (i, j)),
        out_shape=jax.ShapeDtypeStruct((m, n), out_dtype),
        compiler_params=_cparams(("parallel", "arbitrary")),
    )(x, w)


def _gate_up_kernel(x_ref, wg_ref, wu_ref, o_ref):
    x = x_ref[...]
    g = jnp.dot(x, wg_ref[...].astype(BF16), preferred_element_type=F32)
    u = jnp.dot(x, wu_ref[...].astype(BF16), preferred_element_type=F32)
    o_ref[...] = (g * _sigmoid(g) * u).astype(o_ref.dtype)


def _gate_up(x, wg, wu, bm, bn):
    m, k = x.shape
    n = wg.shape[1]
    wspec = pl.BlockSpec((k, bn), lambda i, j: (0, j))
    return pl.pallas_call(
        _gate_up_kernel,
        grid=(m // bm, pl.cdiv(n, bn)),
        in_specs=[pl.BlockSpec((bm, k), lambda i, j: (i, 0)), wspec, wspec],
        out_specs=pl.BlockSpec((bm, bn), lambda i, j: (i, j)),
        out_shape=jax.ShapeDtypeStruct((m, n), BF16),
        compiler_params=pltpu.CompilerParams(dimension_semantics=("parallel", "arbitrary"),
                                             vmem_limit_bytes=VMEM_LIMIT_MAX),
    )(x, wg, wu)


def _merge_kernel(og_ref, or_ref, wg_ref, wr_ref, gg_ref, gr_ref, o_ref):
    yg = jnp.dot(og_ref[...], wg_ref[...], preferred_element_type=F32)
    yr = jnp.dot(or_ref[...], wr_ref[...], preferred_element_type=F32)
    o_ref[...] = (_sigmoid(gg_ref[...]) * yg + _sigmoid(gr_ref[...]) * yr).astype(o_ref.dtype)


def _merge(o_gla, o_rwkv, w_up_gla, w_up_rwkv, proj, col_gate_gla, col_gate_rwkv, bm, bn):
    m, k = o_gla.shape
    n = w_up_gla.shape[1]
    xspec = pl.BlockSpec((bm, k), lambda i, j: (i, 0))
    wspec = pl.BlockSpec((k, bn), lambda i, j: (0, j))
    ga, gb = col_gate_gla // bn, col_gate_rwkv // bn
    return pl.pallas_call(
        _merge_kernel,
        grid=(m // bm, n // bn),
        in_specs=[xspec, xspec, wspec, wspec,
                  pl.BlockSpec((bm, bn), lambda i, j: (i, ga + j)),
                  pl.BlockSpec((bm, bn), lambda i, j: (i, gb + j))],
        out_specs=pl.BlockSpec((bm, bn), lambda i, j: (i, j)),
        out_shape=jax.ShapeDtypeStruct((m, n), BF16),
        compiler_params=_cparams(("parallel", "arbitrary")),
    )(o_gla, o_rwkv, w_up_gla, w_up_rwkv, proj, proj)


def _gla_kernel(q_ref, k_ref, v_ref, og_ref, gd_ref, gup_ref, gbias_ref, onorm_ref, tri_ref,
                o_ref, state_ref, *, dk, bb):
    c, nsub = CHUNK, CHUNK // SUB
    rows = bb * c

    @pl.when(pl.program_id(2) == 0)
    def _():
        state_ref[...] = jnp.zeros_like(state_ref)

    def merged(ref):
        return ref[...].reshape(rows, ref.shape[-1])

    def per_row(x):
        return [x[i * c:(i + 1) * c] for i in range(bb)]

    def chunk_row(x, r):
        return jnp.concatenate(
            [jnp.broadcast_to(x[i * c + r:i * c + r + 1, :], (c, x.shape[-1])) for i in range(bb)], axis=0)

    x = _dot(merged(gd_ref), gup_ref[...]) + gbias_ref[...]
    log_a = -_softplus(-x) * (1.0 / GLA_GATE_NORM)
    b = _dot_split(tri_ref[...], log_a)
    q = merged(q_ref) * (dk ** -0.5)
    k = merged(k_ref)

    row = lax.broadcasted_iota(jnp.int32, (rows, c), 0) & (c - 1)
    lane = lax.broadcasted_iota(jnp.int32, (rows, c), 1)
    blk0 = row & -SUB
    d = lane - blk0
    dc = jnp.where(d >= 0, jnp.where(d <= (row & (SUB - 1)), d, -1), -1)
    mask_off = (lane < blk0).astype(F32)

    refs = [chunk_row(b, i * SUB - 1) for i in range(1, nsub)]
    sub_id = (lax.broadcasted_iota(jnp.int32, (rows, dk), 0) & (c - 1)) >> SUB_LOG2
    bref = jnp.zeros((rows, dk), F32)
    for i in range(1, nsub):
        bref = jnp.where(sub_id == i, refs[i - 1], bref)
    qhat = per_row(q * jnp.exp(b - bref))
    khat = [per_row(k * jnp.exp(jnp.minimum(ref - b, 0.0))) for ref in refs]
    attn = jnp.concatenate(
        [jnp.concatenate(
            [jnp.zeros((SUB, c), F32)]
            + [_dot_nt(qhat[r][i * SUB:(i + 1) * SUB, :], khat[i - 1][r]) for i in range(1, nsub)], axis=0)
         for r in range(bb)], axis=0) * mask_off

    k3 = k.reshape(bb * nsub, SUB, dk)
    b3 = b.reshape(bb * nsub, SUB, dk)
    for j in range(SUB):
        kj = jnp.broadcast_to(k3[:, j:j + 1, :], (bb * nsub, SUB, dk)).reshape(rows, dk)
        bj = jnp.broadcast_to(b3[:, j:j + 1, :], (bb * nsub, SUB, dk)).reshape(rows, dk)
        col = jnp.sum(q * kj * jnp.exp(jnp.minimum(b - bj, 0.0)), axis=-1, keepdims=True)
        attn = jnp.where(dc == j, col, attn)

    b_end = chunk_row(b, c - 1)
    attn = per_row(attn)
    qe = per_row(q * jnp.exp(b))
    k_end = per_row(k * jnp.exp(b_end - b))
    st = [state_ref[i] for i in range(bb)]
    o = [_dot(attn[i], v_ref[i]) + _dot_nt(qe[i], st[i]) for i in range(bb)]
    for i in range(bb):
        state_ref[i] = st[i] * jnp.exp(b_end[i * c:i * c + 1, :]) + _dot_tn(v_ref[i], k_end[i])

    o = _rms(jnp.concatenate(o, axis=0), onorm_ref[...])
    og = merged(og_ref)
    o_ref[...] = (o * (og * _sigmoid(og))).astype(o_ref.dtype).reshape(o_ref.shape)


def _gla(proj, gate_up, gate_bias, out_norm, tri, cols, bb, heads, dk, dv):
    batch, seq, _ = proj.shape
    nc = seq // CHUNK
    cq, ck, cv, cog, cgd = (cols[n] for n in ("gla_q", "gla_k", "gla_v", "gla_og", "gla_gd"))
    return pl.pallas_call(
        functools.partial(_gla_kernel, dk=dk, bb=bb),
        grid=(batch // bb, heads, nc),
        in_specs=[
            pl.BlockSpec((bb, CHUNK, dk), lambda b, h, c: (b, c, cq // dk + h)),
            pl.BlockSpec((bb, CHUNK, dk), lambda b, h, c: (b, c, ck // dk + h)),
            pl.BlockSpec((bb, CHUNK, dv), lambda b, h, c: (b, c, cv // dv + h)),
            pl.BlockSpec((bb, CHUNK, dv), lambda b, h, c: (b, c, cog // dv + h)),
            pl.BlockSpec((bb, CHUNK, LANES), lambda b, h, c: (b, c, cgd // LANES)),
            pl.BlockSpec((LANES, dk), lambda b, h, c: (0, h)),
            pl.BlockSpec((1, dk), lambda b, h, c: (0, h)),
            pl.BlockSpec((1, dv), lambda b, h, c: (0, 0)),
            pl.BlockSpec((bb * CHUNK, bb * CHUNK), lambda b, h, c: (0, 0)),
        ],
        out_specs=pl.BlockSpec((bb, CHUNK, dv), lambda b, h, c: (b, c, h)),
        out_shape=jax.ShapeDtypeStruct((batch, seq, heads * dv), BF16),
        scratch_shapes=[pltpu.VMEM((bb, dv, dk), F32)],
        compiler_params=_cparams(("parallel", "parallel", "arbitrary")),
    )(proj, proj, proj, proj, proj, gate_up, gate_bias, out_norm, tri)


def _rwkv_kernel(r_ref, k_ref, v_ref, gd_ref, wa_ref, rp_ref, kp_ref, vp_ref, gdp_ref, wap_ref,
                 mur_ref, muk_ref, muv_ref, mugd_ref, muwa_ref,
                 w0_ref, w2_ref, a0_ref, a2_ref, g2_ref, kk_ref, ka_ref, rk_ref, lnw_ref, lnb_ref,
                 tri_ref, ones_ref, mstrict_ref, mincl_ref, mblk_ref,
                 o_ref, state_ref, *, lora, bb):
    c, g = CHUNK, RWKV_GROUP
    n, rows = g * c, bb * c
    first = pl.program_id(2) == 0

    @pl.when(first)
    def _():
        state_ref[...] = jnp.zeros_like(state_ref)

    keep_prev = jnp.where(first, 0.0, 1.0)

    def per_row(x):
        return [x[i * c:(i + 1) * c] for i in range(bb)]

    def shifted(cur_ref, prev_ref, mu_ref):
        width = cur_ref.shape[-1]
        p = cur_ref[...].reshape(rows, width)
        prev = jnp.concatenate(
            [jnp.broadcast_to(prev_ref[i, 7:8, :] * keep_prev, (c, width)) for i in range(bb)], axis=0)
        is_row0 = (lax.broadcasted_iota(jnp.int32, p.shape, 0) & (c - 1)) == 0
        p_prev = jnp.where(is_row0, prev, pltpu.roll(p, 1, axis=0))
        return p + (p_prev - p) * mu_ref[...]

    r = shifted(r_ref, rp_ref, mur_ref)
    k = shifted(k_ref, kp_ref, muk_ref)
    v = shifted(v_ref, vp_ref, muv_ref)
    gd = shifted(gd_ref, gdp_ref, mugd_ref)
    wa = shifted(wa_ref, wap_ref, muwa_ref)

    w = -_softplus(-(w0_ref[...] + _dot(jnp.tanh(wa[:, :lora]), w2_ref[...]))) - 0.5
    ld = -jnp.exp(w)
    a = _sigmoid(a0_ref[...] + _dot(wa[:, lora:], a2_ref[...]))
    gate = _dot(_sigmoid(gd), g2_ref[...])

    ones_bd = ones_ref[...]
    m_blk = mblk_ref[...]
    kkr = k * kk_ref[...]
    kk = kkr * lax.rsqrt(jnp.maximum(_dot(kkr * kkr, ones_bd), 1e-24))
    k2 = k * (1.0 + (a - 1.0) * ka_ref[...])
    bw = a * kk

    cum = _dot_split(tri_ref[...], ld)
    c_end = [cum[(i + 1) * c - 1:(i + 1) * c, :] for i in range(bb)]
    c_end_rows = jnp.concatenate([jnp.broadcast_to(e, (c, e.shape[-1])) for e in c_end], axis=0)
    e_neg = jnp.exp(-cum)
    e_end = jnp.exp(c_end_rows - cum)
    rt = per_row(r * jnp.exp(cum))
    kkt = per_row(kk * jnp.exp(cum - ld))
    km = per_row(k2 * e_neg)
    bm = per_row(bw * e_neg)
    k_end = per_row(k2 * e_end)
    b_end = per_row(bw * e_end)
    v_rows = per_row(v)

    def stack(x):
        return jnp.concatenate([x] * g, axis=0) * m_blk

    def tile(x):
        return jnp.concatenate([x] * g, axis=0)

    def fold(x):
        out = x[0:c]
        for i in range(1, g):
            out = out + x[i * c:(i + 1) * c]
        return out

    m_strict, m_incl = mstrict_ref[...], mincl_ref[...]
    lhs = [jnp.concatenate([stack(kq), stack(rq)], axis=0).astype(BF16) for kq, rq in zip(kkt, rt)]
    aa = [_dot_nt(l, jnp.concatenate([tile(b_), tile(k_)], axis=0))
          for l, b_, k_ in zip(lhs, bm, km)]
    a_ab = [t[:n, :n] * m_strict for t in aa]
    a_kr = [jnp.concatenate([t[:n, n:] * m_strict, t[n:, n:] * m_incl], axis=0) for t in aa]
    a_rb = [t[n:, :n] * m_incl for t in aa]

    eye = m_incl - m_strict
    x = [eye - t for t in a_ab]
    p = [_dot(t, t) for t in a_ab]
    span = 2
    while span < c:
        span *= 2
        if span < c:
            xp = [_dot(jnp.concatenate([xi, pi], axis=0), pi) for xi, pi in zip(x, p)]
            x = [xi + t[:n] for xi, t in zip(x, xp)]
            p = [t[n:] for t in xp]
        else:
            x = [xi + _dot(xi, pi) for xi, pi in zip(x, p)]

    st = [state_ref[i] for i in range(bb)]
    m_blk2 = jnp.concatenate([m_blk, m_blk], axis=0)
    sk = [_dot_nt(l, s) for l, s in zip(lhs, st)]
    av = [_dot(t, tile(vi)) * m_blk2 for t, vi in zip(a_kr, v_rows)]
    u = [_dot(xi, -(s[:n] + t[:n])) for xi, s, t in zip(x, sk, av)]
    y = [fold(s[n:] + _dot(t, ui) + w_[n:]) for s, t, ui, w_ in zip(sk, a_rb, u, av)]
    for i in range(bb):
        state_ref[i] = st[i] * jnp.exp(c_end[i]) + (
            _dot_tn(fold(u[i]), b_end[i]) + _dot_tn(v_rows[i], k_end[i])) * m_blk

    y = jnp.concatenate(y, axis=0)
    inv_n = 1.0 / RWKV_HEAD
    mu = _dot(y, ones_bd) * inv_n
    dy = y - mu
    var = _dot(dy * dy, ones_bd) * inv_n
    yn = dy * lax.rsqrt(var + RWKV_LN_EPS) * lnw_ref[...] + lnb_ref[...]
    bonus = _dot(r * k2 * rk_ref[...], ones_bd) * v
    o_ref[...] = ((yn + bonus) * gate).astype(o_ref.dtype).reshape(o_ref.shape)


def _rwkv(proj, mus, w0, w2, a0, a2, g2, k_k, k_a, r_k, ln_w, ln_b, consts, cols, bb, width, lora):
    batch, seq, _ = proj.shape
    nc = seq // CHUNK
    gw = RWKV_GROUP * RWKV_HEAD
    groups = width // gw
    gdw = g2.shape[0]
    waw = 2 * lora

    def prev_rows(c):
        return jnp.maximum(c * (CHUNK // 8) - 1, 0)

    def cur(w_, col, per_group):
        if per_group:
            return pl.BlockSpec((bb, CHUNK, w_), lambda b, q, c: (b, c, col // w_ + q))
        return pl.BlockSpec((bb, CHUNK, w_), lambda b, q, c: (b, c, col // w_))

    def prev(w_, col, per_group):
        if per_group:
            return pl.BlockSpec((bb, 8, w_), lambda b, q, c: (b, prev_rows(c), col // w_ + q))
        return pl.BlockSpec((bb, 8, w_), lambda b, q, c: (b, prev_rows(c), col // w_))

    def gvec():
        return pl.BlockSpec((1, gw), lambda b, q, c: (0, q))

    def full(shape):
        return pl.BlockSpec(shape, lambda b, q, c: (0, 0))

    n = RWKV_GROUP * CHUNK
    in_specs = [
        cur(gw, cols["rw_r"], True), cur(gw, cols["rw_k"], True), cur(gw, cols["rw_v"], True),
        cur(gdw, cols["rw_gd"], False), cur(waw, cols["rw_wa"], False),
        prev(gw, cols["rw_r"], True), prev(gw, cols["rw_k"], True), prev(gw, cols["rw_v"], True),
        prev(gdw, cols["rw_gd"], False), prev(waw, cols["rw_wa"], False),
        gvec(), gvec(), gvec(), full((1, gdw)), full((1, waw)),
        gvec(), pl.BlockSpec((lora, gw), lambda b, q, c: (0, q)),
        gvec(), pl.BlockSpec((lora, gw), lambda b, q, c: (0, q)),
        pl.BlockSpec((gdw, gw), lambda b, q, c: (0, q)),
        gvec(), gvec(), gvec(), gvec(), gvec(),
        full((bb * CHUNK, bb * CHUNK)), full((gw, gw)), full((n, n)), full((n, n)), full((n, gw)),
    ]
    return pl.pallas_call(
        functools.partial(_rwkv_kernel, lora=lora, bb=bb),
        grid=(batch // bb, groups, nc),
        in_specs=in_specs,
        out_specs=pl.BlockSpec((bb, CHUNK, gw), lambda b, q, c: (b, c, q)),
        out_shape=jax.ShapeDtypeStruct((batch, seq, width), BF16),
        scratch_shapes=[pltpu.VMEM((bb, gw, gw), F32)],
        compiler_params=_cparams(("parallel", "parallel", "arbitrary")),
    )(proj, proj, proj, proj, proj, proj, proj, proj, proj, proj,
      mus["r"], mus["k"], mus["v"], mus["gd"], mus["wa"],
      w0, w2, a0, a2, g2, k_k, k_a, r_k, ln_w, ln_b,
      consts["tri_rows"], consts["ones_bd"], consts["m_strict"], consts["m_incl"], consts["m_blk"])


def _pad_cols(w, n):
    return jnp.pad(w, ((0, 0), (0, n - w.shape[1])))


def _pad_rows(w, n):
    return jnp.pad(w, ((0, n - w.shape[0]), (0, 0)))


def _layout(d_model):
    kw, vw, rw = d_model // 4, d_model // 2, d_model // 2
    lora = 128
    gdw = 512
    names = [("gla_q", kw), ("gla_k", kw), ("gla_v", vw), ("gla_og", vw),
             ("rw_r", rw), ("rw_k", rw), ("rw_v", rw),
             ("gate_gla", d_model), ("gate_rwkv", d_model),
             ("rw_gd", gdw), ("rw_wa", 2 * lora), ("gla_gd", LANES)]
    cols, off = {}, 0
    for name, w in names:
        assert off % w == 0, (name, off, w)
        cols[name] = off
        off += w
    return cols, off


def _chunk_consts(bb):
    c, g, hn = CHUNK, RWKV_GROUP, RWKV_HEAD
    assert c == hn
    n = g * c
    i = jnp.arange(n)
    same = (i[:, None] // c) == (i[None, :] // c)
    ti, tj = i[:, None] % c, i[None, :] % c
    lane = jnp.arange(g * hn)
    return {
        "tri": jnp.tril(jnp.ones((c, c), F32)).astype(BF16),
        "tri_rows": jnp.kron(jnp.eye(bb, dtype=F32), jnp.tril(jnp.ones((c, c), F32))).astype(BF16),
        "ones_bd": ((lane[:, None] // hn) == (lane[None, :] // hn)).astype(BF16),
        "m_strict": (same & (ti > tj)).astype(F32),
        "m_incl": (same & (ti >= tj)).astype(F32),
        "m_blk": ((i[:, None] // c) == (lane[None, :] // hn)).astype(F32),
    }


def kernel(x, ffn1_pre_norm, ffn1_w_gate, ffn1_w_up, ffn1_w_down, ffn1_post_norm, mix_pre_norm, w_in, gla_gate_up, gla_gate_bias, gla_out_norm, rwkv_shift_mix, rwkv_w0, rwkv_w2, rwkv_a0, rwkv_a2, rwkv_g2, rwkv_k_k, rwkv_k_a, rwkv_r_k, rwkv_ln_w, rwkv_ln_b, w_up_gla, w_up_rwkv, w_out, mix_post_norm, ffn2_pre_norm, ffn2_w_gate, ffn2_w_up, ffn2_w_down, ffn2_post_norm):
    batch, seq, d_model = x.shape
    depth = ffn1_pre_norm.shape[0]
    t = batch * seq
    assert seq % CHUNK == 0 and d_model % 1024 == 0

    gla_kw, gla_vw, rw_w = d_model // 4, d_model // 2, d_model // 2
    gla_heads = max(4, d_model // 512)
    dk, dv = gla_kw // gla_heads, gla_vw // gla_heads
    lora = rwkv_w2.shape[1]
    g_lora = rwkv_g2.shape[1]
    assert lora == 128 and g_lora <= 512 and dk % LANES == 0

    cols, n_proj = _layout(d_model)
    n_proj_pad = -(-n_proj // 1024) * 1024
    bm = min(1024, t)
    bm_down = min(512, t)
    bm_gu = min(2048, t)
    bt = min(256, t)
    bb = next(n for n in (4, 2, 1) if batch % n == 0)
    consts = _chunk_consts(bb)

    def row(v):
        return v.reshape(1, -1).astype(F32)

    h = x.reshape(t, d_model)
    for l in range(depth):
        xn = _norm_cast(h, row(ffn1_pre_norm[l]), bt)
        hid = _gate_up(xn, ffn1_w_gate[l], ffn1_w_up[l], bm_gu, 256)
        f = _matmul(hid, ffn1_w_down[l].astype(BF16), F32, bm_down, 512)
        h, u = _post_pre(h, f, row(ffn1_post_norm[l]), row(mix_pre_norm[l]), MACARON_WEIGHT, bt)

        wi = w_in[l]
        o_gla, o_rw, o_gate = 0, 2 * gla_kw + 2 * gla_vw + GLA_GATE_RANK, None
        gq, gk, gv, ggd, gog = jnp.split(wi[:, :o_rw], [gla_kw, 2 * gla_kw, 2 * gla_kw + gla_vw,
                                                       2 * gla_kw + gla_vw + GLA_GATE_RANK], axis=1)
        rw_sizes = [rw_w, lora, rw_w, rw_w, lora, g_lora]
        rw_off = [o_rw]
        for s in rw_sizes:
            rw_off.append(rw_off[-1] + s)
        rr, rwd, rk, rv, rad, rgd = (wi[:, rw_off[i]:rw_off[i + 1]] for i in range(6))
        gates = wi[:, rw_off[-1]:]
        w_proj = jnp.concatenate(
            [gq, gk, gv, gog, rr, rk, rv, gates, _pad_cols(rgd, 512), rwd, rad, _pad_cols(ggd, LANES),
             jnp.zeros((d_model, n_proj_pad - n_proj), F32)], axis=1).astype(BF16)
        proj = _matmul(u, w_proj, F32, bm, 1024)

        mu = rwkv_shift_mix[l]
        mu_off = [o - o_rw for o in rw_off]
        mu_r, mu_wd, mu_k, mu_v, mu_ad, mu_gd = (mu[mu_off[i]:mu_off[i + 1]] for i in range(6))
        mus = {"r": row(mu_r), "k": row(mu_k), "v": row(mu_v),
               "gd": row(jnp.pad(mu_gd, (0, 512 - g_lora))),
               "wa": row(jnp.concatenate([mu_wd, mu_ad]))}

        proj3 = proj.reshape(batch, seq, n_proj_pad)
        o_gla = _gla(proj3, _pad_rows(gla_gate_up[l], LANES).astype(BF16), row(gla_gate_bias[l]),
                     row(gla_out_norm[l]), consts["tri_rows"], cols, bb, gla_heads, dk, dv)
        o_rwkv = _rwkv(proj3, mus, row(rwkv_w0[l]), rwkv_w2[l].astype(BF16), row(rwkv_a0[l]),
                       rwkv_a2[l].astype(BF16), _pad_rows(rwkv_g2[l], 512).astype(BF16),
                       row(rwkv_k_k[l]), row(rwkv_k_a[l]), row(rwkv_r_k[l]), row(rwkv_ln_w[l]),
                       row(rwkv_ln_b[l]), consts, cols, bb, rw_w, lora)

        merged = _merge(o_gla.reshape(t, gla_vw), o_rwkv.reshape(t, rw_w),
                        w_up_gla[l].astype(BF16), w_up_rwkv[l].astype(BF16), proj,
                        cols["gate_gla"], cols["gate_rwkv"], bm, 512)
        mixed = _matmul(merged, w_out[l].astype(BF16), F32, bm, 1024)
        h, u = _post_pre(h, mixed, row(mix_post_norm[l]), row(ffn2_pre_norm[l]), 1.0, bt)

        hid = _gate_up(u, ffn2_w_gate[l], ffn2_w_up[l], bm_gu, 256)
        f = _matmul(hid, ffn2_w_down[l].astype(BF16), F32, bm_down, 512)
        h = _post(h, f, row(ffn2_post_norm[l]), MACARON_WEIGHT, bt)
    return h.reshape(batch, seq, d_model)
```

```python
import functools

import jax
import jax.numpy as jnp
from jax import lax
from jax.experimental import pallas as pl
from jax.experimental.pallas import tpu as pltpu

F32 = jnp.float32
BF16 = jnp.bfloat16

NORM_EPS = 1e-6
RWKV_LN_EPS = 64e-5
MACARON_WEIGHT = 0.5
GLA_GATE_NORM = 16.0
GLA_GATE_RANK = 16
RWKV_HEAD = 64
RWKV_GATE_LORA = 480

LANES = 128
CHUNK = 64
SUB_LOG2 = 4
SUB = 1 << SUB_LOG2
RWKV_GROUP = 4
RWKV_GROUPS_PER_STEP = 2
VMEM_LIMIT = 56 * 1024 * 1024
VMEM_LIMIT_MAX = 60 * 1024 * 1024


def _cparams(sem):
    return pltpu.CompilerParams(dimension_semantics=sem, vmem_limit_bytes=VMEM_LIMIT)


def _dot(a, b):
    return jnp.dot(a.astype(BF16), b.astype(BF16), preferred_element_type=F32)


def _dot_nt(a, b):
    return lax.dot_general(a.astype(BF16), b.astype(BF16), (((1,), (1,)), ((), ())),
                           preferred_element_type=F32)


def _dot_tn(a, b):
    return lax.dot_general(a.astype(BF16), b.astype(BF16), (((0,), (0,)), ((), ())),
                           preferred_element_type=F32)


def _dot_split(m01, x):
    hi = x.astype(BF16)
    lo = (x - hi.astype(F32)).astype(BF16)
    return (jnp.dot(m01, hi, preferred_element_type=F32)
            + jnp.dot(m01, lo, preferred_element_type=F32))


def _rms(x, g):
    ms = jnp.mean(x * x, axis=-1, keepdims=True)
    return x * lax.rsqrt(ms + NORM_EPS) * g


def _sigmoid(x):
    return 1.0 / (1.0 + jnp.exp(-x))


def _softplus(x):
    return jnp.maximum(x, 0.0) + jnp.log(1.0 + jnp.exp(-jnp.abs(x)))


def _norm_cast_kernel(x_ref, g_ref, o_ref):
    o_ref[...] = _rms(x_ref[...], g_ref[...]).astype(o_ref.dtype)


def _norm_cast(x, g, bt):
    t, d = x.shape
    return pl.pallas_call(
        _norm_cast_kernel,
        grid=(t // bt,),
        in_specs=[pl.BlockSpec((bt, d), lambda i: (i, 0)),
                  pl.BlockSpec((1, d), lambda i: (0, 0))],
        out_specs=pl.BlockSpec((bt, d), lambda i: (i, 0)),
        out_shape=jax.ShapeDtypeStruct((t, d), BF16),
        compiler_params=_cparams(("parallel",)),
    )(x, g)


def _post_pre_kernel(h_ref, f_ref, gpost_ref, gpre_ref, hout_ref, u_ref, *, alpha):
    h = h_ref[...] + alpha * _rms(f_ref[...], gpost_ref[...])
    hout_ref[...] = h
    u_ref[...] = _rms(h, gpre_ref[...]).astype(u_ref.dtype)


def _post_pre(h, f, g_post, g_pre, alpha, bt):
    t, d = h.shape
    row = pl.BlockSpec((bt, d), lambda i: (i, 0))
    vec = pl.BlockSpec((1, d), lambda i: (0, 0))
    return pl.pallas_call(
        functools.partial(_post_pre_kernel, alpha=alpha),
        grid=(t // bt,),
        in_specs=[row, row, vec, vec],
        out_specs=[row, row],
        out_shape=[jax.ShapeDtypeStruct((t, d), F32), jax.ShapeDtypeStruct((t, d), BF16)],
        compiler_params=_cparams(("parallel",)),
    )(h, f, g_post, g_pre)


def _post_kernel(h_ref, f_ref, gpost_ref, hout_ref, *, alpha):
    hout_ref[...] = h_ref[...] + alpha * _rms(f_ref[...], gpost_ref[...])


def _post(h, f, g_post, alpha, bt):
    t, d = h.shape
    row = pl.BlockSpec((bt, d), lambda i: (i, 0))
    vec = pl.BlockSpec((1, d), lambda i: (0, 0))
    return pl.pallas_call(
        functools.partial(_post_kernel, alpha=alpha),
        grid=(t // bt,),
        in_specs=[row, row, vec],
        out_specs=row,
        out_shape=jax.ShapeDtypeStruct((t, d), F32),
        compiler_params=_cparams(("parallel",)),
    )(h, f, g_post)


def _matmul_kernel(x_ref, w_ref, o_ref):
    o_ref[...] = jnp.dot(x_ref[...], w_ref[...], preferred_element_type=F32).astype(o_ref.dtype)


def _matmul(x, w, out_dtype, bm, bn):
    m, k = x.shape
    n = w.shape[1]
    return pl.pallas_call(
        _matmul_kernel,
        grid=(m // bm, pl.cdiv(n, bn)),
        in_specs=[pl.BlockSpec((bm, k), lambda i, j: (i, 0)),
                  pl.BlockSpec((k, bn), lambda i, j: (0, j))],
        out_specs=pl.BlockSpec((bm, bn), lambda i, j: (i, j)),
        out_shape=jax.ShapeDtypeStruct((m, n), out_dtype),
        compiler_params=_cparams(("parallel", "arbitrary")),
    )(x, w)


def _gate_up_kernel(x_ref, wg_ref, wu_ref, o_ref):
    x = x_ref[...]
    g = jnp.dot(x, wg_ref[...].astype(BF16), preferred_element_type=F32)
    u = jnp.dot(x, wu_ref[...].astype(BF16), preferred_element_type=F32)
    o_ref[...] = (g * _sigmoid(g) * u).astype(o_ref.dtype)


def _gate_up(x, wg, wu, bm, bn):
    m, k = x.shape
    n = wg.shape[1]
    wspec = pl.BlockSpec((k, bn), lambda i, j: (0, j))
    return pl.pallas_call(
        _gate_up_kernel,
        grid=(m // bm, pl.cdiv(n, bn)),
        in_specs=[pl.BlockSpec((bm, k), lambda i, j: (i, 0)), wspec, wspec],
        out_specs=pl.BlockSpec((bm, bn), lambda i, j: (i, j)),
        out_shape=jax.ShapeDtypeStruct((m, n), BF16),
        compiler_params=pltpu.CompilerParams(dimension_semantics=("parallel", "arbitrary"),
                                             vmem_limit_bytes=VMEM_LIMIT_MAX),
    )(x, wg, wu)


def _merge_kernel(og_ref, or_ref, wg_ref, wr_ref, gg_ref, gr_ref, o_ref):
    yg = jnp.dot(og_ref[...], wg_ref[...], preferred_element_type=F32)
    yr = jnp.dot(or_ref[...], wr_ref[...], preferred_element_type=F32)
    o_ref[...] = (_sigmoid(gg_ref[...]) * yg + _sigmoid(gr_ref[...]) * yr).astype(o_ref.dtype)


def _merge(o_gla, o_rwkv, w_up_gla, w_up_rwkv, proj, col_gate_gla, col_gate_rwkv, bm, bn):
    m, k = o_gla.shape
    n = w_up_gla.shape[1]
    xspec = pl.BlockSpec((bm, k), lambda i, j: (i, 0))
    wspec = pl.BlockSpec((k, bn), lambda i, j: (0, j))
    ga, gb = col_gate_gla // bn, col_gate_rwkv // bn
    return pl.pallas_call(
        _merge_kernel,
        grid=(m // bm, n // bn),
        in_specs=[xspec, xspec, wspec, wspec,
                  pl.BlockSpec((bm, bn), lambda i, j: (i, ga + j)),
                  pl.BlockSpec((bm, bn), lambda i, j: (i, gb + j))],
        out_specs=pl.BlockSpec((bm, bn), lambda i, j: (i, j)),
        out_shape=jax.ShapeDtypeStruct((m, n), BF16),
        compiler_params=_cparams(("parallel", "arbitrary")),
    )(o_gla, o_rwkv, w_up_gla, w_up_rwkv, proj, proj)


def _gla_kernel(q_ref, k_ref, v_ref, og_ref, gd_ref, gup_ref, gbias_ref, onorm_ref, tri_ref,
                o_ref, state_ref, *, dk, bb):
    c, nsub = CHUNK, CHUNK // SUB
    rows = bb * c

    @pl.when(pl.program_id(2) == 0)
    def _():
        state_ref[...] = jnp.zeros_like(state_ref)

    def merged(ref):
        return ref[...].reshape(rows, ref.shape[-1])

    def per_row(x):
        return [x[i * c:(i + 1) * c] for i in range(bb)]

    def chunk_row(x, r):
        return jnp.concatenate(
            [jnp.broadcast_to(x[i * c + r:i * c + r + 1, :], (c, x.shape[-1])) for i in range(bb)], axis=0)

    x = _dot(merged(gd_ref), gup_ref[...]) + gbias_ref[...]
    log_a = -_softplus(-x) * (1.0 / GLA_GATE_NORM)
    b = _dot_split(tri_ref[...], log_a)
    q = merged(q_ref) * (dk ** -0.5)
    k = merged(k_ref)

    row = lax.broadcasted_iota(jnp.int32, (rows, c), 0) & (c - 1)
    lane = lax.broadcasted_iota(jnp.int32, (rows, c), 1)
    blk0 = row & -SUB
    d = lane - blk0
    dc = jnp.where(d >= 0, jnp.where(d <= (row & (SUB - 1)), d, -1), -1)
    mask_off = (lane < blk0).astype(F32)

    refs = [chunk_row(b, i * SUB - 1) for i in range(1, nsub)]
    sub_id = (lax.broadcasted_iota(jnp.int32, (rows, dk), 0) & (c - 1)) >> SUB_LOG2
    bref = jnp.zeros((rows, dk), F32)
    for i in range(1, nsub):
        bref = jnp.where(sub_id == i, refs[i - 1], bref)
    qhat = per_row(q * jnp.exp(b - bref))
    khat = [per_row(k * jnp.exp(jnp.minimum(ref - b, 0.0))) for ref in refs]
    attn = jnp.concatenate(
        [jnp.concatenate(
            [jnp.zeros((SUB, c), F32)]
            + [_dot_nt(qhat[r][i * SUB:(i + 1) * SUB, :], khat[i - 1][r]) for i in range(1, nsub)], axis=0)
         for r in range(bb)], axis=0) * mask_off

    k3 = k.reshape(bb * nsub, SUB, dk)
    b3 = b.reshape(bb * nsub, SUB, dk)
    for j in range(SUB):
        kj = jnp.broadcast_to(k3[:, j:j + 1, :], (bb * nsub, SUB, dk)).reshape(rows, dk)
        bj = jnp.broadcast_to(b3[:, j:j + 1, :], (bb * nsub, SUB, dk)).reshape(rows, dk)
        col = jnp.sum(q * kj * jnp.exp(jnp.minimum(b - bj, 0.0)), axis=-1, keepdims=True)
        attn = jnp.where(dc == j, col, attn)

    b_end = chunk_row(b, c - 1)
    attn = per_row(attn)
    qe = per_row(q * jnp.exp(b))
    k_end = per_row(k * jnp.exp(b_end - b))
    st = [state_ref[i] for i in range(bb)]
    o = [_dot(attn[i], v_ref[i]) + _dot_nt(qe[i], st[i]) for i in range(bb)]
    for i in range(bb):
        state_ref[i] = st[i] * jnp.exp(b_end[i * c:i * c + 1, :]) + _dot_tn(v_ref[i], k_end[i])

    o = _rms(jnp.concatenate(o, axis=0), onorm_ref[...])
    og = merged(og_ref)
    o_ref[...] = (o * (og * _sigmoid(og))).astype(o_ref.dtype).reshape(o_ref.shape)


def _gla(proj, gate_up, gate_bias, out_norm, tri, cols, bb, heads, dk, dv):
    batch, seq, _ = proj.shape
    nc = seq // CHUNK
    cq, ck, cv, cog, cgd = (cols[n] for n in ("gla_q", "gla_k", "gla_v", "gla_og", "gla_gd"))
    return pl.pallas_call(
        functools.partial(_gla_kernel, dk=dk, bb=bb),
        grid=(batch // bb, heads, nc),
        in_specs=[
            pl.BlockSpec((bb, CHUNK, dk), lambda b, h, c: (b, c, cq // dk + h)),
            pl.BlockSpec((bb, CHUNK, dk), lambda b, h, c: (b, c, ck // dk + h)),
            pl.BlockSpec((bb, CHUNK, dv), lambda b, h, c: (b, c, cv // dv + h)),
            pl.BlockSpec((bb, CHUNK, dv), lambda b, h, c: (b, c, cog // dv + h)),
            pl.BlockSpec((bb, CHUNK, LANES), lambda b, h, c: (b, c, cgd // LANES)),
            pl.BlockSpec((LANES, dk), lambda b, h, c: (0, h)),
            pl.BlockSpec((1, dk), lambda b, h, c: (0, h)),
            pl.BlockSpec((1, dv), lambda b, h, c: (0, 0)),
            pl.BlockSpec((bb * CHUNK, bb * CHUNK), lambda b, h, c: (0, 0)),
        ],
        out_specs=pl.BlockSpec((bb, CHUNK, dv), lambda b, h, c: (b, c, h)),
        out_shape=jax.ShapeDtypeStruct((batch, seq, heads * dv), BF16),
        scratch_shapes=[pltpu.VMEM((bb, dv, dk), F32)],
        compiler_params=_cparams(("parallel", "parallel", "arbitrary")),
    )(proj, proj, proj, proj, proj, gate_up, gate_bias, out_norm, tri)


def _rwkv_kernel(r_ref, k_ref, v_ref, gd_ref, wa_ref, rp_ref, kp_ref, vp_ref, gdp_ref, wap_ref,
                 mur_ref, muk_ref, muv_ref, mugd_ref, muwa_ref,
                 w0_ref, w2_ref, a0_ref, a2_ref, g2_ref, kk_ref, ka_ref, rk_ref, lnw_ref, lnb_ref,
                 tri_ref, ones_ref, mstrict_ref, mincl_ref, mblk_ref,
                 o_ref, state_ref, *, lora, bb, gq):
    c, g = CHUNK, RWKV_GROUP
    n, rows, gw = g * c, bb * c, g * RWKV_HEAD
    first = pl.program_id(2) == 0

    @pl.when(first)
    def _():
        state_ref[...] = jnp.zeros_like(state_ref)

    keep_prev = jnp.where(first, 0.0, 1.0)

    def per_row(x):
        return [x[i * c:(i + 1) * c] for i in range(bb)]

    def shifted(cur_ref, prev_ref, mu_ref, lanes):
        p = cur_ref[:, :, lanes].reshape(rows, -1)
        width = p.shape[-1]
        prev = jnp.concatenate(
            [jnp.broadcast_to(prev_ref[i, 7:8, lanes] * keep_prev, (c, width)) for i in range(bb)], axis=0)
        is_row0 = (lax.broadcasted_iota(jnp.int32, p.shape, 0) & (c - 1)) == 0
        p_prev = jnp.where(is_row0, prev, pltpu.roll(p, 1, axis=0))
        return p + (p_prev - p) * mu_ref[:, lanes]

    everything = slice(None)
    wa = shifted(wa_ref, wap_ref, muwa_ref, everything)
    tanh_wd = jnp.tanh(wa[:, :lora]).astype(BF16)
    a_down = wa[:, lora:].astype(BF16)
    sig_gd = _sigmoid(shifted(gd_ref, gdp_ref, mugd_ref, everything)).astype(BF16)

    ones_bd = ones_ref[...]
    m_blk = mblk_ref[...]
    m_blk_bf16 = m_blk.astype(BF16)
    tri = tri_ref[...]

    r_all, k2_all, v_all, gate_all = [], [], [], []
    rt, kkt, km, bm, k_end, b_end, v_rows, c_end = [], [], [], [], [], [], [], []
    for q in range(gq):
        lanes = slice(q * gw, (q + 1) * gw)
        r = shifted(r_ref, rp_ref, mur_ref, lanes)
        k = shifted(k_ref, kp_ref, muk_ref, lanes)
        v = shifted(v_ref, vp_ref, muv_ref, lanes)
        w = -_softplus(-(w0_ref[:, lanes] + _dot(tanh_wd, w2_ref[:, lanes]))) - 0.5
        ld = -jnp.exp(w)
        a = _sigmoid(a0_ref[:, lanes] + _dot(a_down, a2_ref[:, lanes]))
        gate_all.append(_dot(sig_gd, g2_ref[:, lanes]))

        kkr = k * kk_ref[:, lanes]
        kk = kkr * lax.rsqrt(jnp.maximum(_dot(kkr * kkr, ones_bd), 1e-24))
        k2 = k * (1.0 + (a - 1.0) * ka_ref[:, lanes])
        bw = a * kk

        cum = _dot_split(tri, ld)
        ends = [cum[(i + 1) * c - 1:(i + 1) * c, :] for i in range(bb)]
        c_end_rows = jnp.concatenate([jnp.broadcast_to(e, (c, gw)) for e in ends], axis=0)
        e_neg = jnp.exp(-cum)
        e_end = jnp.exp(c_end_rows - cum)
        c_end += ends
        rt += per_row(r * jnp.exp(cum))
        kkt += per_row(kk * jnp.exp(cum - ld))
        km += per_row(k2 * e_neg)
        bm += per_row(bw * e_neg)
        k_end += per_row(k2 * e_end)
        b_end += per_row(bw * e_end)
        v_rows += per_row(v)
        r_all.append(r)
        k2_all.append(k2)
        v_all.append(v)

    def stack(x):
        return jnp.concatenate([x.astype(BF16)] * g, axis=0) * m_blk_bf16

    m_strict, m_incl = mstrict_ref[...], mincl_ref[...]
    lhs = [jnp.concatenate([kq, rq], axis=0).astype(BF16) for kq, rq in zip(kkt, rt)]
    aa = [_dot_nt(l, jnp.concatenate([stack(b_), stack(k_)], axis=0))
          for l, b_, k_ in zip(lhs, bm, km)]
    a_ab = [t[:c, :n] * m_strict for t in aa]
    a_kr = [jnp.concatenate([t[:c, n:] * m_strict, t[c:, n:] * m_incl], axis=0) for t in aa]
    a_rb = [t[c:, :n] * m_incl for t in aa]

    eye = m_incl - m_strict
    x = [eye - t for t in a_ab]
    p = [_dot(t, stack(t)) for t in a_ab]
    span = 2
    while span < c:
        span *= 2
        if span < c:
            xp = [_dot(jnp.concatenate([xi, pi], axis=0), stack(pi)) for xi, pi in zip(x, p)]
            x = [xi + t[:c] for xi, t in zip(x, xp)]
            p = [t[c:] for t in xp]
        else:
            x = [xi + _dot(xi, stack(pi)) for xi, pi in zip(x, p)]

    chains = [(q, i) for q in range(gq) for i in range(bb)]
    st = [state_ref[i, q] for q, i in chains]
    sk = [_dot_nt(l, s) for l, s in zip(lhs, st)]
    av = [_dot(t, stack(vi)) for t, vi in zip(a_kr, v_rows)]
    u = [_dot(xi, stack(-(s[:c] + t[:c]))) for xi, s, t in zip(x, sk, av)]
    y = [s[c:] + _dot(t, stack(ui)) + w_[c:] for s, t, ui, w_ in zip(sk, a_rb, u, av)]
    for j, (q, i) in enumerate(chains):
        state_ref[i, q] = st[j] * jnp.exp(c_end[j]) + (
            _dot_tn(u[j], b_end[j]) + _dot_tn(v_rows[j], k_end[j])) * m_blk

    inv_n = 1.0 / RWKV_HEAD
    for q in range(gq):
        lanes = slice(q * gw, (q + 1) * gw)
        yq = jnp.concatenate(y[q * bb:(q + 1) * bb], axis=0)
        mu = _dot(yq, ones_bd) * inv_n
        dy = yq - mu
        var = _dot(dy * dy, ones_bd) * inv_n
        yn = dy * lax.rsqrt(var + RWKV_LN_EPS) * lnw_ref[:, lanes] + lnb_ref[:, lanes]
        bonus = _dot(r_all[q] * k2_all[q] * rk_ref[:, lanes], ones_bd) * v_all[q]
        o_ref[:, :, lanes] = ((yn + bonus) * gate_all[q]).astype(o_ref.dtype).reshape(bb, c, gw)


def _rwkv(proj, mus, w0, w2, a0, a2, g2, k_k, k_a, r_k, ln_w, ln_b, consts, cols, bb, width, lora):
    batch, seq, _ = proj.shape
    nc = seq // CHUNK
    gw = RWKV_GROUP * RWKV_HEAD
    gq = next(m for m in (RWKV_GROUPS_PER_STEP, 1) if (width // gw) % m == 0)
    sw = gq * gw
    steps = width // sw
    gdw = g2.shape[0]
    waw = 2 * lora

    def prev_rows(c):
        return jnp.maximum(c * (CHUNK // 8) - 1, 0)

    def cur(w_, col, per_group):
        if per_group:
            return pl.BlockSpec((bb, CHUNK, w_), lambda b, q, c: (b, c, col // w_ + q))
        return pl.BlockSpec((bb, CHUNK, w_), lambda b, q, c: (b, c, col // w_))

    def prev(w_, col, per_group):
        if per_group:
            return pl.BlockSpec((bb, 8, w_), lambda b, q, c: (b, prev_rows(c), col // w_ + q))
        return pl.BlockSpec((bb, 8, w_), lambda b, q, c: (b, prev_rows(c), col // w_))

    def gvec():
        return pl.BlockSpec((1, sw), lambda b, q, c: (0, q))

    def full(shape):
        return pl.BlockSpec(shape, lambda b, q, c: (0, 0))

    n = RWKV_GROUP * CHUNK
    in_specs = [
        cur(sw, cols["rw_r"], True), cur(sw, cols["rw_k"], True), cur(sw, cols["rw_v"], True),
        cur(gdw, cols["rw_gd"], False), cur(waw, cols["rw_wa"], False),
        prev(sw, cols["rw_r"], True), prev(sw, cols["rw_k"], True), prev(sw, cols["rw_v"], True),
        prev(gdw, cols["rw_gd"], False), prev(waw, cols["rw_wa"], False),
        gvec(), gvec(), gvec(), full((1, gdw)), full((1, waw)),
        gvec(), pl.BlockSpec((lora, sw), lambda b, q, c: (0, q)),
        gvec(), pl.BlockSpec((lora, sw), lambda b, q, c: (0, q)),
        pl.BlockSpec((gdw, sw), lambda b, q, c: (0, q)),
        gvec(), gvec(), gvec(), gvec(), gvec(),
        full((bb * CHUNK, bb * CHUNK)), full((gw, gw)), full((CHUNK, n)), full((CHUNK, n)), full((n, gw)),
    ]
    return pl.pallas_call(
        functools.partial(_rwkv_kernel, lora=lora, bb=bb, gq=gq),
        grid=(batch // bb, steps, nc),
        in_specs=in_specs,
        out_specs=pl.BlockSpec((bb, CHUNK, sw), lambda b, q, c: (b, c, q)),
        out_shape=jax.ShapeDtypeStruct((batch, seq, width), BF16),
        scratch_shapes=[pltpu.VMEM((bb, gq, gw, gw), F32)],
        compiler_params=_cparams(("parallel", "parallel", "arbitrary")),
    )(proj, proj, proj, proj, proj, proj, proj, proj, proj, proj,
      mus["r"], mus["k"], mus["v"], mus["gd"], mus["wa"],
      w0, w2, a0, a2, g2, k_k, k_a, r_k, ln_w, ln_b,
      consts["tri_rows"], consts["ones_bd"], consts["m_strict"], consts["m_incl"], consts["m_blk"])


def _pad_cols(w, n):
    return jnp.pad(w, ((0, 0), (0, n - w.shape[1])))


def _pad_rows(w, n):
    return jnp.pad(w, ((0, n - w.shape[0]), (0, 0)))


def _layout(d_model):
    kw, vw, rw = d_model // 4, d_model // 2, d_model // 2
    lora = 128
    gdw = 512
    names = [("gla_q", kw), ("gla_k", kw), ("gla_v", vw), ("gla_og", vw),
             ("rw_r", rw), ("rw_k", rw), ("rw_v", rw),
             ("gate_gla", d_model), ("gate_rwkv", d_model),
             ("rw_gd", gdw), ("rw_wa", 2 * lora), ("gla_gd", LANES)]
    cols, off = {}, 0
    for name, w in names:
        assert off % w == 0, (name, off, w)
        cols[name] = off
        off += w
    return cols, off


def _chunk_consts(bb):
    c, g, hn = CHUNK, RWKV_GROUP, RWKV_HEAD
    assert c == hn
    n = g * c
    i = jnp.arange(n)
    ti, tj = jnp.arange(c)[:, None], i[None, :] % c
    lane = jnp.arange(g * hn)
    return {
        "tri_rows": jnp.kron(jnp.eye(bb, dtype=F32), jnp.tril(jnp.ones((c, c), F32))).astype(BF16),
        "ones_bd": ((lane[:, None] // hn) == (lane[None, :] // hn)).astype(BF16),
        "m_strict": (ti > tj).astype(F32),
        "m_incl": (ti >= tj).astype(F32),
        "m_blk": ((i[:, None] // c) == (lane[None, :] // hn)).astype(F32),
    }


def kernel(x, ffn1_pre_norm, ffn1_w_gate, ffn1_w_up, ffn1_w_down, ffn1_post_norm, mix_pre_norm, w_in, gla_gate_up, gla_gate_bias, gla_out_norm, rwkv_shift_mix, rwkv_w0, rwkv_w2, rwkv_a0, rwkv_a2, rwkv_g2, rwkv_k_k, rwkv_k_a, rwkv_r_k, rwkv_ln_w, rwkv_ln_b, w_up_gla, w_up_rwkv, w_out, mix_post_norm, ffn2_pre_norm, ffn2_w_gate, ffn2_w_up, ffn2_w_down, ffn2_post_norm):
    batch, seq, d_model = x.shape
    depth = ffn1_pre_norm.shape[0]
    t = batch * seq
    assert seq % CHUNK == 0 and d_model % 1024 == 0

    gla_kw, gla_vw, rw_w = d_model // 4, d_model // 2, d_model // 2
    gla_heads = max(4, d_model // 512)
    dk, dv = gla_kw // gla_heads, gla_vw // gla_heads
    lora = rwkv_w2.shape[1]
    g_lora = rwkv_g2.shape[1]
    assert lora == 128 and g_lora <= 512 and dk % LANES == 0

    cols, n_proj = _layout(d_model)
    n_proj_pad = -(-n_proj // 1024) * 1024
    bm = min(1024, t)
    bm_down = min(512, t)
    bm_gu = min(2048, t)
    bt = min(256, t)
    bb = next(n for n in (4, 2, 1) if batch % n == 0)
    consts = _chunk_consts(bb)

    def row(v):
        return v.reshape(1, -1).astype(F32)

    h = x.reshape(t, d_model)
    for l in range(depth):
        xn = _norm_cast(h, row(ffn1_pre_norm[l]), bt)
        hid = _gate_up(xn, ffn1_w_gate[l], ffn1_w_up[l], bm_gu, 256)
        f = _matmul(hid, ffn1_w_down[l].astype(BF16), F32, bm_down, 512)
        h, u = _post_pre(h, f, row(ffn1_post_norm[l]), row(mix_pre_norm[l]), MACARON_WEIGHT, bt)

        wi = w_in[l]
        o_gla, o_rw, o_gate = 0, 2 * gla_kw + 2 * gla_vw + GLA_GATE_RANK, None
        gq, gk, gv, ggd, gog = jnp.split(wi[:, :o_rw], [gla_kw, 2 * gla_kw, 2 * gla_kw + gla_vw,
                                                       2 * gla_kw + gla_vw + GLA_GATE_RANK], axis=1)
        rw_sizes = [rw_w, lora, rw_w, rw_w, lora, g_lora]
        rw_off = [o_rw]
        for s in rw_sizes:
            rw_off.append(rw_off[-1] + s)
        rr, rwd, rk, rv, rad, rgd = (wi[:, rw_off[i]:rw_off[i + 1]] for i in range(6))
        gates = wi[:, rw_off[-1]:]
        w_proj = jnp.concatenate(
            [gq, gk, gv, gog, rr, rk, rv, gates, _pad_cols(rgd, 512), rwd, rad, _pad_cols(ggd, LANES),
             jnp.zeros((d_model, n_proj_pad - n_proj), F32)], axis=1).astype(BF16)
        proj = _matmul(u, w_proj, F32, bm, 1024)

        mu = rwkv_shift_mix[l]
        mu_off = [o - o_rw for o in rw_off]
        mu_r, mu_wd, mu_k, mu_v, mu_ad, mu_gd = (mu[mu_off[i]:mu_off[i + 1]] for i in range(6))
        mus = {"r": row(mu_r), "k": row(mu_k), "v": row(mu_v),
               "gd": row(jnp.pad(mu_gd, (0, 512 - g_lora))),
               "wa": row(jnp.concatenate([mu_wd, mu_ad]))}

        proj3 = proj.reshape(batch, seq, n_proj_pad)
        o_gla = _gla(proj3, _pad_rows(gla_gate_up[l], LANES).astype(BF16), row(gla_gate_bias[l]),
                     row(gla_out_norm[l]), consts["tri_rows"], cols, bb, gla_heads, dk, dv)
        o_rwkv = _rwkv(proj3, mus, row(rwkv_w0[l]), rwkv_w2[l].astype(BF16), row(rwkv_a0[l]),
                       rwkv_a2[l].astype(BF16), _pad_rows(rwkv_g2[l], 512).astype(BF16),
                       row(rwkv_k_k[l]), row(rwkv_k_a[l]), row(rwkv_r_k[l]), row(rwkv_ln_w[l]),
                       row(rwkv_ln_b[l]), consts, cols, bb, rw_w, lora)

        merged = _merge(o_gla.reshape(t, gla_vw), o_rwkv.reshape(t, rw_w),
                        w_up_gla[l].astype(BF16), w_up_rwkv[l].astype(BF16), proj,
                        cols["gate_gla"], cols["gate_rwkv"], bm, 512)
        mixed = _matmul(merged, w_out[l].astype(BF16), F32, bm, 1024)
        h, u = _post_pre(h, mixed, row(mix_post_norm[l]), row(ffn2_pre_norm[l]), 1.0, bt)

        hid = _gate_up(u, ffn2_w_gate[l], ffn2_w_up[l], bm_gu, 256)
        f = _matmul(hid, ffn2_w_down[l].astype(BF16), F32, bm_down, 512)
        h = _post(h, f, row(ffn2_post_norm[l]), MACARON_WEIGHT, bt)
    return h.reshape(batch, seq, d_model)
```

```python
import functools

import jax
import jax.numpy as jnp
from jax import lax
from jax.experimental import pallas as pl
from jax.experimental.pallas import tpu as pltpu

F32 = jnp.float32
BF16 = jnp.bfloat16

NORM_EPS = 1e-6
RWKV_LN_EPS = 64e-5
MACARON_WEIGHT = 0.5
GLA_GATE_NORM = 16.0
GLA_GATE_RANK = 16
LOG2_E = 1.4426950408889634
RWKV_HEAD = 64
RWKV_GATE_LORA = 480

LANES = 128
CHUNK = 64
SUB_LOG2 = 4
SUB = 1 << SUB_LOG2
GLA_HEADS_PER_STEP = 2
RWKV_GROUP = 4
RWKV_GROUPS_PER_STEP = 4
VMEM_LIMIT = 56 * 1024 * 1024
VMEM_LIMIT_MAX = 60 * 1024 * 1024


def _cparams(sem):
    return pltpu.CompilerParams(dimension_semantics=sem, vmem_limit_bytes=VMEM_LIMIT)


def _dot(a, b):
    return jnp.dot(a.astype(BF16), b.astype(BF16), preferred_element_type=F32)


def _dot_nt(a, b):
    return lax.dot_general(a.astype(BF16), b.astype(BF16), (((1,), (1,)), ((), ())),
                           preferred_element_type=F32)


def _dot_tn(a, b):
    return lax.dot_general(a.astype(BF16), b.astype(BF16), (((0,), (0,)), ((), ())),
                           preferred_element_type=F32)


def _dot_split(m01, x):
    hi = x.astype(BF16)
    lo = (x - hi.astype(F32)).astype(BF16)
    return (jnp.dot(m01, hi, preferred_element_type=F32)
            + jnp.dot(m01, lo, preferred_element_type=F32))


def _rms(x, g):
    ms = jnp.mean(x * x, axis=-1, keepdims=True)
    return x * lax.rsqrt(ms + NORM_EPS) * g


def _sigmoid(x):
    return 1.0 / (1.0 + jnp.exp(-x))


def _softplus(x):
    return jnp.maximum(x, 0.0) + jnp.log(1.0 + jnp.exp(-jnp.abs(x)))


def _norm_cast_kernel(x_ref, g_ref, o_ref):
    o_ref[...] = _rms(x_ref[...], g_ref[...]).astype(o_ref.dtype)


def _norm_cast(x, g, bt):
    t, d = x.shape
    return pl.pallas_call(
        _norm_cast_kernel,
        grid=(t // bt,),
        in_specs=[pl.BlockSpec((bt, d), lambda i: (i, 0)),
                  pl.BlockSpec((1, d), lambda i: (0, 0))],
        out_specs=pl.BlockSpec((bt, d), lambda i: (i, 0)),
        out_shape=jax.ShapeDtypeStruct((t, d), BF16),
        compiler_params=_cparams(("parallel",)),
    )(x, g)


def _post_pre_kernel(h_ref, f_ref, gpost_ref, gpre_ref, hout_ref, u_ref, *, alpha):
    h = h_ref[...] + alpha * _rms(f_ref[...], gpost_ref[...])
    hout_ref[...] = h
    u_ref[...] = _rms(h, gpre_ref[...]).astype(u_ref.dtype)


def _post_pre(h, f, g_post, g_pre, alpha, bt):
    t, d = h.shape
    row = pl.BlockSpec((bt, d), lambda i: (i, 0))
    vec = pl.BlockSpec((1, d), lambda i: (0, 0))
    return pl.pallas_call(
        functools.partial(_post_pre_kernel, alpha=alpha),
        grid=(t // bt,),
        in_specs=[row, row, vec, vec],
        out_specs=[row, row],
        out_shape=[jax.ShapeDtypeStruct((t, d), F32), jax.ShapeDtypeStruct((t, d), BF16)],
        compiler_params=_cparams(("parallel",)),
    )(h, f, g_post, g_pre)


def _post_kernel(h_ref, f_ref, gpost_ref, hout_ref, *, alpha):
    hout_ref[...] = h_ref[...] + alpha * _rms(f_ref[...], gpost_ref[...])


def _post(h, f, g_post, alpha, bt):
    t, d = h.shape
    row = pl.BlockSpec((bt, d), lambda i: (i, 0))
    vec = pl.BlockSpec((1, d), lambda i: (0, 0))
    return pl.pallas_call(
        functools.partial(_post_kernel, alpha=alpha),
        grid=(t // bt,),
        in_specs=[row, row, vec],
        out_specs=row,
        out_shape=jax.ShapeDtypeStruct((t, d), F32),
        compiler_params=_cparams(("parallel",)),
    )(h, f, g_post)


def _matmul_kernel(x_ref, w_ref, o_ref):
    o_ref[...] = jnp.dot(x_ref[...], w_ref[...], preferred_element_type=F32).astype(o_ref.dtype)


def _matmul(x, w, out_dtype, bm, bn):
    m, k = x.shape
    n = w.shape[1]
    return pl.pallas_call(
        _matmul_kernel,
        grid=(m // bm, pl.cdiv(n, bn)),
        in_specs=[pl.BlockSpec((bm, k), lambda i, j: (i, 0)),
                  pl.BlockSpec((k, bn), lambda i, j: (0, j))],
        out_specs=pl.BlockSpec((bm, bn), lambda i, j: (i, j)),
        out_shape=jax.ShapeDtypeStruct((m, n), out_dtype),
        compiler_params=_cparams(("parallel", "arbitrary")),
    )(x, w)


def _gate_up_kernel(x_ref, wg_ref, wu_ref, o_ref):
    x = x_ref[...]
    g = jnp.dot(x, wg_ref[...].astype(BF16), preferred_element_type=F32)
    u = jnp.dot(x, wu_ref[...].astype(BF16), preferred_element_type=F32)
    o_ref[...] = (g * _sigmoid(g) * u).astype(o_ref.dtype)


def _gate_up(x, wg, wu, bm, bn):
    m, k = x.shape
    n = wg.shape[1]
    wspec = pl.BlockSpec((k, bn), lambda i, j: (0, j))
    return pl.pallas_call(
        _gate_up_kernel,
        grid=(m // bm, pl.cdiv(n, bn)),
        in_specs=[pl.BlockSpec((bm, k), lambda i, j: (i, 0)), wspec, wspec],
        out_specs=pl.BlockSpec((bm, bn), lambda i, j: (i, j)),
        out_shape=jax.ShapeDtypeStruct((m, n), BF16),
        compiler_params=pltpu.CompilerParams(dimension_semantics=("parallel", "arbitrary"),
                                             vmem_limit_bytes=VMEM_LIMIT_MAX),
    )(x, wg, wu)


def _merge_kernel(og_ref, or_ref, wg_ref, wr_ref, gg_ref, gr_ref, o_ref):
    yg = jnp.dot(og_ref[...], wg_ref[...], preferred_element_type=F32)
    yr = jnp.dot(or_ref[...], wr_ref[...], preferred_element_type=F32)
    o_ref[...] = (_sigmoid(gg_ref[...]) * yg + _sigmoid(gr_ref[...]) * yr).astype(o_ref.dtype)


def _merge(o_gla, o_rwkv, w_up_gla, w_up_rwkv, proj, col_gate_gla, col_gate_rwkv, bm, bn):
    m, k = o_gla.shape
    n = w_up_gla.shape[1]
    xspec = pl.BlockSpec((bm, k), lambda i, j: (i, 0))
    wspec = pl.BlockSpec((k, bn), lambda i, j: (0, j))
    ga, gb = col_gate_gla // bn, col_gate_rwkv // bn
    return pl.pallas_call(
        _merge_kernel,
        grid=(m // bm, n // bn),
        in_specs=[xspec, xspec, wspec, wspec,
                  pl.BlockSpec((bm, bn), lambda i, j: (i, ga + j)),
                  pl.BlockSpec((bm, bn), lambda i, j: (i, gb + j))],
        out_specs=pl.BlockSpec((bm, bn), lambda i, j: (i, j)),
        out_shape=jax.ShapeDtypeStruct((m, n), BF16),
        compiler_params=_cparams(("parallel", "arbitrary")),
    )(o_gla, o_rwkv, w_up_gla, w_up_rwkv, proj, proj)


def _gla_kernel(q_ref, k_ref, v_ref, og_ref, gd_ref, gup_ref, gbias_ref, onorm_ref, tri_ref, sel_ref,
                o_ref, state_ref, *, dk, dv, bb, hq):
    c, nsub = CHUNK, CHUNK // SUB
    rows = bb * c

    @pl.when(pl.program_id(2) == 0)
    def _():
        state_ref[...] = jnp.zeros_like(state_ref)

    def merged(ref, lanes=slice(None)):
        return ref[:, :, lanes].reshape(rows, -1)

    def per_row(x):
        return [x[i * c:(i + 1) * c] for i in range(bb)]

    def chunk_row(x, r):
        return jnp.concatenate(
            [jnp.broadcast_to(x[i * c + r:i * c + r + 1, :], (c, x.shape[-1])) for i in range(bb)], axis=0)

    row = lax.broadcasted_iota(jnp.int32, (rows, c), 0) & (c - 1)
    lane = lax.broadcasted_iota(jnp.int32, (rows, c), 1)
    blk0 = row & -SUB
    d = lane - blk0
    dc = jnp.where(d >= 0, jnp.where(d <= (row & (SUB - 1)), d, -1), -1)
    mask_off = (lane < blk0).astype(F32)
    sub_id = (lax.broadcasted_iota(jnp.int32, (rows, dk), 0) & (c - 1)) >> SUB_LOG2
    gd = merged(gd_ref).astype(BF16)
    tri = tri_ref[...]

    qs, ks, bs, qhat, khat, b_ends = [], [], [], [], [], []
    for h in range(hq):
        lanes = slice(h * dk, (h + 1) * dk)
        x = _dot(gd, gup_ref[:, lanes]) + gbias_ref[:, lanes]
        log_a = -_softplus(-x) * (1.0 / GLA_GATE_NORM)
        b = _dot_split(tri, log_a * LOG2_E)
        q = merged(q_ref, lanes) * (dk ** -0.5)
        k = merged(k_ref, lanes)
        refs = [chunk_row(b, i * SUB - 1) for i in range(1, nsub)]
        bref = jnp.zeros((rows, dk), F32)
        for i in range(1, nsub):
            bref = jnp.where(sub_id == i, refs[i - 1], bref)
        qhat.append(per_row(q * jnp.exp2(b - bref)))
        khat.append([per_row(k * jnp.exp2(jnp.minimum(ref - b, 0.0))) for ref in refs])
        qs.append(q)
        ks.append(k)
        bs.append(b)
        b_ends.append(chunk_row(b, c - 1))

    attn = [jnp.concatenate(
        [jnp.concatenate(
            [jnp.zeros((SUB, c), F32)]
            + [_dot_nt(qhat[h][r][i * SUB:(i + 1) * SUB, :], khat[h][i - 1][r]) for i in range(1, nsub)],
            axis=0)
         for r in range(bb)], axis=0) * mask_off for h in range(hq)]

    for h in range(hq):
        q, k, b = qs[h], ks[h], bs[h]
        k3 = k.reshape(bb * nsub, SUB, dk)
        b3 = b.reshape(bb * nsub, SUB, dk)
        terms = []
        for j in range(SUB):
            kj = jnp.broadcast_to(k3[:, j:j + 1, :], (bb * nsub, SUB, dk)).reshape(rows, dk)
            bj = jnp.broadcast_to(b3[:, j:j + 1, :], (bb * nsub, SUB, dk)).reshape(rows, dk)
            terms.append((q * kj * jnp.exp2(jnp.minimum(b - bj, 0.0))).astype(BF16))
        diag = jnp.dot(jnp.concatenate(terms, axis=1), sel_ref[...], preferred_element_type=F32)
        attn[h] = jnp.where(dc >= 0, diag, attn[h])

    chains = [(h, i) for h in range(hq) for i in range(bb)]
    attn_c = [t for h in range(hq) for t in per_row(attn[h])]
    qe = [t for h in range(hq) for t in per_row(qs[h] * jnp.exp2(bs[h]))]
    k_end = [t for h in range(hq) for t in per_row(ks[h] * jnp.exp2(b_ends[h] - bs[h]))]
    v_c = [v_ref[i, :, h * dv:(h + 1) * dv] for h, i in chains]
    st = [state_ref[i, h] for h, i in chains]
    o = [_dot(a_, v_) + _dot_nt(q_, s_) for a_, v_, q_, s_ in zip(attn_c, v_c, qe, st)]
    for j, (h, i) in enumerate(chains):
        state_ref[i, h] = st[j] * jnp.exp2(b_ends[h][i * c:i * c + 1, :]) + _dot_tn(v_c[j], k_end[j])

    for h in range(hq):
        lanes = slice(h * dv, (h + 1) * dv)
        oh = _rms(jnp.concatenate(o[h * bb:(h + 1) * bb], axis=0), onorm_ref[...])
        og = merged(og_ref, lanes)
        o_ref[:, :, lanes] = (oh * (og * _sigmoid(og))).astype(o_ref.dtype).reshape(bb, c, dv)


def _gla(proj, gate_up, gate_bias, out_norm, tri, cols, bb, heads, dk, dv):
    batch, seq, _ = proj.shape
    nc = seq // CHUNK
    sel = (jnp.repeat(jnp.arange(SUB), dk)[:, None] == (jnp.arange(CHUNK) % SUB)[None, :]).astype(BF16)
    hq = next(m for m in (GLA_HEADS_PER_STEP, 1) if heads % m == 0)
    kw, vw = hq * dk, hq * dv
    cq, ck, cv, cog, cgd = (cols[n] for n in ("gla_q", "gla_k", "gla_v", "gla_og", "gla_gd"))
    return pl.pallas_call(
        functools.partial(_gla_kernel, dk=dk, dv=dv, bb=bb, hq=hq),
        grid=(batch // bb, heads // hq, nc),
        in_specs=[
            pl.BlockSpec((bb, CHUNK, kw), lambda b, h, c: (b, c, cq // kw + h)),
            pl.BlockSpec((bb, CHUNK, kw), lambda b, h, c: (b, c, ck // kw + h)),
            pl.BlockSpec((bb, CHUNK, vw), lambda b, h, c: (b, c, cv // vw + h)),
            pl.BlockSpec((bb, CHUNK, vw), lambda b, h, c: (b, c, cog // vw + h)),
            pl.BlockSpec((bb, CHUNK, LANES), lambda b, h, c: (b, c, cgd // LANES)),
            pl.BlockSpec((LANES, kw), lambda b, h, c: (0, h)),
            pl.BlockSpec((1, kw), lambda b, h, c: (0, h)),
            pl.BlockSpec((1, dv), lambda b, h, c: (0, 0)),
            pl.BlockSpec((bb * CHUNK, bb * CHUNK), lambda b, h, c: (0, 0)),
            pl.BlockSpec((SUB * dk, CHUNK), lambda b, h, c: (0, 0)),
        ],
        out_specs=pl.BlockSpec((bb, CHUNK, vw), lambda b, h, c: (b, c, h)),
        out_shape=jax.ShapeDtypeStruct((batch, seq, heads * dv), BF16),
        scratch_shapes=[pltpu.VMEM((bb, hq, dv, dk), F32)],
        compiler_params=_cparams(("parallel", "parallel", "arbitrary")),
    )(proj, proj, proj, proj, proj, gate_up, gate_bias, out_norm, tri, sel)


def _rwkv_kernel(r_ref, k_ref, v_ref, gd_ref, wa_ref, rp_ref, kp_ref, vp_ref, gdp_ref, wap_ref,
                 mur_ref, muk_ref, muv_ref, mugd_ref, muwa_ref,
                 w0_ref, w2_ref, a0_ref, a2_ref, g2_ref, kk_ref, ka_ref, rk_ref, lnw_ref, lnb_ref,
                 tri_ref, ones_ref, mstrict_ref, mincl_ref, mblk_ref,
                 o_ref, state_ref, *, lora, bb, gq):
    c, g = CHUNK, RWKV_GROUP
    n, rows, gw = g * c, bb * c, g * RWKV_HEAD
    first = pl.program_id(2) == 0

    @pl.when(first)
    def _():
        state_ref[...] = jnp.zeros_like(state_ref)

    keep_prev = jnp.where(first, 0.0, 1.0)

    def per_row(x):
        return [x[i * c:(i + 1) * c] for i in range(bb)]

    def shifted(cur_ref, prev_ref, mu_ref, lanes):
        p = cur_ref[:, :, lanes].reshape(rows, -1)
        width = p.shape[-1]
        prev = jnp.concatenate(
            [jnp.broadcast_to(prev_ref[i, 7:8, lanes] * keep_prev, (c, width)) for i in range(bb)], axis=0)
        is_row0 = (lax.broadcasted_iota(jnp.int32, p.shape, 0) & (c - 1)) == 0
        p_prev = jnp.where(is_row0, prev, pltpu.roll(p, 1, axis=0))
        return p + (p_prev - p) * mu_ref[:, lanes]

    everything = slice(None)
    wa = shifted(wa_ref, wap_ref, muwa_ref, everything)
    tanh_wd = jnp.tanh(wa[:, :lora]).astype(BF16)
    a_down = wa[:, lora:].astype(BF16)
    sig_gd = _sigmoid(shifted(gd_ref, gdp_ref, mugd_ref, everything)).astype(BF16)

    ones_bd = ones_ref[...]
    m_blk = mblk_ref[...]
    m_blk_bf16 = m_blk.astype(BF16)
    tri = tri_ref[...]

    r_all, k2_all, v_all, gate_all = [], [], [], []
    rt, kkt, km, bm, k_end, b_end, v_rows, c_end = [], [], [], [], [], [], [], []
    for q in range(gq):
        lanes = slice(q * gw, (q + 1) * gw)
        r = shifted(r_ref, rp_ref, mur_ref, lanes)
        k = shifted(k_ref, kp_ref, muk_ref, lanes)
        v = shifted(v_ref, vp_ref, muv_ref, lanes)
        w = -_softplus(-(w0_ref[:, lanes] + _dot(tanh_wd, w2_ref[:, lanes]))) - 0.5
        ld = -jnp.exp(w)
        a = _sigmoid(a0_ref[:, lanes] + _dot(a_down, a2_ref[:, lanes]))
        gate_all.append(_dot(sig_gd, g2_ref[:, lanes]))

        kkr = k * kk_ref[:, lanes]
        kk = kkr * lax.rsqrt(jnp.maximum(_dot(kkr * kkr, ones_bd), 1e-24))
        k2 = k * (1.0 + (a - 1.0) * ka_ref[:, lanes])
        bw = a * kk

        cum = _dot_split(tri, ld)
        ends = [cum[(i + 1) * c - 1:(i + 1) * c, :] for i in range(bb)]
        c_end_rows = jnp.concatenate([jnp.broadcast_to(e, (c, gw)) for e in ends], axis=0)
        e_neg = jnp.exp(-cum)
        e_end = jnp.exp(c_end_rows - cum)
        c_end += ends
        rt += per_row(r * jnp.exp(cum))
        kkt += per_row(kk * jnp.exp(cum - ld))
        km += per_row(k2 * e_neg)
        bm += per_row(bw * e_neg)
        k_end += per_row(k2 * e_end)
        b_end += per_row(bw * e_end)
        v_rows += per_row(v)
        r_all.append(r)
        k2_all.append(k2)
        v_all.append(v)

    def stack(x):
        return jnp.concatenate([x.astype(BF16)] * g, axis=0) * m_blk_bf16

    m_strict, m_incl = mstrict_ref[...], mincl_ref[...]
    lhs = [jnp.concatenate([kq, rq], axis=0).astype(BF16) for kq, rq in zip(kkt, rt)]
    aa = [_dot_nt(l, jnp.concatenate([stack(b_), stack(k_)], axis=0))
          for l, b_, k_ in zip(lhs, bm, km)]
    a_ab = [t[:c, :n] * m_strict for t in aa]
    a_kr = [jnp.concatenate([t[:c, n:] * m_strict, t[c:, n:] * m_incl], axis=0) for t in aa]
    a_rb = [t[c:, :n] * m_incl for t in aa]

    eye = m_incl - m_strict
    x = [eye - t for t in a_ab]
    p = [_dot(t, stack(t)) for t in a_ab]
    span = 2
    while span < c:
        span *= 2
        if span < c:
            xp = [_dot(jnp.concatenate([xi, pi], axis=0), stack(pi)) for xi, pi in zip(x, p)]
            x = [xi + t[:c] for xi, t in zip(x, xp)]
            p = [t[c:] for t in xp]
        else:
            x = [xi + _dot(xi, stack(pi)) for xi, pi in zip(x, p)]

    chains = [(q, i) for q in range(gq) for i in range(bb)]
    st = [state_ref[i, q] for q, i in chains]
    sk = [_dot_nt(l, s) for l, s in zip(lhs, st)]
    av = [_dot(t, stack(vi)) for t, vi in zip(a_kr, v_rows)]
    u = [_dot(xi, stack(-(s[:c] + t[:c]))) for xi, s, t in zip(x, sk, av)]
    y = [s[c:] + _dot(t, stack(ui)) + w_[c:] for s, t, ui, w_ in zip(sk, a_rb, u, av)]
    for j, (q, i) in enumerate(chains):
        state_ref[i, q] = st[j] * jnp.exp(c_end[j]) + (
            _dot_tn(u[j], b_end[j]) + _dot_tn(v_rows[j], k_end[j])) * m_blk

    inv_n = 1.0 / RWKV_HEAD
    for q in range(gq):
        lanes = slice(q * gw, (q + 1) * gw)
        yq = jnp.concatenate(y[q * bb:(q + 1) * bb], axis=0)
        mu = _dot(yq, ones_bd) * inv_n
        dy = yq - mu
        var = _dot(dy * dy, ones_bd) * inv_n
        yn = dy * lax.rsqrt(var + RWKV_LN_EPS) * lnw_ref[:, lanes] + lnb_ref[:, lanes]
        bonus = _dot(r_all[q] * k2_all[q] * rk_ref[:, lanes], ones_bd) * v_all[q]
        o_ref[:, :, lanes] = ((yn + bonus) * gate_all[q]).astype(o_ref.dtype).reshape(bb, c, gw)


def _rwkv(proj, mus, w0, w2, a0, a2, g2, k_k, k_a, r_k, ln_w, ln_b, consts, cols, bb, width, lora):
    batch, seq, _ = proj.shape
    nc = seq // CHUNK
    gw = RWKV_GROUP * RWKV_HEAD
    gq = next(m for m in (RWKV_GROUPS_PER_STEP, 1) if (width // gw) % m == 0)
    sw = gq * gw
    steps = width // sw
    gdw = g2.shape[0]
    waw = 2 * lora

    def prev_rows(c):
        return jnp.maximum(c * (CHUNK // 8) - 1, 0)

    def cur(w_, col, per_group):
        if per_group:
            return pl.BlockSpec((bb, CHUNK, w_), lambda b, q, c: (b, c, col // w_ + q))
        return pl.BlockSpec((bb, CHUNK, w_), lambda b, q, c: (b, c, col // w_))

    def prev(w_, col, per_group):
        if per_group:
            return pl.BlockSpec((bb, 8, w_), lambda b, q, c: (b, prev_rows(c), col // w_ + q))
        return pl.BlockSpec((bb, 8, w_), lambda b, q, c: (b, prev_rows(c), col // w_))

    def gvec():
        return pl.BlockSpec((1, sw), lambda b, q, c: (0, q))

    def full(shape):
        return pl.BlockSpec(shape, lambda b, q, c: (0, 0))

    n = RWKV_GROUP * CHUNK
    in_specs = [
        cur(sw, cols["rw_r"], True), cur(sw, cols["rw_k"], True), cur(sw, cols["rw_v"], True),
        cur(gdw, cols["rw_gd"], False), cur(waw, cols["rw_wa"], False),
        prev(sw, cols["rw_r"], True), prev(sw, cols["rw_k"], True), prev(sw, cols["rw_v"], True),
        prev(gdw, cols["rw_gd"], False), prev(waw, cols["rw_wa"], False),
        gvec(), gvec(), gvec(), full((1, gdw)), full((1, waw)),
        gvec(), pl.BlockSpec((lora, sw), lambda b, q, c: (0, q)),
        gvec(), pl.BlockSpec((lora, sw), lambda b, q, c: (0, q)),
        pl.BlockSpec((gdw, sw), lambda b, q, c: (0, q)),
        gvec(), gvec(), gvec(), gvec(), gvec(),
        full((bb * CHUNK, bb * CHUNK)), full((gw, gw)), full((CHUNK, n)), full((CHUNK, n)), full((n, gw)),
    ]
    return pl.pallas_call(
        functools.partial(_rwkv_kernel, lora=lora, bb=bb, gq=gq),
        grid=(batch // bb, steps, nc),
        in_specs=in_specs,
        out_specs=pl.BlockSpec((bb, CHUNK, sw), lambda b, q, c: (b, c, q)),
        out_shape=jax.ShapeDtypeStruct((batch, seq, width), BF16),
        scratch_shapes=[pltpu.VMEM((bb, gq, gw, gw), F32)],
        compiler_params=_cparams(("parallel", "parallel", "arbitrary")),
    )(proj, proj, proj, proj, proj, proj, proj, proj, proj, proj,
      mus["r"], mus["k"], mus["v"], mus["gd"], mus["wa"],
      w0, w2, a0, a2, g2, k_k, k_a, r_k, ln_w, ln_b,
      consts["tri_rows"], consts["ones_bd"], consts["m_strict"], consts["m_incl"], consts["m_blk"])


def _pad_cols(w, n):
    return jnp.pad(w, ((0, 0), (0, n - w.shape[1])))


def _pad_rows(w, n):
    return jnp.pad(w, ((0, n - w.shape[0]), (0, 0)))


def _layout(d_model):
    kw, vw, rw = d_model // 4, d_model // 2, d_model // 2
    lora = 128
    gdw = 512
    names = [("gla_q", kw), ("gla_k", kw), ("gla_v", vw), ("gla_og", vw),
             ("rw_r", rw), ("rw_k", rw), ("rw_v", rw),
             ("gate_gla", d_model), ("gate_rwkv", d_model),
             ("rw_gd", gdw), ("rw_wa", 2 * lora), ("gla_gd", LANES)]
    cols, off = {}, 0
    for name, w in names:
        assert off % w == 0, (name, off, w)
        cols[name] = off
        off += w
    return cols, off


def _chunk_consts(bb):
    c, g, hn = CHUNK, RWKV_GROUP, RWKV_HEAD
    assert c == hn
    n = g * c
    i = jnp.arange(n)
    ti, tj = jnp.arange(c)[:, None], i[None, :] % c
    lane = jnp.arange(g * hn)
    return {
        "tri_rows": jnp.kron(jnp.eye(bb, dtype=F32), jnp.tril(jnp.ones((c, c), F32))).astype(BF16),
        "ones_bd": ((lane[:, None] // hn) == (lane[None, :] // hn)).astype(BF16),
        "m_strict": (ti > tj).astype(F32),
        "m_incl": (ti >= tj).astype(F32),
        "m_blk": ((i[:, None] // c) == (lane[None, :] // hn)).astype(F32),
    }


def kernel(x, ffn1_pre_norm, ffn1_w_gate, ffn1_w_up, ffn1_w_down, ffn1_post_norm, mix_pre_norm, w_in, gla_gate_up, gla_gate_bias, gla_out_norm, rwkv_shift_mix, rwkv_w0, rwkv_w2, rwkv_a0, rwkv_a2, rwkv_g2, rwkv_k_k, rwkv_k_a, rwkv_r_k, rwkv_ln_w, rwkv_ln_b, w_up_gla, w_up_rwkv, w_out, mix_post_norm, ffn2_pre_norm, ffn2_w_gate, ffn2_w_up, ffn2_w_down, ffn2_post_norm):
    batch, seq, d_model = x.shape
    depth = ffn1_pre_norm.shape[0]
    t = batch * seq
    assert seq % CHUNK == 0 and d_model % 1024 == 0

    gla_kw, gla_vw, rw_w = d_model // 4, d_model // 2, d_model // 2
    gla_heads = max(4, d_model // 512)
    dk, dv = gla_kw // gla_heads, gla_vw // gla_heads
    lora = rwkv_w2.shape[1]
    g_lora = rwkv_g2.shape[1]
    assert lora == 128 and g_lora <= 512 and dk % LANES == 0

    cols, n_proj = _layout(d_model)
    n_proj_pad = -(-n_proj // 1024) * 1024
    bm = min(1024, t)
    bm_down = min(512, t)
    bm_gu = min(2048, t)
    bt = min(256, t)
    bb = next(n for n in (4, 2, 1) if batch % n == 0)
    consts = _chunk_consts(bb)

    def row(v):
        return v.reshape(1, -1).astype(F32)

    h = x.reshape(t, d_model)
    for l in range(depth):
        xn = _norm_cast(h, row(ffn1_pre_norm[l]), bt)
        hid = _gate_up(xn, ffn1_w_gate[l], ffn1_w_up[l], bm_gu, 256)
        f = _matmul(hid, ffn1_w_down[l].astype(BF16), F32, bm_down, 512)
        h, u = _post_pre(h, f, row(ffn1_post_norm[l]), row(mix_pre_norm[l]), MACARON_WEIGHT, bt)

        wi = w_in[l]
        o_gla, o_rw, o_gate = 0, 2 * gla_kw + 2 * gla_vw + GLA_GATE_RANK, None
        gq, gk, gv, ggd, gog = jnp.split(wi[:, :o_rw], [gla_kw, 2 * gla_kw, 2 * gla_kw + gla_vw,
                                                       2 * gla_kw + gla_vw + GLA_GATE_RANK], axis=1)
        rw_sizes = [rw_w, lora, rw_w, rw_w, lora, g_lora]
        rw_off = [o_rw]
        for s in rw_sizes:
            rw_off.append(rw_off[-1] + s)
        rr, rwd, rk, rv, rad, rgd = (wi[:, rw_off[i]:rw_off[i + 1]] for i in range(6))
        gates = wi[:, rw_off[-1]:]
        w_proj = jnp.concatenate(
            [gq, gk, gv, gog, rr, rk, rv, gates, _pad_cols(rgd, 512), rwd, rad, _pad_cols(ggd, LANES),
             jnp.zeros((d_model, n_proj_pad - n_proj), F32)], axis=1).astype(BF16)
        proj = _matmul(u, w_proj, F32, bm, 1024)

        mu = rwkv_shift_mix[l]
        mu_off = [o - o_rw for o in rw_off]
        mu_r, mu_wd, mu_k, mu_v, mu_ad, mu_gd = (mu[mu_off[i]:mu_off[i + 1]] for i in range(6))
        mus = {"r": row(mu_r), "k": row(mu_k), "v": row(mu_v),
               "gd": row(jnp.pad(mu_gd, (0, 512 - g_lora))),
               "wa": row(jnp.concatenate([mu_wd, mu_ad]))}

        proj3 = proj.reshape(batch, seq, n_proj_pad)
        o_gla = _gla(proj3, _pad_rows(gla_gate_up[l], LANES).astype(BF16), row(gla_gate_bias[l]),
                     row(gla_out_norm[l]), consts["tri_rows"], cols, bb, gla_heads, dk, dv)
        o_rwkv = _rwkv(proj3, mus, row(rwkv_w0[l]), rwkv_w2[l].astype(BF16), row(rwkv_a0[l]),
                       rwkv_a2[l].astype(BF16), _pad_rows(rwkv_g2[l], 512).astype(BF16),
                       row(rwkv_k_k[l]), row(rwkv_k_a[l]), row(rwkv_r_k[l]), row(rwkv_ln_w[l]),
                       row(rwkv_ln_b[l]), consts, cols, bb, rw_w, lora)

        merged = _merge(o_gla.reshape(t, gla_vw), o_rwkv.reshape(t, rw_w),
                        w_up_gla[l].astype(BF16), w_up_rwkv[l].astype(BF16), proj,
                        cols["gate_gla"], cols["gate_rwkv"], bm, 512)
        mixed = _matmul(merged, w_out[l].astype(BF16), F32, bm, 1024)
        h, u = _post_pre(h, mixed, row(mix_post_norm[l]), row(ffn2_pre_norm[l]), 1.0, bt)

        hid = _gate_up(u, ffn2_w_gate[l], ffn2_w_up[l], bm_gu, 256)
        f = _matmul(hid, ffn2_w_down[l].astype(BF16), F32, bm_down, 512)
        h = _post(h, f, row(ffn2_post_norm[l]), MACARON_WEIGHT, bt)
    return h.reshape(batch, seq, d_model)
```

```python
import functools

import jax
import jax.numpy as jnp
from jax import lax
from jax.experimental import pallas as pl
from jax.experimental.pallas import tpu as pltpu

F32 = jnp.float32
BF16 = jnp.bfloat16

NORM_EPS = 1e-6
RWKV_LN_EPS = 64e-5
MACARON_WEIGHT = 0.5
GLA_GATE_NORM = 16.0
GLA_GATE_RANK = 16
LOG2_E = 1.4426950408889634
RWKV_HEAD = 64
RWKV_GATE_LORA = 480

LANES = 128
CHUNK = 64
SUB_LOG2 = 4
SUB = 1 << SUB_LOG2
GLA_HEADS_PER_STEP = 2
RWKV_GROUP = 4
RWKV_GROUPS_PER_STEP = 4
VMEM_LIMIT = 56 * 1024 * 1024
VMEM_LIMIT_MAX = 60 * 1024 * 1024


def _cparams(sem):
    return pltpu.CompilerParams(dimension_semantics=sem, vmem_limit_bytes=VMEM_LIMIT)


def _dot(a, b):
    return jnp.dot(a.astype(BF16), b.astype(BF16), preferred_element_type=F32)


def _dot_nt(a, b):
    return lax.dot_general(a.astype(BF16), b.astype(BF16), (((1,), (1,)), ((), ())),
                           preferred_element_type=F32)


def _dot_tn(a, b):
    return lax.dot_general(a.astype(BF16), b.astype(BF16), (((0,), (0,)), ((), ())),
                           preferred_element_type=F32)


def _dot_split(m01, x):
    hi = x.astype(BF16)
    lo = (x - hi.astype(F32)).astype(BF16)
    return (jnp.dot(m01, hi, preferred_element_type=F32)
            + jnp.dot(m01, lo, preferred_element_type=F32))


def _rms(x, g):
    ms = jnp.mean(x * x, axis=-1, keepdims=True)
    return x * lax.rsqrt(ms + NORM_EPS) * g


def _sigmoid(x):
    return 1.0 / (1.0 + jnp.exp(-x))


def _softplus(x):
    return jnp.maximum(x, 0.0) + jnp.log(1.0 + jnp.exp(-jnp.abs(x)))


def _norm_cast_kernel(x_ref, g_ref, o_ref):
    o_ref[...] = _rms(x_ref[...], g_ref[...]).astype(o_ref.dtype)


def _norm_cast(x, g, bt):
    t, d = x.shape
    return pl.pallas_call(
        _norm_cast_kernel,
        grid=(t // bt,),
        in_specs=[pl.BlockSpec((bt, d), lambda i: (i, 0)),
                  pl.BlockSpec((1, d), lambda i: (0, 0))],
        out_specs=pl.BlockSpec((bt, d), lambda i: (i, 0)),
        out_shape=jax.ShapeDtypeStruct((t, d), BF16),
        compiler_params=_cparams(("parallel",)),
    )(x, g)


def _post_pre_kernel(h_ref, f_ref, gpost_ref, gpre_ref, hout_ref, u_ref, *, alpha):
    h = h_ref[...] + alpha * _rms(f_ref[...].astype(F32), gpost_ref[...])
    hout_ref[...] = h
    u_ref[...] = _rms(h, gpre_ref[...]).astype(u_ref.dtype)


def _post_pre(h, f, g_post, g_pre, alpha, bt):
    t, d = h.shape
    row = pl.BlockSpec((bt, d), lambda i: (i, 0))
    vec = pl.BlockSpec((1, d), lambda i: (0, 0))
    return pl.pallas_call(
        functools.partial(_post_pre_kernel, alpha=alpha),
        grid=(t // bt,),
        in_specs=[row, row, vec, vec],
        out_specs=[row, row],
        out_shape=[jax.ShapeDtypeStruct((t, d), F32), jax.ShapeDtypeStruct((t, d), BF16)],
        compiler_params=_cparams(("parallel",)),
    )(h, f, g_post, g_pre)


def _post_kernel(h_ref, f_ref, gpost_ref, hout_ref, *, alpha):
    hout_ref[...] = h_ref[...] + alpha * _rms(f_ref[...].astype(F32), gpost_ref[...])


def _post(h, f, g_post, alpha, bt):
    t, d = h.shape
    row = pl.BlockSpec((bt, d), lambda i: (i, 0))
    vec = pl.BlockSpec((1, d), lambda i: (0, 0))
    return pl.pallas_call(
        functools.partial(_post_kernel, alpha=alpha),
        grid=(t // bt,),
        in_specs=[row, row, vec],
        out_specs=row,
        out_shape=jax.ShapeDtypeStruct((t, d), F32),
        compiler_params=_cparams(("parallel",)),
    )(h, f, g_post)


def _matmul_kernel(x_ref, w_ref, o_ref):
    o_ref[...] = jnp.dot(x_ref[...], w_ref[...], preferred_element_type=F32).astype(o_ref.dtype)


def _matmul(x, w, out_dtype, bm, bn):
    m, k = x.shape
    n = w.shape[1]
    return pl.pallas_call(
        _matmul_kernel,
        grid=(m // bm, pl.cdiv(n, bn)),
        in_specs=[pl.BlockSpec((bm, k), lambda i, j: (i, 0)),
                  pl.BlockSpec((k, bn), lambda i, j: (0, j))],
        out_specs=pl.BlockSpec((bm, bn), lambda i, j: (i, j)),
        out_shape=jax.ShapeDtypeStruct((m, n), out_dtype),
        compiler_params=_cparams(("parallel", "arbitrary")),
    )(x, w)


def _gate_up_kernel(x_ref, wg_ref, wu_ref, o_ref):
    x = x_ref[...]
    g = jnp.dot(x, wg_ref[...].astype(BF16), preferred_element_type=F32)
    u = jnp.dot(x, wu_ref[...].astype(BF16), preferred_element_type=F32)
    o_ref[...] = (g * _sigmoid(g) * u).astype(o_ref.dtype)


def _gate_up(x, wg, wu, bm, bn):
    m, k = x.shape
    n = wg.shape[1]
    wspec = pl.BlockSpec((k, bn), lambda i, j: (0, j))
    return pl.pallas_call(
        _gate_up_kernel,
        grid=(m // bm, pl.cdiv(n, bn)),
        in_specs=[pl.BlockSpec((bm, k), lambda i, j: (i, 0)), wspec, wspec],
        out_specs=pl.BlockSpec((bm, bn), lambda i, j: (i, j)),
        out_shape=jax.ShapeDtypeStruct((m, n), BF16),
        compiler_params=pltpu.CompilerParams(dimension_semantics=("parallel", "arbitrary"),
                                             vmem_limit_bytes=VMEM_LIMIT_MAX),
    )(x, wg, wu)


def _merge_kernel(og_ref, or_ref, wg_ref, wr_ref, gg_ref, gr_ref, o_ref):
    yg = jnp.dot(og_ref[...], wg_ref[...], preferred_element_type=F32)
    yr = jnp.dot(or_ref[...], wr_ref[...], preferred_element_type=F32)
    o_ref[...] = (_sigmoid(gg_ref[...]) * yg + _sigmoid(gr_ref[...]) * yr).astype(o_ref.dtype)


def _merge(o_gla, o_rwkv, w_up_gla, w_up_rwkv, proj, col_gate_gla, col_gate_rwkv, bm, bn):
    m, k = o_gla.shape
    n = w_up_gla.shape[1]
    xspec = pl.BlockSpec((bm, k), lambda i, j: (i, 0))
    wspec = pl.BlockSpec((k, bn), lambda i, j: (0, j))
    ga, gb = col_gate_gla // bn, col_gate_rwkv // bn
    return pl.pallas_call(
        _merge_kernel,
        grid=(m // bm, n // bn),
        in_specs=[xspec, xspec, wspec, wspec,
                  pl.BlockSpec((bm, bn), lambda i, j: (i, ga + j)),
                  pl.BlockSpec((bm, bn), lambda i, j: (i, gb + j))],
        out_specs=pl.BlockSpec((bm, bn), lambda i, j: (i, j)),
        out_shape=jax.ShapeDtypeStruct((m, n), BF16),
        compiler_params=_cparams(("parallel", "arbitrary")),
    )(o_gla, o_rwkv, w_up_gla, w_up_rwkv, proj, proj)


def _gla_kernel(q_ref, k_ref, v_ref, og_ref, gd_ref, gup_ref, gbias_ref, onorm_ref, tri_ref, sel_ref,
                o_ref, state_ref, *, dk, dv, bb, hq):
    c, nsub = CHUNK, CHUNK // SUB
    rows = bb * c

    @pl.when(pl.program_id(2) == 0)
    def _():
        state_ref[...] = jnp.zeros_like(state_ref)

    def merged(ref, lanes=slice(None)):
        return ref[:, :, lanes].reshape(rows, -1)

    def per_row(x):
        return [x[i * c:(i + 1) * c] for i in range(bb)]

    def chunk_row(x, r):
        return jnp.concatenate(
            [jnp.broadcast_to(x[i * c + r:i * c + r + 1, :], (c, x.shape[-1])) for i in range(bb)], axis=0)

    row = lax.broadcasted_iota(jnp.int32, (rows, c), 0) & (c - 1)
    lane = lax.broadcasted_iota(jnp.int32, (rows, c), 1)
    blk0 = row & -SUB
    d = lane - blk0
    dc = jnp.where(d >= 0, jnp.where(d <= (row & (SUB - 1)), d, -1), -1)
    mask_off = (lane < blk0).astype(F32)
    sub_id = (lax.broadcasted_iota(jnp.int32, (rows, dk), 0) & (c - 1)) >> SUB_LOG2
    gd = merged(gd_ref).astype(BF16)
    tri = tri_ref[...]

    qs, ks, bs, qhat, khat, b_ends = [], [], [], [], [], []
    for h in range(hq):
        lanes = slice(h * dk, (h + 1) * dk)
        x = _dot(gd, gup_ref[:, lanes]) + gbias_ref[:, lanes]
        log_a = -_softplus(-x) * (1.0 / GLA_GATE_NORM)
        b = _dot_split(tri, log_a * LOG2_E)
        q = merged(q_ref, lanes) * (dk ** -0.5)
        k = merged(k_ref, lanes)
        refs = [chunk_row(b, i * SUB - 1) for i in range(1, nsub)]
        bref = jnp.zeros((rows, dk), F32)
        for i in range(1, nsub):
            bref = jnp.where(sub_id == i, refs[i - 1], bref)
        qhat.append(per_row(q * jnp.exp2(b - bref)))
        khat.append([per_row(k * jnp.exp2(jnp.minimum(ref - b, 0.0))) for ref in refs])
        qs.append(q)
        ks.append(k)
        bs.append(b)
        b_ends.append(chunk_row(b, c - 1))

    attn = [jnp.concatenate(
        [jnp.concatenate(
            [jnp.zeros((SUB, c), F32)]
            + [_dot_nt(qhat[h][r][i * SUB:(i + 1) * SUB, :], khat[h][i - 1][r]) for i in range(1, nsub)],
            axis=0)
         for r in range(bb)], axis=0) * mask_off for h in range(hq)]

    for h in range(hq):
        q, k, b = qs[h], ks[h], bs[h]
        k3 = k.reshape(bb * nsub, SUB, dk)
        b3 = b.reshape(bb * nsub, SUB, dk)
        terms = []
        for j in range(SUB):
            kj = jnp.broadcast_to(k3[:, j:j + 1, :], (bb * nsub, SUB, dk)).reshape(rows, dk)
            bj = jnp.broadcast_to(b3[:, j:j + 1, :], (bb * nsub, SUB, dk)).reshape(rows, dk)
            terms.append((q * kj * jnp.exp2(jnp.minimum(b - bj, 0.0))).astype(BF16))
        diag = jnp.dot(jnp.concatenate(terms, axis=1), sel_ref[...], preferred_element_type=F32)
        attn[h] = jnp.where(dc >= 0, diag, attn[h])

    chains = [(h, i) for h in range(hq) for i in range(bb)]
    attn_c = [t for h in range(hq) for t in per_row(attn[h])]
    qe = [t for h in range(hq) for t in per_row(qs[h] * jnp.exp2(bs[h]))]
    k_end = [t for h in range(hq) for t in per_row(ks[h] * jnp.exp2(b_ends[h] - bs[h]))]
    v_c = [v_ref[i, :, h * dv:(h + 1) * dv] for h, i in chains]
    st = [state_ref[i, h] for h, i in chains]
    o = [_dot(a_, v_) + _dot_nt(q_, s_) for a_, v_, q_, s_ in zip(attn_c, v_c, qe, st)]
    for j, (h, i) in enumerate(chains):
        state_ref[i, h] = st[j] * jnp.exp2(b_ends[h][i * c:i * c + 1, :]) + _dot_tn(v_c[j], k_end[j])

    for h in range(hq):
        lanes = slice(h * dv, (h + 1) * dv)
        oh = _rms(jnp.concatenate(o[h * bb:(h + 1) * bb], axis=0), onorm_ref[...])
        og = merged(og_ref, lanes)
        o_ref[:, :, lanes] = (oh * (og * _sigmoid(og))).astype(o_ref.dtype).reshape(bb, c, dv)


def _gla(proj, gate_up, gate_bias, out_norm, tri, cols, bb, heads, dk, dv):
    batch, seq, _ = proj.shape
    nc = seq // CHUNK
    sel = (jnp.repeat(jnp.arange(SUB), dk)[:, None] == (jnp.arange(CHUNK) % SUB)[None, :]).astype(BF16)
    hq = next(m for m in (GLA_HEADS_PER_STEP, 1) if heads % m == 0)
    kw, vw = hq * dk, hq * dv
    cq, ck, cv, cog, cgd = (cols[n] for n in ("gla_q", "gla_k", "gla_v", "gla_og", "gla_gd"))
    return pl.pallas_call(
        functools.partial(_gla_kernel, dk=dk, dv=dv, bb=bb, hq=hq),
        grid=(batch // bb, heads // hq, nc),
        in_specs=[
            pl.BlockSpec((bb, CHUNK, kw), lambda b, h, c: (b, c, cq // kw + h)),
            pl.BlockSpec((bb, CHUNK, kw), lambda b, h, c: (b, c, ck // kw + h)),
            pl.BlockSpec((bb, CHUNK, vw), lambda b, h, c: (b, c, cv // vw + h)),
            pl.BlockSpec((bb, CHUNK, vw), lambda b, h, c: (b, c, cog // vw + h)),
            pl.BlockSpec((bb, CHUNK, LANES), lambda b, h, c: (b, c, cgd // LANES)),
            pl.BlockSpec((LANES, kw), lambda b, h, c: (0, h)),
            pl.BlockSpec((1, kw), lambda b, h, c: (0, h)),
            pl.BlockSpec((1, dv), lambda b, h, c: (0, 0)),
            pl.BlockSpec((bb * CHUNK, bb * CHUNK), lambda b, h, c: (0, 0)),
            pl.BlockSpec((SUB * dk, CHUNK), lambda b, h, c: (0, 0)),
        ],
        out_specs=pl.BlockSpec((bb, CHUNK, vw), lambda b, h, c: (b, c, h)),
        out_shape=jax.ShapeDtypeStruct((batch, seq, heads * dv), BF16),
        scratch_shapes=[pltpu.VMEM((bb, hq, dv, dk), F32)],
        compiler_params=_cparams(("parallel", "parallel", "arbitrary")),
    )(proj, proj, proj, proj, proj, gate_up, gate_bias, out_norm, tri, sel)


def _rwkv_kernel(r_ref, k_ref, v_ref, gd_ref, wa_ref, rp_ref, kp_ref, vp_ref, gdp_ref, wap_ref,
                 mur_ref, muk_ref, muv_ref, mugd_ref, muwa_ref,
                 w0_ref, w2_ref, a0_ref, a2_ref, g2_ref, kk_ref, ka_ref, rk_ref, lnw_ref, lnb_ref,
                 tri_ref, ones_ref, mstrict_ref, mincl_ref, mblk_ref,
                 o_ref, state_ref, *, lora, bb, gq):
    c, g = CHUNK, RWKV_GROUP
    n, rows, gw = g * c, bb * c, g * RWKV_HEAD
    first = pl.program_id(2) == 0

    @pl.when(first)
    def _():
        state_ref[...] = jnp.zeros_like(state_ref)

    keep_prev = jnp.where(first, 0.0, 1.0)

    def per_row(x):
        return [x[i * c:(i + 1) * c] for i in range(bb)]

    def shifted(cur_ref, prev_ref, mu_ref, lanes):
        p = cur_ref[:, :, lanes].reshape(rows, -1)
        width = p.shape[-1]
        prev = jnp.concatenate(
            [jnp.broadcast_to(prev_ref[i, 7:8, lanes] * keep_prev, (c, width)) for i in range(bb)], axis=0)
        is_row0 = (lax.broadcasted_iota(jnp.int32, p.shape, 0) & (c - 1)) == 0
        p_prev = jnp.where(is_row0, prev, pltpu.roll(p, 1, axis=0))
        return p + (p_prev - p) * mu_ref[:, lanes]

    everything = slice(None)
    wa = shifted(wa_ref, wap_ref, muwa_ref, everything)
    tanh_wd = jnp.tanh(wa[:, :lora]).astype(BF16)
    a_down = wa[:, lora:].astype(BF16)
    sig_gd = _sigmoid(shifted(gd_ref, gdp_ref, mugd_ref, everything)).astype(BF16)

    ones_bd = ones_ref[...]
    m_blk = mblk_ref[...]
    m_blk_bf16 = m_blk.astype(BF16)
    tri = tri_ref[...]

    r_all, k2_all, v_all, gate_all = [], [], [], []
    rt, kkt, km, bm, k_end, b_end, v_rows, c_end = [], [], [], [], [], [], [], []
    for q in range(gq):
        lanes = slice(q * gw, (q + 1) * gw)
        r = shifted(r_ref, rp_ref, mur_ref, lanes)
        k = shifted(k_ref, kp_ref, muk_ref, lanes)
        v = shifted(v_ref, vp_ref, muv_ref, lanes)
        w = -_softplus(-(w0_ref[:, lanes] + _dot(tanh_wd, w2_ref[:, lanes]))) - 0.5
        ld = -jnp.exp(w)
        a = _sigmoid(a0_ref[:, lanes] + _dot(a_down, a2_ref[:, lanes]))
        gate_all.append(_dot(sig_gd, g2_ref[:, lanes]))

        kkr = k * kk_ref[:, lanes]
        kk = kkr * lax.rsqrt(jnp.maximum(_dot(kkr * kkr, ones_bd), 1e-24))
        k2 = k * (1.0 + (a - 1.0) * ka_ref[:, lanes])
        bw = a * kk

        cum = _dot_split(tri, ld)
        ends = [cum[(i + 1) * c - 1:(i + 1) * c, :] for i in range(bb)]
        c_end_rows = jnp.concatenate([jnp.broadcast_to(e, (c, gw)) for e in ends], axis=0)
        e_neg = jnp.exp(-cum)
        e_end = jnp.exp(c_end_rows - cum)
        c_end += ends
        rt += per_row(r * jnp.exp(cum))
        kkt += per_row(kk * jnp.exp(cum - ld))
        km += per_row(k2 * e_neg)
        bm += per_row(bw * e_neg)
        k_end += per_row(k2 * e_end)
        b_end += per_row(bw * e_end)
        v_rows += per_row(v)
        r_all.append(r)
        k2_all.append(k2)
        v_all.append(v)

    def stack(x):
        return jnp.concatenate([x.astype(BF16)] * g, axis=0) * m_blk_bf16

    m_strict, m_incl = mstrict_ref[...], mincl_ref[...]
    lhs = [jnp.concatenate([kq, rq], axis=0).astype(BF16) for kq, rq in zip(kkt, rt)]
    aa = [_dot_nt(l, jnp.concatenate([stack(b_), stack(k_)], axis=0))
          for l, b_, k_ in zip(lhs, bm, km)]
    a_ab = [t[:c, :n] * m_strict for t in aa]
    a_kr = [jnp.concatenate([t[:c, n:] * m_strict, t[c:, n:] * m_incl], axis=0) for t in aa]
    a_rb = [t[c:, :n] * m_incl for t in aa]

    eye = m_incl - m_strict
    x = [eye - t for t in a_ab]
    p = [_dot(t, stack(t)) for t in a_ab]
    span = 2
    while span < c:
        span *= 2
        if span < c:
            xp = [_dot(jnp.concatenate([xi, pi], axis=0), stack(pi)) for xi, pi in zip(x, p)]
            x = [xi + t[:c] for xi, t in zip(x, xp)]
            p = [t[c:] for t in xp]
        else:
            x = [xi + _dot(xi, stack(pi)) for xi, pi in zip(x, p)]

    chains = [(q, i) for q in range(gq) for i in range(bb)]
    st = [state_ref[i, q] for q, i in chains]
    sk = [_dot_nt(l, s) for l, s in zip(lhs, st)]
    av = [_dot(t, stack(vi)) for t, vi in zip(a_kr, v_rows)]
    u = [_dot(xi, stack(-(s[:c] + t[:c]))) for xi, s, t in zip(x, sk, av)]
    y = [s[c:] + _dot(t, stack(ui)) + w_[c:] for s, t, ui, w_ in zip(sk, a_rb, u, av)]
    for j, (q, i) in enumerate(chains):
        state_ref[i, q] = st[j] * jnp.exp(c_end[j]) + (
            _dot_tn(u[j], b_end[j]) + _dot_tn(v_rows[j], k_end[j])) * m_blk

    inv_n = 1.0 / RWKV_HEAD
    for q in range(gq):
        lanes = slice(q * gw, (q + 1) * gw)
        yq = jnp.concatenate(y[q * bb:(q + 1) * bb], axis=0)
        mu = _dot(yq, ones_bd) * inv_n
        dy = yq - mu
        var = _dot(dy * dy, ones_bd) * inv_n
        yn = dy * lax.rsqrt(var + RWKV_LN_EPS) * lnw_ref[:, lanes] + lnb_ref[:, lanes]
        bonus = _dot(r_all[q] * k2_all[q] * rk_ref[:, lanes], ones_bd) * v_all[q]
        o_ref[:, :, lanes] = ((yn + bonus) * gate_all[q]).astype(o_ref.dtype).reshape(bb, c, gw)


def _rwkv(proj, mus, w0, w2, a0, a2, g2, k_k, k_a, r_k, ln_w, ln_b, consts, cols, bb, width, lora):
    batch, seq, _ = proj.shape
    nc = seq // CHUNK
    gw = RWKV_GROUP * RWKV_HEAD
    gq = next(m for m in (RWKV_GROUPS_PER_STEP, 1) if (width // gw) % m == 0)
    sw = gq * gw
    steps = width // sw
    gdw = g2.shape[0]
    waw = 2 * lora

    def prev_rows(c):
        return jnp.maximum(c * (CHUNK // 8) - 1, 0)

    def cur(w_, col, per_group):
        if per_group:
            return pl.BlockSpec((bb, CHUNK, w_), lambda b, q, c: (b, c, col // w_ + q))
        return pl.BlockSpec((bb, CHUNK, w_), lambda b, q, c: (b, c, col // w_))

    def prev(w_, col, per_group):
        if per_group:
            return pl.BlockSpec((bb, 8, w_), lambda b, q, c: (b, prev_rows(c), col // w_ + q))
        return pl.BlockSpec((bb, 8, w_), lambda b, q, c: (b, prev_rows(c), col // w_))

    def gvec():
        return pl.BlockSpec((1, sw), lambda b, q, c: (0, q))

    def full(shape):
        return pl.BlockSpec(shape, lambda b, q, c: (0, 0))

    n = RWKV_GROUP * CHUNK
    in_specs = [
        cur(sw, cols["rw_r"], True), cur(sw, cols["rw_k"], True), cur(sw, cols["rw_v"], True),
        cur(gdw, cols["rw_gd"], False), cur(waw, cols["rw_wa"], False),
        prev(sw, cols["rw_r"], True), prev(sw, cols["rw_k"], True), prev(sw, cols["rw_v"], True),
        prev(gdw, cols["rw_gd"], False), prev(waw, cols["rw_wa"], False),
        gvec(), gvec(), gvec(), full((1, gdw)), full((1, waw)),
        gvec(), pl.BlockSpec((lora, sw), lambda b, q, c: (0, q)),
        gvec(), pl.BlockSpec((lora, sw), lambda b, q, c: (0, q)),
        pl.BlockSpec((gdw, sw), lambda b, q, c: (0, q)),
        gvec(), gvec(), gvec(), gvec(), gvec(),
        full((bb * CHUNK, bb * CHUNK)), full((gw, gw)), full((CHUNK, n)), full((CHUNK, n)), full((n, gw)),
    ]
    return pl.pallas_call(
        functools.partial(_rwkv_kernel, lora=lora, bb=bb, gq=gq),
        grid=(batch // bb, steps, nc),
        in_specs=in_specs,
        out_specs=pl.BlockSpec((bb, CHUNK, sw), lambda b, q, c: (b, c, q)),
        out_shape=jax.ShapeDtypeStruct((batch, seq, width), BF16),
        scratch_shapes=[pltpu.VMEM((bb, gq, gw, gw), F32)],
        compiler_params=_cparams(("parallel", "parallel", "arbitrary")),
    )(proj, proj, proj, proj, proj, proj, proj, proj, proj, proj,
      mus["r"], mus["k"], mus["v"], mus["gd"], mus["wa"],
      w0, w2, a0, a2, g2, k_k, k_a, r_k, ln_w, ln_b,
      consts["tri_rows"], consts["ones_bd"], consts["m_strict"], consts["m_incl"], consts["m_blk"])


def _pad_cols(w, n):
    return jnp.pad(w, ((0, 0), (0, n - w.shape[1])))


def _pad_rows(w, n):
    return jnp.pad(w, ((0, n - w.shape[0]), (0, 0)))


def _layout(d_model):
    kw, vw, rw = d_model // 4, d_model // 2, d_model // 2
    lora = 128
    gdw = 512
    names = [("gla_q", kw), ("gla_k", kw), ("gla_v", vw), ("gla_og", vw),
             ("rw_r", rw), ("rw_k", rw), ("rw_v", rw),
             ("gate_gla", d_model), ("gate_rwkv", d_model),
             ("rw_gd", gdw), ("rw_wa", 2 * lora), ("gla_gd", LANES)]
    cols, off = {}, 0
    for name, w in names:
        assert off % w == 0, (name, off, w)
        cols[name] = off
        off += w
    return cols, off


def _chunk_consts(bb):
    c, g, hn = CHUNK, RWKV_GROUP, RWKV_HEAD
    assert c == hn
    n = g * c
    i = jnp.arange(n)
    ti, tj = jnp.arange(c)[:, None], i[None, :] % c
    lane = jnp.arange(g * hn)
    return {
        "tri_rows": jnp.kron(jnp.eye(bb, dtype=F32), jnp.tril(jnp.ones((c, c), F32))).astype(BF16),
        "ones_bd": ((lane[:, None] // hn) == (lane[None, :] // hn)).astype(BF16),
        "m_strict": (ti > tj).astype(F32),
        "m_incl": (ti >= tj).astype(F32),
        "m_blk": ((i[:, None] // c) == (lane[None, :] // hn)).astype(F32),
    }


def kernel(x, ffn1_pre_norm, ffn1_w_gate, ffn1_w_up, ffn1_w_down, ffn1_post_norm, mix_pre_norm, w_in, gla_gate_up, gla_gate_bias, gla_out_norm, rwkv_shift_mix, rwkv_w0, rwkv_w2, rwkv_a0, rwkv_a2, rwkv_g2, rwkv_k_k, rwkv_k_a, rwkv_r_k, rwkv_ln_w, rwkv_ln_b, w_up_gla, w_up_rwkv, w_out, mix_post_norm, ffn2_pre_norm, ffn2_w_gate, ffn2_w_up, ffn2_w_down, ffn2_post_norm):
    batch, seq, d_model = x.shape
    depth = ffn1_pre_norm.shape[0]
    t = batch * seq
    assert seq % CHUNK == 0 and d_model % 1024 == 0

    gla_kw, gla_vw, rw_w = d_model // 4, d_model // 2, d_model // 2
    gla_heads = max(4, d_model // 512)
    dk, dv = gla_kw // gla_heads, gla_vw // gla_heads
    lora = rwkv_w2.shape[1]
    g_lora = rwkv_g2.shape[1]
    assert lora == 128 and g_lora <= 512 and dk % LANES == 0

    cols, n_proj = _layout(d_model)
    n_proj_pad = -(-n_proj // 1024) * 1024
    bm = min(1024, t)
    bm_down = min(512, t)
    bm_gu = min(2048, t)
    bt = min(256, t)
    bb = next(n for n in (4, 2, 1) if batch % n == 0)
    consts = _chunk_consts(bb)

    def row(v):
        return v.reshape(1, -1).astype(F32)

    h = x.reshape(t, d_model)
    for l in range(depth):
        xn = _norm_cast(h, row(ffn1_pre_norm[l]), bt)
        hid = _gate_up(xn, ffn1_w_gate[l], ffn1_w_up[l], bm_gu, 256)
        f = _matmul(hid, ffn1_w_down[l].astype(BF16), BF16, bm_down, 512)
        h, u = _post_pre(h, f, row(ffn1_post_norm[l]), row(mix_pre_norm[l]), MACARON_WEIGHT, bt)

        wi = w_in[l]
        o_gla, o_rw, o_gate = 0, 2 * gla_kw + 2 * gla_vw + GLA_GATE_RANK, None
        gq, gk, gv, ggd, gog = jnp.split(wi[:, :o_rw], [gla_kw, 2 * gla_kw, 2 * gla_kw + gla_vw,
                                                       2 * gla_kw + gla_vw + GLA_GATE_RANK], axis=1)
        rw_sizes = [rw_w, lora, rw_w, rw_w, lora, g_lora]
        rw_off = [o_rw]
        for s in rw_sizes:
            rw_off.append(rw_off[-1] + s)
        rr, rwd, rk, rv, rad, rgd = (wi[:, rw_off[i]:rw_off[i + 1]] for i in range(6))
        gates = wi[:, rw_off[-1]:]
        w_proj = jnp.concatenate(
            [piece.astype(BF16) for piece in
             (gq, gk, gv, gog, rr, rk, rv, gates, _pad_cols(rgd, 512), rwd, rad, _pad_cols(ggd, LANES),
              jnp.zeros((d_model, n_proj_pad - n_proj), F32))], axis=1)
        proj = _matmul(u, w_proj, F32, bm, 1024)

        mu = rwkv_shift_mix[l]
        mu_off = [o - o_rw for o in rw_off]
        mu_r, mu_wd, mu_k, mu_v, mu_ad, mu_gd = (mu[mu_off[i]:mu_off[i + 1]] for i in range(6))
        mus = {"r": row(mu_r), "k": row(mu_k), "v": row(mu_v),
               "gd": row(jnp.pad(mu_gd, (0, 512 - g_lora))),
               "wa": row(jnp.concatenate([mu_wd, mu_ad]))}

        proj3 = proj.reshape(batch, seq, n_proj_pad)
        o_gla = _gla(proj3, _pad_rows(gla_gate_up[l], LANES).astype(BF16), row(gla_gate_bias[l]),
                     row(gla_out_norm[l]), consts["tri_rows"], cols, bb, gla_heads, dk, dv)
        o_rwkv = _rwkv(proj3, mus, row(rwkv_w0[l]), rwkv_w2[l].astype(BF16), row(rwkv_a0[l]),
                       rwkv_a2[l].astype(BF16), _pad_rows(rwkv_g2[l], 512).astype(BF16),
                       row(rwkv_k_k[l]), row(rwkv_k_a[l]), row(rwkv_r_k[l]), row(rwkv_ln_w[l]),
                       row(rwkv_ln_b[l]), consts, cols, bb, rw_w, lora)

        merged = _merge(o_gla.reshape(t, gla_vw), o_rwkv.reshape(t, rw_w),
                        w_up_gla[l].astype(BF16), w_up_rwkv[l].astype(BF16), proj,
                        cols["gate_gla"], cols["gate_rwkv"], bm, 512)
        mixed = _matmul(merged, w_out[l].astype(BF16), BF16, bm, 1024)
        h, u = _post_pre(h, mixed, row(mix_post_norm[l]), row(ffn2_pre_norm[l]), 1.0, bt)

        hid = _gate_up(u, ffn2_w_gate[l], ffn2_w_up[l], bm_gu, 256)
        f = _matmul(hid, ffn2_w_down[l].astype(BF16), BF16, bm_down, 512)
        h = _post(h, f, row(ffn2_post_norm[l]), MACARON_WEIGHT, bt)
    return h.reshape(batch, seq, d_model)
```

```python
import functools

import jax
import jax.numpy as jnp
from jax import lax
from jax.experimental import pallas as pl
from jax.experimental.pallas import tpu as pltpu

F32 = jnp.float32
BF16 = jnp.bfloat16

NORM_EPS = 1e-6
RWKV_LN_EPS = 64e-5
MACARON_WEIGHT = 0.5
GLA_GATE_NORM = 16.0
GLA_GATE_RANK = 16
LOG2_E = 1.4426950408889634
RWKV_HEAD = 64
RWKV_GATE_LORA = 480

LANES = 128
CHUNK = 64
SUB_LOG2 = 3
SUB = 1 << SUB_LOG2
GLA_HEADS_PER_STEP = 2
RWKV_GROUP = 4
RWKV_GROUPS_PER_STEP = 4
VMEM_LIMIT = 56 * 1024 * 1024
VMEM_LIMIT_MAX = 60 * 1024 * 1024


def _cparams(sem):
    return pltpu.CompilerParams(dimension_semantics=sem, vmem_limit_bytes=VMEM_LIMIT)


def _dot(a, b):
    return jnp.dot(a.astype(BF16), b.astype(BF16), preferred_element_type=F32)


def _dot_nt(a, b):
    return lax.dot_general(a.astype(BF16), b.astype(BF16), (((1,), (1,)), ((), ())),
                           preferred_element_type=F32)


def _dot_tn(a, b):
    return lax.dot_general(a.astype(BF16), b.astype(BF16), (((0,), (0,)), ((), ())),
                           preferred_element_type=F32)


def _dot_split(m01, x):
    hi = x.astype(BF16)
    lo = (x - hi.astype(F32)).astype(BF16)
    return (jnp.dot(m01, hi, preferred_element_type=F32)
            + jnp.dot(m01, lo, preferred_element_type=F32))


def _rms(x, g):
    ms = jnp.mean(x * x, axis=-1, keepdims=True)
    return x * lax.rsqrt(ms + NORM_EPS) * g


def _sigmoid(x):
    return 1.0 / (1.0 + jnp.exp(-x))


def _softplus(x):
    return jnp.maximum(x, 0.0) + jnp.log(1.0 + jnp.exp(-jnp.abs(x)))


def _norm_cast_kernel(x_ref, g_ref, o_ref):
    o_ref[...] = _rms(x_ref[...], g_ref[...]).astype(o_ref.dtype)


def _norm_cast(x, g, bt):
    t, d = x.shape
    return pl.pallas_call(
        _norm_cast_kernel,
        grid=(t // bt,),
        in_specs=[pl.BlockSpec((bt, d), lambda i: (i, 0)),
                  pl.BlockSpec((1, d), lambda i: (0, 0))],
        out_specs=pl.BlockSpec((bt, d), lambda i: (i, 0)),
        out_shape=jax.ShapeDtypeStruct((t, d), BF16),
        compiler_params=_cparams(("parallel",)),
    )(x, g)


def _post_pre_kernel(h_ref, f_ref, gpost_ref, gpre_ref, hout_ref, u_ref, *, alpha):
    h = h_ref[...] + alpha * _rms(f_ref[...].astype(F32), gpost_ref[...])
    hout_ref[...] = h
    u_ref[...] = _rms(h, gpre_ref[...]).astype(u_ref.dtype)


def _post_pre(h, f, g_post, g_pre, alpha, bt):
    t, d = h.shape
    row = pl.BlockSpec((bt, d), lambda i: (i, 0))
    vec = pl.BlockSpec((1, d), lambda i: (0, 0))
    return pl.pallas_call(
        functools.partial(_post_pre_kernel, alpha=alpha),
        grid=(t // bt,),
        in_specs=[row, row, vec, vec],
        out_specs=[row, row],
        out_shape=[jax.ShapeDtypeStruct((t, d), F32), jax.ShapeDtypeStruct((t, d), BF16)],
        compiler_params=_cparams(("parallel",)),
    )(h, f, g_post, g_pre)


def _post_kernel(h_ref, f_ref, gpost_ref, hout_ref, *, alpha):
    hout_ref[...] = h_ref[...] + alpha * _rms(f_ref[...].astype(F32), gpost_ref[...])


def _post(h, f, g_post, alpha, bt):
    t, d = h.shape
    row = pl.BlockSpec((bt, d), lambda i: (i, 0))
    vec = pl.BlockSpec((1, d), lambda i: (0, 0))
    return pl.pallas_call(
        functools.partial(_post_kernel, alpha=alpha),
        grid=(t // bt,),
        in_specs=[row, row, vec],
        out_specs=row,
        out_shape=jax.ShapeDtypeStruct((t, d), F32),
        compiler_params=_cparams(("parallel",)),
    )(h, f, g_post)


def _matmul_kernel(x_ref, w_ref, o_ref):
    o_ref[...] = jnp.dot(x_ref[...], w_ref[...], preferred_element_type=F32).astype(o_ref.dtype)


def _matmul(x, w, out_dtype, bm, bn):
    m, k = x.shape
    n = w.shape[1]
    return pl.pallas_call(
        _matmul_kernel,
        grid=(m // bm, pl.cdiv(n, bn)),
        in_specs=[pl.BlockSpec((bm, k), lambda i, j: (i, 0)),
                  pl.BlockSpec((k, bn), lambda i, j: (0, j))],
        out_specs=pl.BlockSpec((bm, bn), lambda i, j: (i, j)),
        out_shape=jax.ShapeDtypeStruct((m, n), out_dtype),
        compiler_params=_cparams(("parallel", "arbitrary")),
    )(x, w)


def _gate_up_kernel(x_ref, wg_ref, wu_ref, o_ref):
    x = x_ref[...]
    g = jnp.dot(x, wg_ref[...].astype(BF16), preferred_element_type=F32)
    u = jnp.dot(x, wu_ref[...].astype(BF16), preferred_element_type=F32)
    o_ref[...] = (g * _sigmoid(g) * u).astype(o_ref.dtype)


def _gate_up(x, wg, wu, bm, bn):
    m, k = x.shape
    n = wg.shape[1]
    wspec = pl.BlockSpec((k, bn), lambda i, j: (0, j))
    return pl.pallas_call(
        _gate_up_kernel,
        grid=(m // bm, pl.cdiv(n, bn)),
        in_specs=[pl.BlockSpec((bm, k), lambda i, j: (i, 0)), wspec, wspec],
        out_specs=pl.BlockSpec((bm, bn), lambda i, j: (i, j)),
        out_shape=jax.ShapeDtypeStruct((m, n), BF16),
        compiler_params=pltpu.CompilerParams(dimension_semantics=("parallel", "arbitrary"),
                                             vmem_limit_bytes=VMEM_LIMIT_MAX),
    )(x, wg, wu)


def _merge_kernel(og_ref, or_ref, wg_ref, wr_ref, gg_ref, gr_ref, o_ref):
    yg = jnp.dot(og_ref[...], wg_ref[...].astype(BF16), preferred_element_type=F32)
    yr = jnp.dot(or_ref[...], wr_ref[...].astype(BF16), preferred_element_type=F32)
    o_ref[...] = (_sigmoid(gg_ref[...]) * yg + _sigmoid(gr_ref[...]) * yr).astype(o_ref.dtype)


def _merge(o_gla, o_rwkv, w_up_gla, w_up_rwkv, proj, col_gate_gla, col_gate_rwkv, bm, bn):
    m, k = o_gla.shape
    n = w_up_gla.shape[1]
    xspec = pl.BlockSpec((bm, k), lambda i, j: (i, 0))
    wspec = pl.BlockSpec((k, bn), lambda i, j: (0, j))
    ga, gb = col_gate_gla // bn, col_gate_rwkv // bn
    return pl.pallas_call(
        _merge_kernel,
        grid=(m // bm, n // bn),
        in_specs=[xspec, xspec, wspec, wspec,
                  pl.BlockSpec((bm, bn), lambda i, j: (i, ga + j)),
                  pl.BlockSpec((bm, bn), lambda i, j: (i, gb + j))],
        out_specs=pl.BlockSpec((bm, bn), lambda i, j: (i, j)),
        out_shape=jax.ShapeDtypeStruct((m, n), BF16),
        compiler_params=_cparams(("parallel", "arbitrary")),
    )(o_gla, o_rwkv, w_up_gla, w_up_rwkv, proj, proj)


def _gla_kernel(q_ref, k_ref, v_ref, og_ref, gd_ref, gup_ref, gbias_ref, onorm_ref, tri_ref, sel_ref,
                o_ref, state_ref, *, dk, dv, bb, hq):
    c, nsub = CHUNK, CHUNK // SUB
    rows = bb * c

    @pl.when(pl.program_id(2) == 0)
    def _():
        state_ref[...] = jnp.zeros_like(state_ref)

    def merged(ref, lanes=slice(None)):
        return ref[:, :, lanes].reshape(rows, -1)

    def per_row(x):
        return [x[i * c:(i + 1) * c] for i in range(bb)]

    def chunk_row(x, r):
        return jnp.concatenate(
            [jnp.broadcast_to(x[i * c + r:i * c + r + 1, :], (c, x.shape[-1])) for i in range(bb)], axis=0)

    row = lax.broadcasted_iota(jnp.int32, (rows, c), 0) & (c - 1)
    lane = lax.broadcasted_iota(jnp.int32, (rows, c), 1)
    blk0 = row & -SUB
    d = lane - blk0
    dc = jnp.where(d >= 0, jnp.where(d <= (row & (SUB - 1)), d, -1), -1)
    mask_off = (lane < blk0).astype(F32)
    sub_id = (lax.broadcasted_iota(jnp.int32, (rows, dk), 0) & (c - 1)) >> SUB_LOG2
    gd = merged(gd_ref).astype(BF16)
    tri = tri_ref[...]

    qs, ks, bs, qhat, khat, b_ends = [], [], [], [], [], []
    for h in range(hq):
        lanes = slice(h * dk, (h + 1) * dk)
        x = _dot(gd, gup_ref[:, lanes]) + gbias_ref[:, lanes]
        log_a = -_softplus(-x) * (1.0 / GLA_GATE_NORM)
        b = _dot_split(tri, log_a * LOG2_E)
        q = merged(q_ref, lanes) * (dk ** -0.5)
        k = merged(k_ref, lanes)
        refs = [chunk_row(b, i * SUB - 1) for i in range(1, nsub)]
        bref = jnp.zeros((rows, dk), F32)
        for i in range(1, nsub):
            bref = jnp.where(sub_id == i, refs[i - 1], bref)
        qhat.append(per_row(q * jnp.exp2(b - bref)))
        khat.append([per_row(k * jnp.exp2(jnp.minimum(ref - b, 0.0))) for ref in refs])
        qs.append(q)
        ks.append(k)
        bs.append(b)
        b_ends.append(chunk_row(b, c - 1))

    attn = [jnp.concatenate(
        [jnp.concatenate(
            [jnp.zeros((SUB, c), F32)]
            + [_dot_nt(qhat[h][r][i * SUB:(i + 1) * SUB, :], khat[h][i - 1][r]) for i in range(1, nsub)],
            axis=0)
         for r in range(bb)], axis=0) * mask_off for h in range(hq)]

    for h in range(hq):
        q, k, b = qs[h], ks[h], bs[h]
        k3 = k.reshape(bb * nsub, SUB, dk)
        b3 = b.reshape(bb * nsub, SUB, dk)
        terms = []
        for j in range(SUB):
            kj = jnp.broadcast_to(k3[:, j:j + 1, :], (bb * nsub, SUB, dk)).reshape(rows, dk)
            bj = jnp.broadcast_to(b3[:, j:j + 1, :], (bb * nsub, SUB, dk)).reshape(rows, dk)
            terms.append((q * kj * jnp.exp2(jnp.minimum(b - bj, 0.0))).astype(BF16))
        diag = jnp.dot(jnp.concatenate(terms, axis=1), sel_ref[...], preferred_element_type=F32)
        attn[h] = jnp.where(dc >= 0, diag, attn[h])

    chains = [(h, i) for h in range(hq) for i in range(bb)]
    attn_c = [t for h in range(hq) for t in per_row(attn[h])]
    qe = [t for h in range(hq) for t in per_row(qs[h] * jnp.exp2(bs[h]))]
    k_end = [t for h in range(hq) for t in per_row(ks[h] * jnp.exp2(b_ends[h] - bs[h]))]
    v_c = [v_ref[i, :, h * dv:(h + 1) * dv] for h, i in chains]
    st = [state_ref[i, h] for h, i in chains]
    o = [_dot(a_, v_) + _dot_nt(q_, s_) for a_, v_, q_, s_ in zip(attn_c, v_c, qe, st)]
    for j, (h, i) in enumerate(chains):
        state_ref[i, h] = st[j] * jnp.exp2(b_ends[h][i * c:i * c + 1, :]) + _dot_tn(v_c[j], k_end[j])

    for h in range(hq):
        lanes = slice(h * dv, (h + 1) * dv)
        oh = _rms(jnp.concatenate(o[h * bb:(h + 1) * bb], axis=0), onorm_ref[...])
        og = merged(og_ref, lanes)
        o_ref[:, :, lanes] = (oh * (og * _sigmoid(og))).astype(o_ref.dtype).reshape(bb, c, dv)


def _gla(proj, gate_up, gate_bias, out_norm, tri, cols, bb, heads, dk, dv):
    batch, seq, _ = proj.shape
    nc = seq // CHUNK
    sel = (jnp.repeat(jnp.arange(SUB), dk)[:, None] == (jnp.arange(CHUNK) % SUB)[None, :]).astype(BF16)
    hq = next(m for m in (GLA_HEADS_PER_STEP, 1) if heads % m == 0)
    kw, vw = hq * dk, hq * dv
    cq, ck, cv, cog, cgd = (cols[n] for n in ("gla_q", "gla_k", "gla_v", "gla_og", "gla_gd"))
    return pl.pallas_call(
        functools.partial(_gla_kernel, dk=dk, dv=dv, bb=bb, hq=hq),
        grid=(batch // bb, heads // hq, nc),
        in_specs=[
            pl.BlockSpec((bb, CHUNK, kw), lambda b, h, c: (b, c, cq // kw + h)),
            pl.BlockSpec((bb, CHUNK, kw), lambda b, h, c: (b, c, ck // kw + h)),
            pl.BlockSpec((bb, CHUNK, vw), lambda b, h, c: (b, c, cv // vw + h)),
            pl.BlockSpec((bb, CHUNK, vw), lambda b, h, c: (b, c, cog // vw + h)),
            pl.BlockSpec((bb, CHUNK, LANES), lambda b, h, c: (b, c, cgd // LANES)),
            pl.BlockSpec((LANES, kw), lambda b, h, c: (0, h)),
            pl.BlockSpec((1, kw), lambda b, h, c: (0, h)),
            pl.BlockSpec((1, dv), lambda b, h, c: (0, 0)),
            pl.BlockSpec((bb * CHUNK, bb * CHUNK), lambda b, h, c: (0, 0)),
            pl.BlockSpec((SUB * dk, CHUNK), lambda b, h, c: (0, 0)),
        ],
        out_specs=pl.BlockSpec((bb, CHUNK, vw), lambda b, h, c: (b, c, h)),
        out_shape=jax.ShapeDtypeStruct((batch, seq, heads * dv), BF16),
        scratch_shapes=[pltpu.VMEM((bb, hq, dv, dk), F32)],
        compiler_params=_cparams(("parallel", "parallel", "arbitrary")),
    )(proj, proj, proj, proj, proj, gate_up, gate_bias, out_norm, tri, sel)


def _rwkv_kernel(r_ref, k_ref, v_ref, gd_ref, wa_ref, rp_ref, kp_ref, vp_ref, gdp_ref, wap_ref,
                 mur_ref, muk_ref, muv_ref, mugd_ref, muwa_ref,
                 w0_ref, w2_ref, a0_ref, a2_ref, g2_ref, kk_ref, ka_ref, rk_ref, lnw_ref, lnb_ref,
                 tri_ref, ones_ref, mstrict_ref, mincl_ref, mblk_ref,
                 o_ref, state_ref, *, lora, bb, gq):
    c, g = CHUNK, RWKV_GROUP
    n, rows, gw = g * c, bb * c, g * RWKV_HEAD
    first = pl.program_id(2) == 0

    @pl.when(first)
    def _():
        state_ref[...] = jnp.zeros_like(state_ref)

    keep_prev = jnp.where(first, 0.0, 1.0)

    def per_row(x):
        return [x[i * c:(i + 1) * c] for i in range(bb)]

    def shifted(cur_ref, prev_ref, mu_ref, lanes):
        p = cur_ref[:, :, lanes].reshape(rows, -1)
        width = p.shape[-1]
        prev = jnp.concatenate(
            [jnp.broadcast_to(prev_ref[i, 7:8, lanes] * keep_prev, (c, width)) for i in range(bb)], axis=0)
        is_row0 = (lax.broadcasted_iota(jnp.int32, p.shape, 0) & (c - 1)) == 0
        p_prev = jnp.where(is_row0, prev, pltpu.roll(p, 1, axis=0))
        return p + (p_prev - p) * mu_ref[:, lanes]

    everything = slice(None)
    wa = shifted(wa_ref, wap_ref, muwa_ref, everything)
    tanh_wd = jnp.tanh(wa[:, :lora]).astype(BF16)
    a_down = wa[:, lora:].astype(BF16)
    sig_gd = _sigmoid(shifted(gd_ref, gdp_ref, mugd_ref, everything)).astype(BF16)

    ones_bd = ones_ref[...]
    m_blk = mblk_ref[...]
    m_blk_bf16 = m_blk.astype(BF16)
    tri = tri_ref[...]

    r_all, k2_all, v_all, gate_all = [], [], [], []
    rt, kkt, km, bm, k_end, b_end, v_rows, c_end = [], [], [], [], [], [], [], []
    for q in range(gq):
        lanes = slice(q * gw, (q + 1) * gw)
        r = shifted(r_ref, rp_ref, mur_ref, lanes)
        k = shifted(k_ref, kp_ref, muk_ref, lanes)
        v = shifted(v_ref, vp_ref, muv_ref, lanes)
        w = -_softplus(-(w0_ref[:, lanes] + _dot(tanh_wd, w2_ref[:, lanes]))) - 0.5
        ld = -jnp.exp(w)
        a = _sigmoid(a0_ref[:, lanes] + _dot(a_down, a2_ref[:, lanes]))
        gate_all.append(_dot(sig_gd, g2_ref[:, lanes]))

        kkr = k * kk_ref[:, lanes]
        kk = kkr * lax.rsqrt(jnp.maximum(_dot(kkr * kkr, ones_bd), 1e-24))
        k2 = k * (1.0 + (a - 1.0) * ka_ref[:, lanes])
        bw = a * kk

        cum = _dot_split(tri, ld)
        ends = [cum[(i + 1) * c - 1:(i + 1) * c, :] for i in range(bb)]
        c_end_rows = jnp.concatenate([jnp.broadcast_to(e, (c, gw)) for e in ends], axis=0)
        e_neg = jnp.exp(-cum)
        e_end = jnp.exp(c_end_rows - cum)
        c_end += ends
        rt += per_row(r * jnp.exp(cum))
        kkt += per_row(kk * jnp.exp(cum - ld))
        km += per_row(k2 * e_neg)
        bm += per_row(bw * e_neg)
        k_end += per_row(k2 * e_end)
        b_end += per_row(bw * e_end)
        v_rows += per_row(v)
        r_all.append(r)
        k2_all.append(k2)
        v_all.append(v)

    def stack(x):
        return jnp.concatenate([x.astype(BF16)] * g, axis=0) * m_blk_bf16

    m_strict, m_incl = mstrict_ref[...], mincl_ref[...]
    lhs = [jnp.concatenate([kq, rq], axis=0).astype(BF16) for kq, rq in zip(kkt, rt)]
    aa = [_dot_nt(l, jnp.concatenate([stack(b_), stack(k_)], axis=0))
          for l, b_, k_ in zip(lhs, bm, km)]
    a_ab = [t[:c, :n] * m_strict for t in aa]
    a_kr = [jnp.concatenate([t[:c, n:] * m_strict, t[c:, n:] * m_incl], axis=0) for t in aa]
    a_rb = [t[c:, :n] * m_incl for t in aa]

    eye = m_incl - m_strict
    x = [eye - t for t in a_ab]
    p = [_dot(t, stack(t)) for t in a_ab]
    span = 2
    while span < c:
        span *= 2
        if span < c:
            xp = [_dot(jnp.concatenate([xi, pi], axis=0), stack(pi)) for xi, pi in zip(x, p)]
            x = [xi + t[:c] for xi, t in zip(x, xp)]
            p = [t[c:] for t in xp]
        else:
            x = [xi + _dot(xi, stack(pi)) for xi, pi in zip(x, p)]

    chains = [(q, i) for q in range(gq) for i in range(bb)]
    st = [state_ref[i, q] for q, i in chains]
    sk = [_dot_nt(l, s) for l, s in zip(lhs, st)]
    av = [_dot(t, stack(vi)) for t, vi in zip(a_kr, v_rows)]
    u = [_dot(xi, stack(-(s[:c] + t[:c]))) for xi, s, t in zip(x, sk, av)]
    y = [s[c:] + _dot(t, stack(ui)) + w_[c:] for s, t, ui, w_ in zip(sk, a_rb, u, av)]
    for j, (q, i) in enumerate(chains):
        state_ref[i, q] = st[j] * jnp.exp(c_end[j]) + (
            _dot_tn(u[j], b_end[j]) + _dot_tn(v_rows[j], k_end[j])) * m_blk

    inv_n = 1.0 / RWKV_HEAD
    for q in range(gq):
        lanes = slice(q * gw, (q + 1) * gw)
        yq = jnp.concatenate(y[q * bb:(q + 1) * bb], axis=0)
        mu = _dot(yq, ones_bd) * inv_n
        dy = yq - mu
        var = _dot(dy * dy, ones_bd) * inv_n
        yn = dy * lax.rsqrt(var + RWKV_LN_EPS) * lnw_ref[:, lanes] + lnb_ref[:, lanes]
        bonus = _dot(r_all[q] * k2_all[q] * rk_ref[:, lanes], ones_bd) * v_all[q]
        o_ref[:, :, lanes] = ((yn + bonus) * gate_all[q]).astype(o_ref.dtype).reshape(bb, c, gw)


def _rwkv(proj, mus, w0, w2, a0, a2, g2, k_k, k_a, r_k, ln_w, ln_b, consts, cols, bb, width, lora):
    batch, seq, _ = proj.shape
    nc = seq // CHUNK
    gw = RWKV_GROUP * RWKV_HEAD
    gq = next(m for m in (RWKV_GROUPS_PER_STEP, 1) if (width // gw) % m == 0)
    sw = gq * gw
    steps = width // sw
    gdw = g2.shape[0]
    waw = 2 * lora

    def prev_rows(c):
        return jnp.maximum(c * (CHUNK // 8) - 1, 0)

    def cur(w_, col, per_group):
        if per_group:
            return pl.BlockSpec((bb, CHUNK, w_), lambda b, q, c: (b, c, col // w_ + q))
        return pl.BlockSpec((bb, CHUNK, w_), lambda b, q, c: (b, c, col // w_))

    def prev(w_, col, per_group):
        if per_group:
            return pl.BlockSpec((bb, 8, w_), lambda b, q, c: (b, prev_rows(c), col // w_ + q))
        return pl.BlockSpec((bb, 8, w_), lambda b, q, c: (b, prev_rows(c), col // w_))

    def gvec():
        return pl.BlockSpec((1, sw), lambda b, q, c: (0, q))

    def full(shape):
        return pl.BlockSpec(shape, lambda b, q, c: (0, 0))

    n = RWKV_GROUP * CHUNK
    in_specs = [
        cur(sw, cols["rw_r"], True), cur(sw, cols["rw_k"], True), cur(sw, cols["rw_v"], True),
        cur(gdw, cols["rw_gd"], False), cur(waw, cols["rw_wa"], False),
        prev(sw, cols["rw_r"], True), prev(sw, cols["rw_k"], True), prev(sw, cols["rw_v"], True),
        prev(gdw, cols["rw_gd"], False), prev(waw, cols["rw_wa"], False),
        gvec(), gvec(), gvec(), full((1, gdw)), full((1, waw)),
        gvec(), pl.BlockSpec((lora, sw), lambda b, q, c: (0, q)),
        gvec(), pl.BlockSpec((lora, sw), lambda b, q, c: (0, q)),
        pl.BlockSpec((gdw, sw), lambda b, q, c: (0, q)),
        gvec(), gvec(), gvec(), gvec(), gvec(),
        full((bb * CHUNK, bb * CHUNK)), full((gw, gw)), full((CHUNK, n)), full((CHUNK, n)), full((n, gw)),
    ]
    return pl.pallas_call(
        functools.partial(_rwkv_kernel, lora=lora, bb=bb, gq=gq),
        grid=(batch // bb, steps, nc),
        in_specs=in_specs,
        out_specs=pl.BlockSpec((bb, CHUNK, sw), lambda b, q, c: (b, c, q)),
        out_shape=jax.ShapeDtypeStruct((batch, seq, width), BF16),
        scratch_shapes=[pltpu.VMEM((bb, gq, gw, gw), F32)],
        compiler_params=_cparams(("parallel", "parallel", "arbitrary")),
    )(proj, proj, proj, proj, proj, proj, proj, proj, proj, proj,
      mus["r"], mus["k"], mus["v"], mus["gd"], mus["wa"],
      w0, w2, a0, a2, g2, k_k, k_a, r_k, ln_w, ln_b,
      consts["tri_rows"], consts["ones_bd"], consts["m_strict"], consts["m_incl"], consts["m_blk"])


def _pad_cols(w, n):
    return jnp.pad(w, ((0, 0), (0, n - w.shape[1])))


def _pad_rows(w, n):
    return jnp.pad(w, ((0, n - w.shape[0]), (0, 0)))


def _layout(d_model):
    kw, vw, rw = d_model // 4, d_model // 2, d_model // 2
    lora = 128
    gdw = 512
    names = [("gla_q", kw), ("gla_k", kw), ("gla_v", vw), ("gla_og", vw),
             ("rw_r", rw), ("rw_k", rw), ("rw_v", rw),
             ("gate_gla", d_model), ("gate_rwkv", d_model),
             ("rw_gd", gdw), ("rw_wa", 2 * lora), ("gla_gd", LANES)]
    cols, off = {}, 0
    for name, w in names:
        assert off % w == 0, (name, off, w)
        cols[name] = off
        off += w
    return cols, off


def _chunk_consts(bb):
    c, g, hn = CHUNK, RWKV_GROUP, RWKV_HEAD
    assert c == hn
    n = g * c
    i = jnp.arange(n)
    ti, tj = jnp.arange(c)[:, None], i[None, :] % c
    lane = jnp.arange(g * hn)
    return {
        "tri_rows": jnp.kron(jnp.eye(bb, dtype=F32), jnp.tril(jnp.ones((c, c), F32))).astype(BF16),
        "ones_bd": ((lane[:, None] // hn) == (lane[None, :] // hn)).astype(BF16),
        "m_strict": (ti > tj).astype(F32),
        "m_incl": (ti >= tj).astype(F32),
        "m_blk": ((i[:, None] // c) == (lane[None, :] // hn)).astype(F32),
    }


def kernel(x, ffn1_pre_norm, ffn1_w_gate, ffn1_w_up, ffn1_w_down, ffn1_post_norm, mix_pre_norm, w_in, gla_gate_up, gla_gate_bias, gla_out_norm, rwkv_shift_mix, rwkv_w0, rwkv_w2, rwkv_a0, rwkv_a2, rwkv_g2, rwkv_k_k, rwkv_k_a, rwkv_r_k, rwkv_ln_w, rwkv_ln_b, w_up_gla, w_up_rwkv, w_out, mix_post_norm, ffn2_pre_norm, ffn2_w_gate, ffn2_w_up, ffn2_w_down, ffn2_post_norm):
    batch, seq, d_model = x.shape
    depth = ffn1_pre_norm.shape[0]
    t = batch * seq
    assert seq % CHUNK == 0 and d_model % 1024 == 0

    gla_kw, gla_vw, rw_w = d_model // 4, d_model // 2, d_model // 2
    gla_heads = max(4, d_model // 512)
    dk, dv = gla_kw // gla_heads, gla_vw // gla_heads
    lora = rwkv_w2.shape[1]
    g_lora = rwkv_g2.shape[1]
    assert lora == 128 and g_lora <= 512 and dk % LANES == 0

    cols, n_proj = _layout(d_model)
    n_proj_pad = -(-n_proj // 1024) * 1024
    bm = min(1024, t)
    bm_down = min(512, t)
    bm_gu = min(2048, t)
    bt = min(256, t)
    bb = next(n for n in (4, 2, 1) if batch % n == 0)
    consts = _chunk_consts(bb)

    def row(v):
        return v.reshape(1, -1).astype(F32)

    h = x.reshape(t, d_model)
    for l in range(depth):
        xn = _norm_cast(h, row(ffn1_pre_norm[l]), bt)
        hid = _gate_up(xn, ffn1_w_gate[l], ffn1_w_up[l], bm_gu, 256)
        f = _matmul(hid, ffn1_w_down[l].astype(BF16), BF16, bm_down, 512)
        h, u = _post_pre(h, f, row(ffn1_post_norm[l]), row(mix_pre_norm[l]), MACARON_WEIGHT, bt)

        wi = w_in[l]
        o_gla, o_rw, o_gate = 0, 2 * gla_kw + 2 * gla_vw + GLA_GATE_RANK, None
        gq, gk, gv, ggd, gog = jnp.split(wi[:, :o_rw], [gla_kw, 2 * gla_kw, 2 * gla_kw + gla_vw,
                                                       2 * gla_kw + gla_vw + GLA_GATE_RANK], axis=1)
        rw_sizes = [rw_w, lora, rw_w, rw_w, lora, g_lora]
        rw_off = [o_rw]
        for s in rw_sizes:
            rw_off.append(rw_off[-1] + s)
        rr, rwd, rk, rv, rad, rgd = (wi[:, rw_off[i]:rw_off[i + 1]] for i in range(6))
        gates = wi[:, rw_off[-1]:]
        w_proj = jnp.concatenate(
            [piece.astype(BF16) for piece in
             (gq, gk, gv, gog, rr, rk, rv, gates, _pad_cols(rgd, 512), rwd, rad, _pad_cols(ggd, LANES),
              jnp.zeros((d_model, n_proj_pad - n_proj), F32))], axis=1)
        proj = _matmul(u, w_proj, F32, bm, 1024)

        mu = rwkv_shift_mix[l]
        mu_off = [o - o_rw for o in rw_off]
        mu_r, mu_wd, mu_k, mu_v, mu_ad, mu_gd = (mu[mu_off[i]:mu_off[i + 1]] for i in range(6))
        mus = {"r": row(mu_r), "k": row(mu_k), "v": row(mu_v),
               "gd": row(jnp.pad(mu_gd, (0, 512 - g_lora))),
               "wa": row(jnp.concatenate([mu_wd, mu_ad]))}

        proj3 = proj.reshape(batch, seq, n_proj_pad)
        o_gla = _gla(proj3, _pad_rows(gla_gate_up[l], LANES).astype(BF16), row(gla_gate_bias[l]),
                     row(gla_out_norm[l]), consts["tri_rows"], cols, bb, gla_heads, dk, dv)
        o_rwkv = _rwkv(proj3, mus, row(rwkv_w0[l]), rwkv_w2[l].astype(BF16), row(rwkv_a0[l]),
                       rwkv_a2[l].astype(BF16), _pad_rows(rwkv_g2[l], 512).astype(BF16),
                       row(rwkv_k_k[l]), row(rwkv_k_a[l]), row(rwkv_r_k[l]), row(rwkv_ln_w[l]),
                       row(rwkv_ln_b[l]), consts, cols, bb, rw_w, lora)

        merged = _merge(o_gla.reshape(t, gla_vw), o_rwkv.reshape(t, rw_w),
                        w_up_gla[l], w_up_rwkv[l], proj,
                        cols["gate_gla"], cols["gate_rwkv"], bm, 512)
        mixed = _matmul(merged, w_out[l].astype(BF16), BF16, bm, 1024)
        h, u = _post_pre(h, mixed, row(mix_post_norm[l]), row(ffn2_pre_norm[l]), 1.0, bt)

        hid = _gate_up(u, ffn2_w_gate[l], ffn2_w_up[l], bm_gu, 256)
        f = _matmul(hid, ffn2_w_down[l].astype(BF16), BF16, bm_down, 512)
        h = _post(h, f, row(ffn2_post_norm[l]), MACARON_WEIGHT, bt)
    return h.reshape(batch, seq, d_model)
```

```python
import functools

import jax
import jax.numpy as jnp
from jax import lax
from jax.experimental import pallas as pl
from jax.experimental.pallas import tpu as pltpu

F32 = jnp.float32
BF16 = jnp.bfloat16

NORM_EPS = 1e-6
RWKV_LN_EPS = 64e-5
MACARON_WEIGHT = 0.5
GLA_GATE_NORM = 16.0
GLA_GATE_RANK = 16
LOG2_E = 1.4426950408889634
RWKV_HEAD = 64
RWKV_GATE_LORA = 480

LANES = 128
CHUNK = 64
SUB_LOG2 = 3
SUB = 1 << SUB_LOG2
GLA_HEADS_PER_STEP = 2
RWKV_GROUP = 4
RWKV_GROUPS_PER_STEP = 4
VMEM_LIMIT = 56 * 1024 * 1024
VMEM_LIMIT_MAX = 60 * 1024 * 1024


def _cparams(sem):
    return pltpu.CompilerParams(dimension_semantics=sem, vmem_limit_bytes=VMEM_LIMIT)


def _dot(a, b):
    return jnp.dot(a.astype(BF16), b.astype(BF16), preferred_element_type=F32)


def _dot_nt(a, b):
    return lax.dot_general(a.astype(BF16), b.astype(BF16), (((1,), (1,)), ((), ())),
                           preferred_element_type=F32)


def _dot_tn(a, b):
    return lax.dot_general(a.astype(BF16), b.astype(BF16), (((0,), (0,)), ((), ())),
                           preferred_element_type=F32)


def _dot_split(m01, x):
    hi = x.astype(BF16)
    lo = (x - hi.astype(F32)).astype(BF16)
    return (jnp.dot(m01, hi, preferred_element_type=F32)
            + jnp.dot(m01, lo, preferred_element_type=F32))


def _rms(x, g):
    ms = jnp.mean(x * x, axis=-1, keepdims=True)
    return x * lax.rsqrt(ms + NORM_EPS) * g


def _sigmoid(x):
    return 1.0 / (1.0 + jnp.exp(-x))


def _softplus(x):
    return jnp.maximum(x, 0.0) + jnp.log(1.0 + jnp.exp(-jnp.abs(x)))


def _norm_cast_kernel(x_ref, g_ref, o_ref):
    o_ref[...] = _rms(x_ref[...], g_ref[...]).astype(o_ref.dtype)


def _norm_cast(x, g, bt):
    t, d = x.shape
    return pl.pallas_call(
        _norm_cast_kernel,
        grid=(t // bt,),
        in_specs=[pl.BlockSpec((bt, d), lambda i: (i, 0)),
                  pl.BlockSpec((1, d), lambda i: (0, 0))],
        out_specs=pl.BlockSpec((bt, d), lambda i: (i, 0)),
        out_shape=jax.ShapeDtypeStruct((t, d), BF16),
        compiler_params=_cparams(("parallel",)),
    )(x, g)


def _post_pre_kernel(h_ref, f_ref, gpost_ref, gpre_ref, hout_ref, u_ref, *, alpha):
    h = h_ref[...] + alpha * _rms(f_ref[...].astype(F32), gpost_ref[...])
    hout_ref[...] = h
    u_ref[...] = _rms(h, gpre_ref[...]).astype(u_ref.dtype)


def _post_pre(h, f, g_post, g_pre, alpha, bt):
    t, d = h.shape
    row = pl.BlockSpec((bt, d), lambda i: (i, 0))
    vec = pl.BlockSpec((1, d), lambda i: (0, 0))
    return pl.pallas_call(
        functools.partial(_post_pre_kernel, alpha=alpha),
        grid=(t // bt,),
        in_specs=[row, row, vec, vec],
        out_specs=[row, row],
        out_shape=[jax.ShapeDtypeStruct((t, d), F32), jax.ShapeDtypeStruct((t, d), BF16)],
        compiler_params=_cparams(("parallel",)),
    )(h, f, g_post, g_pre)


def _post_kernel(h_ref, f_ref, gpost_ref, hout_ref, *, alpha):
    hout_ref[...] = h_ref[...] + alpha * _rms(f_ref[...].astype(F32), gpost_ref[...])


def _post(h, f, g_post, alpha, bt):
    t, d = h.shape
    row = pl.BlockSpec((bt, d), lambda i: (i, 0))
    vec = pl.BlockSpec((1, d), lambda i: (0, 0))
    return pl.pallas_call(
        functools.partial(_post_kernel, alpha=alpha),
        grid=(t // bt,),
        in_specs=[row, row, vec],
        out_specs=row,
        out_shape=jax.ShapeDtypeStruct((t, d), F32),
        compiler_params=_cparams(("parallel",)),
    )(h, f, g_post)


def _matmul_kernel(x_ref, w_ref, o_ref):
    o_ref[...] = jnp.dot(x_ref[...], w_ref[...], preferred_element_type=F32).astype(o_ref.dtype)


def _matmul(x, w, out_dtype, bm, bn):
    m, k = x.shape
    n = w.shape[1]
    return pl.pallas_call(
        _matmul_kernel,
        grid=(m // bm, pl.cdiv(n, bn)),
        in_specs=[pl.BlockSpec((bm, k), lambda i, j: (i, 0)),
                  pl.BlockSpec((k, bn), lambda i, j: (0, j))],
        out_specs=pl.BlockSpec((bm, bn), lambda i, j: (i, j)),
        out_shape=jax.ShapeDtypeStruct((m, n), out_dtype),
        compiler_params=_cparams(("parallel", "arbitrary")),
    )(x, w)


def _gate_up_kernel(x_ref, wg_ref, wu_ref, o_ref):
    x = x_ref[...]
    g = jnp.dot(x, wg_ref[...].astype(BF16), preferred_element_type=F32)
    u = jnp.dot(x, wu_ref[...].astype(BF16), preferred_element_type=F32)
    o_ref[...] = (g * _sigmoid(g) * u).astype(o_ref.dtype)


def _gate_up(x, wg, wu, bm, bn):
    m, k = x.shape
    n = wg.shape[1]
    wspec = pl.BlockSpec((k, bn), lambda i, j: (0, j))
    return pl.pallas_call(
        _gate_up_kernel,
        grid=(m // bm, pl.cdiv(n, bn)),
        in_specs=[pl.BlockSpec((bm, k), lambda i, j: (i, 0)), wspec, wspec],
        out_specs=pl.BlockSpec((bm, bn), lambda i, j: (i, j)),
        out_shape=jax.ShapeDtypeStruct((m, n), BF16),
        compiler_params=pltpu.CompilerParams(dimension_semantics=("parallel", "arbitrary"),
                                             vmem_limit_bytes=VMEM_LIMIT_MAX),
    )(x, wg, wu)


def _merge_kernel(og_ref, or_ref, wg_ref, wr_ref, gg_ref, gr_ref, o_ref):
    yg = jnp.dot(og_ref[...], wg_ref[...], preferred_element_type=F32)
    yr = jnp.dot(or_ref[...], wr_ref[...], preferred_element_type=F32)
    o_ref[...] = (_sigmoid(gg_ref[...]) * yg + _sigmoid(gr_ref[...]) * yr).astype(o_ref.dtype)


def _merge(o_gla, o_rwkv, w_up_gla, w_up_rwkv, proj, col_gate_gla, col_gate_rwkv, bm, bn):
    m, k = o_gla.shape
    n = w_up_gla.shape[1]
    xspec = pl.BlockSpec((bm, k), lambda i, j: (i, 0))
    wspec = pl.BlockSpec((k, bn), lambda i, j: (0, j))
    ga, gb = col_gate_gla // bn, col_gate_rwkv // bn
    return pl.pallas_call(
        _merge_kernel,
        grid=(m // bm, n // bn),
        in_specs=[xspec, xspec, wspec, wspec,
                  pl.BlockSpec((bm, bn), lambda i, j: (i, ga + j)),
                  pl.BlockSpec((bm, bn), lambda i, j: (i, gb + j))],
        out_specs=pl.BlockSpec((bm, bn), lambda i, j: (i, j)),
        out_shape=jax.ShapeDtypeStruct((m, n), BF16),
        compiler_params=_cparams(("parallel", "arbitrary")),
    )(o_gla, o_rwkv, w_up_gla, w_up_rwkv, proj, proj)


def _gla_kernel(q_ref, k_ref, v_ref, og_ref, gd_ref, gup_ref, gbias_ref, onorm_ref, tri_ref, sel_ref,
                o_ref, state_ref, *, dk, dv, bb, hq):
    c, nsub = CHUNK, CHUNK // SUB
    rows = bb * c

    @pl.when(pl.program_id(2) == 0)
    def _():
        state_ref[...] = jnp.zeros_like(state_ref)

    def merged(ref, lanes=slice(None)):
        return ref[:, :, lanes].reshape(rows, -1)

    def per_row(x):
        return [x[i * c:(i + 1) * c] for i in range(bb)]

    def chunk_row(x, r):
        return jnp.concatenate(
            [jnp.broadcast_to(x[i * c + r:i * c + r + 1, :], (c, x.shape[-1])) for i in range(bb)], axis=0)

    row = lax.broadcasted_iota(jnp.int32, (rows, c), 0) & (c - 1)
    lane = lax.broadcasted_iota(jnp.int32, (rows, c), 1)
    blk0 = row & -SUB
    d = lane - blk0
    dc = jnp.where(d >= 0, jnp.where(d <= (row & (SUB - 1)), d, -1), -1)
    mask_off = (lane < blk0).astype(F32)
    sub_id = (lax.broadcasted_iota(jnp.int32, (rows, dk), 0) & (c - 1)) >> SUB_LOG2
    gd = merged(gd_ref).astype(BF16)
    tri = tri_ref[...]

    qs, ks, bs, qhat, khat, b_ends = [], [], [], [], [], []
    for h in range(hq):
        lanes = slice(h * dk, (h + 1) * dk)
        x = _dot(gd, gup_ref[:, lanes]) + gbias_ref[:, lanes]
        log_a = -_softplus(-x) * (1.0 / GLA_GATE_NORM)
        b = _dot_split(tri, log_a * LOG2_E)
        q = merged(q_ref, lanes) * (dk ** -0.5)
        k = merged(k_ref, lanes)
        refs = [chunk_row(b, i * SUB - 1) for i in range(1, nsub)]
        bref = jnp.zeros((rows, dk), F32)
        for i in range(1, nsub):
            bref = jnp.where(sub_id == i, refs[i - 1], bref)
        qhat.append(per_row(q * jnp.exp2(b - bref)))
        khat.append([per_row(k * jnp.exp2(jnp.minimum(ref - b, 0.0))) for ref in refs])
        qs.append(q)
        ks.append(k)
        bs.append(b)
        b_ends.append(chunk_row(b, c - 1))

    attn = [jnp.concatenate(
        [jnp.concatenate(
            [jnp.zeros((SUB, c), F32)]
            + [_dot_nt(qhat[h][r][i * SUB:(i + 1) * SUB, :], khat[h][i - 1][r]) for i in range(1, nsub)],
            axis=0)
         for r in range(bb)], axis=0) * mask_off for h in range(hq)]

    for h in range(hq):
        q, k, b = qs[h], ks[h], bs[h]
        k3 = k.reshape(bb * nsub, SUB, dk)
        b3 = b.reshape(bb * nsub, SUB, dk)
        terms = []
        for j in range(SUB):
            kj = jnp.broadcast_to(k3[:, j:j + 1, :], (bb * nsub, SUB, dk)).reshape(rows, dk)
            bj = jnp.broadcast_to(b3[:, j:j + 1, :], (bb * nsub, SUB, dk)).reshape(rows, dk)
            terms.append((q * kj * jnp.exp2(jnp.minimum(b - bj, 0.0))).astype(BF16))
        diag = jnp.dot(jnp.concatenate(terms, axis=1), sel_ref[...], preferred_element_type=F32)
        attn[h] = jnp.where(dc >= 0, diag, attn[h])

    chains = [(h, i) for h in range(hq) for i in range(bb)]
    attn_c = [t for h in range(hq) for t in per_row(attn[h])]
    qe = [t for h in range(hq) for t in per_row(qs[h] * jnp.exp2(bs[h]))]
    k_end = [t for h in range(hq) for t in per_row(ks[h] * jnp.exp2(b_ends[h] - bs[h]))]
    v_c = [v_ref[i, :, h * dv:(h + 1) * dv] for h, i in chains]
    st = [state_ref[i, h] for h, i in chains]
    o = [_dot(a_, v_) + _dot_nt(q_, s_) for a_, v_, q_, s_ in zip(attn_c, v_c, qe, st)]
    for j, (h, i) in enumerate(chains):
        state_ref[i, h] = st[j] * jnp.exp2(b_ends[h][i * c:i * c + 1, :]) + _dot_tn(v_c[j], k_end[j])

    for h in range(hq):
        lanes = slice(h * dv, (h + 1) * dv)
        oh = _rms(jnp.concatenate(o[h * bb:(h + 1) * bb], axis=0), onorm_ref[...])
        og = merged(og_ref, lanes)
        o_ref[:, :, lanes] = (oh * (og * _sigmoid(og))).astype(o_ref.dtype).reshape(bb, c, dv)


def _gla(proj, gate_up, gate_bias, out_norm, tri, cols, bb, heads, dk, dv):
    batch, seq, _ = proj.shape
    nc = seq // CHUNK
    sel = (jnp.repeat(jnp.arange(SUB), dk)[:, None] == (jnp.arange(CHUNK) % SUB)[None, :]).astype(BF16)
    hq = next(m for m in (GLA_HEADS_PER_STEP, 1) if heads % m == 0)
    kw, vw = hq * dk, hq * dv
    cq, ck, cv, cog, cgd = (cols[n] for n in ("gla_q", "gla_k", "gla_v", "gla_og", "gla_gd"))
    return pl.pallas_call(
        functools.partial(_gla_kernel, dk=dk, dv=dv, bb=bb, hq=hq),
        grid=(batch // bb, heads // hq, nc),
        in_specs=[
            pl.BlockSpec((bb, CHUNK, kw), lambda b, h, c: (b, c, cq // kw + h)),
            pl.BlockSpec((bb, CHUNK, kw), lambda b, h, c: (b, c, ck // kw + h)),
            pl.BlockSpec((bb, CHUNK, vw), lambda b, h, c: (b, c, cv // vw + h)),
            pl.BlockSpec((bb, CHUNK, vw), lambda b, h, c: (b, c, cog // vw + h)),
            pl.BlockSpec((bb, CHUNK, LANES), lambda b, h, c: (b, c, cgd // LANES)),
            pl.BlockSpec((LANES, kw), lambda b, h, c: (0, h)),
            pl.BlockSpec((1, kw), lambda b, h, c: (0, h)),
            pl.BlockSpec((1, dv), lambda b, h, c: (0, 0)),
            pl.BlockSpec((bb * CHUNK, bb * CHUNK), lambda b, h, c: (0, 0)),
            pl.BlockSpec((SUB * dk, CHUNK), lambda b, h, c: (0, 0)),
        ],
        out_specs=pl.BlockSpec((bb, CHUNK, vw), lambda b, h, c: (b, c, h)),
        out_shape=jax.ShapeDtypeStruct((batch, seq, heads * dv), BF16),
        scratch_shapes=[pltpu.VMEM((bb, hq, dv, dk), F32)],
        compiler_params=_cparams(("parallel", "parallel", "arbitrary")),
    )(proj, proj, proj, proj, proj, gate_up, gate_bias, out_norm, tri, sel)


def _rwkv_kernel(r_ref, k_ref, v_ref, gd_ref, wa_ref, rp_ref, kp_ref, vp_ref, gdp_ref, wap_ref,
                 mur_ref, muk_ref, muv_ref, mugd_ref, muwa_ref,
                 w0_ref, w2_ref, a0_ref, a2_ref, g2_ref, kk_ref, ka_ref, rk_ref, lnw_ref, lnb_ref,
                 tri_ref, ones_ref, mstrict_ref, mincl_ref, mblk_ref,
                 o_ref, state_ref, *, lora, bb, gq):
    c, g = CHUNK, RWKV_GROUP
    n, rows, gw = g * c, bb * c, g * RWKV_HEAD
    first = pl.program_id(2) == 0

    @pl.when(first)
    def _():
        state_ref[...] = jnp.zeros_like(state_ref)

    keep_prev = jnp.where(first, 0.0, 1.0)

    def per_row(x):
        return [x[i * c:(i + 1) * c] for i in range(bb)]

    def shifted(cur_ref, prev_ref, mu_ref, lanes):
        p = cur_ref[:, :, lanes].reshape(rows, -1)
        width = p.shape[-1]
        prev = jnp.concatenate(
            [jnp.broadcast_to(prev_ref[i, 7:8, lanes] * keep_prev, (c, width)) for i in range(bb)], axis=0)
        is_row0 = (lax.broadcasted_iota(jnp.int32, p.shape, 0) & (c - 1)) == 0
        p_prev = jnp.where(is_row0, prev, pltpu.roll(p, 1, axis=0))
        return p + (p_prev - p) * mu_ref[:, lanes]

    everything = slice(None)
    wa = shifted(wa_ref, wap_ref, muwa_ref, everything)
    tanh_wd = jnp.tanh(wa[:, :lora]).astype(BF16)
    a_down = wa[:, lora:].astype(BF16)
    sig_gd = _sigmoid(shifted(gd_ref, gdp_ref, mugd_ref, everything)).astype(BF16)

    ones_bd = ones_ref[...]
    m_blk = mblk_ref[...]
    m_blk_bf16 = m_blk.astype(BF16)
    tri = tri_ref[...]

    r_all, k2_all, v_all, gate_all = [], [], [], []
    rt, kkt, km, bm, k_end, b_end, v_rows, c_end = [], [], [], [], [], [], [], []
    for q in range(gq):
        lanes = slice(q * gw, (q + 1) * gw)
        r = shifted(r_ref, rp_ref, mur_ref, lanes)
        k = shifted(k_ref, kp_ref, muk_ref, lanes)
        v = shifted(v_ref, vp_ref, muv_ref, lanes)
        w = -_softplus(-(w0_ref[:, lanes] + _dot(tanh_wd, w2_ref[:, lanes]))) - 0.5
        ld = -jnp.exp(w)
        a = _sigmoid(a0_ref[:, lanes] + _dot(a_down, a2_ref[:, lanes]))
        gate_all.append(_dot(sig_gd, g2_ref[:, lanes]))

        kkr = k * kk_ref[:, lanes]
        kk = kkr * lax.rsqrt(jnp.maximum(_dot(kkr * kkr, ones_bd), 1e-24))
        k2 = k * (1.0 + (a - 1.0) * ka_ref[:, lanes])
        bw = a * kk

        cum = _dot_split(tri, ld)
        ends = [cum[(i + 1) * c - 1:(i + 1) * c, :] for i in range(bb)]
        c_end_rows = jnp.concatenate([jnp.broadcast_to(e, (c, gw)) for e in ends], axis=0)
        e_neg = jnp.exp(-cum)
        e_end = jnp.exp(c_end_rows - cum)
        c_end += ends
        rt += per_row(r * jnp.exp(cum))
        kkt += per_row(kk * jnp.exp(cum - ld))
        km += per_row(k2 * e_neg)
        bm += per_row(bw * e_neg)
        k_end += per_row(k2 * e_end)
        b_end += per_row(bw * e_end)
        v_rows += per_row(v)
        r_all.append(r)
        k2_all.append(k2)
        v_all.append(v)

    def stack(x):
        return jnp.concatenate([x.astype(BF16)] * g, axis=0) * m_blk_bf16

    m_strict, m_incl = mstrict_ref[...], mincl_ref[...]
    lhs = [jnp.concatenate([kq, rq], axis=0).astype(BF16) for kq, rq in zip(kkt, rt)]
    aa = [_dot_nt(l, jnp.concatenate([stack(b_), stack(k_)], axis=0))
          for l, b_, k_ in zip(lhs, bm, km)]
    a_ab = [t[:c, :n] * m_strict for t in aa]
    a_kr = [jnp.concatenate([t[:c, n:] * m_strict, t[c:, n:] * m_incl], axis=0) for t in aa]
    a_rb = [t[c:, :n] * m_incl for t in aa]

    eye = m_incl - m_strict
    x = [eye - t for t in a_ab]
    p = [_dot(t, stack(t)) for t in a_ab]
    span = 2
    while span < c:
        span *= 2
        if span < c:
            xp = [_dot(jnp.concatenate([xi, pi], axis=0), stack(pi)) for xi, pi in zip(x, p)]
            x = [xi + t[:c] for xi, t in zip(x, xp)]
            p = [t[c:] for t in xp]
        else:
            x = [xi + _dot(xi, stack(pi)) for xi, pi in zip(x, p)]

    chains = [(q, i) for q in range(gq) for i in range(bb)]
    st = [state_ref[i, q] for q, i in chains]
    sk = [_dot_nt(l, s) for l, s in zip(lhs, st)]
    av = [_dot(t, stack(vi)) for t, vi in zip(a_kr, v_rows)]
    u = [_dot(xi, stack(-(s[:c] + t[:c]))) for xi, s, t in zip(x, sk, av)]
    y = [s[c:] + _dot(t, stack(ui)) + w_[c:] for s, t, ui, w_ in zip(sk, a_rb, u, av)]
    for j, (q, i) in enumerate(chains):
        state_ref[i, q] = st[j] * jnp.exp(c_end[j]) + _dot_tn(
            jnp.concatenate([u[j], v_rows[j]], axis=0),
            jnp.concatenate([b_end[j], k_end[j]], axis=0)) * m_blk

    inv_n = 1.0 / RWKV_HEAD
    for q in range(gq):
        lanes = slice(q * gw, (q + 1) * gw)
        yq = jnp.concatenate(y[q * bb:(q + 1) * bb], axis=0)
        mu = _dot(yq, ones_bd) * inv_n
        dy = yq - mu
        var = _dot(dy * dy, ones_bd) * inv_n
        yn = dy * lax.rsqrt(var + RWKV_LN_EPS) * lnw_ref[:, lanes] + lnb_ref[:, lanes]
        bonus = _dot(r_all[q] * k2_all[q] * rk_ref[:, lanes], ones_bd) * v_all[q]
        o_ref[:, :, lanes] = ((yn + bonus) * gate_all[q]).astype(o_ref.dtype).reshape(bb, c, gw)


def _rwkv(proj, mus, w0, w2, a0, a2, g2, k_k, k_a, r_k, ln_w, ln_b, consts, cols, bb, width, lora):
    batch, seq, _ = proj.shape
    nc = seq // CHUNK
    gw = RWKV_GROUP * RWKV_HEAD
    gq = next(m for m in (RWKV_GROUPS_PER_STEP, 1) if (width // gw) % m == 0)
    sw = gq * gw
    steps = width // sw
    gdw = g2.shape[0]
    waw = 2 * lora

    def prev_rows(c):
        return jnp.maximum(c * (CHUNK // 8) - 1, 0)

    def cur(w_, col, per_group):
        if per_group:
            return pl.BlockSpec((bb, CHUNK, w_), lambda b, q, c: (b, c, col // w_ + q))
        return pl.BlockSpec((bb, CHUNK, w_), lambda b, q, c: (b, c, col // w_))

    def prev(w_, col, per_group):
        if per_group:
            return pl.BlockSpec((bb, 8, w_), lambda b, q, c: (b, prev_rows(c), col // w_ + q))
        return pl.BlockSpec((bb, 8, w_), lambda b, q, c: (b, prev_rows(c), col // w_))

    def gvec():
        return pl.BlockSpec((1, sw), lambda b, q, c: (0, q))

    def full(shape):
        return pl.BlockSpec(shape, lambda b, q, c: (0, 0))

    n = RWKV_GROUP * CHUNK
    in_specs = [
        cur(sw, cols["rw_r"], True), cur(sw, cols["rw_k"], True), cur(sw, cols["rw_v"], True),
        cur(gdw, cols["rw_gd"], False), cur(waw, cols["rw_wa"], False),
        prev(sw, cols["rw_r"], True), prev(sw, cols["rw_k"], True), prev(sw, cols["rw_v"], True),
        prev(gdw, cols["rw_gd"], False), prev(waw, cols["rw_wa"], False),
        gvec(), gvec(), gvec(), full((1, gdw)), full((1, waw)),
        gvec(), pl.BlockSpec((lora, sw), lambda b, q, c: (0, q)),
        gvec(), pl.BlockSpec((lora, sw), lambda b, q, c: (0, q)),
        pl.BlockSpec((gdw, sw), lambda b, q, c: (0, q)),
        gvec(), gvec(), gvec(), gvec(), gvec(),
        full((bb * CHUNK, bb * CHUNK)), full((gw, gw)), full((CHUNK, n)), full((CHUNK, n)), full((n, gw)),
    ]
    return pl.pallas_call(
        functools.partial(_rwkv_kernel, lora=lora, bb=bb, gq=gq),
        grid=(batch // bb, steps, nc),
        in_specs=in_specs,
        out_specs=pl.BlockSpec((bb, CHUNK, sw), lambda b, q, c: (b, c, q)),
        out_shape=jax.ShapeDtypeStruct((batch, seq, width), BF16),
        scratch_shapes=[pltpu.VMEM((bb, gq, gw, gw), F32)],
        compiler_params=_cparams(("parallel", "parallel", "arbitrary")),
    )(proj, proj, proj, proj, proj, proj, proj, proj, proj, proj,
      mus["r"], mus["k"], mus["v"], mus["gd"], mus["wa"],
      w0, w2, a0, a2, g2, k_k, k_a, r_k, ln_w, ln_b,
      consts["tri_rows"], consts["ones_bd"], consts["m_strict"], consts["m_incl"], consts["m_blk"])


def _pad_cols(w, n):
    return jnp.pad(w, ((0, 0), (0, n - w.shape[1])))


def _pad_rows(w, n):
    return jnp.pad(w, ((0, n - w.shape[0]), (0, 0)))


def _layout(d_model):
    kw, vw, rw = d_model // 4, d_model // 2, d_model // 2
    lora = 128
    gdw = 512
    names = [("gla_q", kw), ("gla_k", kw), ("gla_v", vw), ("gla_og", vw),
             ("rw_r", rw), ("rw_k", rw), ("rw_v", rw),
             ("gate_gla", d_model), ("gate_rwkv", d_model),
             ("rw_gd", gdw), ("rw_wa", 2 * lora), ("gla_gd", LANES)]
    cols, off = {}, 0
    for name, w in names:
        assert off % w == 0, (name, off, w)
        cols[name] = off
        off += w
    return cols, off


def _chunk_consts(bb):
    c, g, hn = CHUNK, RWKV_GROUP, RWKV_HEAD
    assert c == hn
    n = g * c
    i = jnp.arange(n)
    ti, tj = jnp.arange(c)[:, None], i[None, :] % c
    lane = jnp.arange(g * hn)
    return {
        "tri_rows": jnp.kron(jnp.eye(bb, dtype=F32), jnp.tril(jnp.ones((c, c), F32))).astype(BF16),
        "ones_bd": ((lane[:, None] // hn) == (lane[None, :] // hn)).astype(BF16),
        "m_strict": (ti > tj).astype(F32),
        "m_incl": (ti >= tj).astype(F32),
        "m_blk": ((i[:, None] // c) == (lane[None, :] // hn)).astype(F32),
    }


def kernel(x, ffn1_pre_norm, ffn1_w_gate, ffn1_w_up, ffn1_w_down, ffn1_post_norm, mix_pre_norm, w_in, gla_gate_up, gla_gate_bias, gla_out_norm, rwkv_shift_mix, rwkv_w0, rwkv_w2, rwkv_a0, rwkv_a2, rwkv_g2, rwkv_k_k, rwkv_k_a, rwkv_r_k, rwkv_ln_w, rwkv_ln_b, w_up_gla, w_up_rwkv, w_out, mix_post_norm, ffn2_pre_norm, ffn2_w_gate, ffn2_w_up, ffn2_w_down, ffn2_post_norm):
    batch, seq, d_model = x.shape
    depth = ffn1_pre_norm.shape[0]
    t = batch * seq
    assert seq % CHUNK == 0 and d_model % 1024 == 0

    gla_kw, gla_vw, rw_w = d_model // 4, d_model // 2, d_model // 2
    gla_heads = max(4, d_model // 512)
    dk, dv = gla_kw // gla_heads, gla_vw // gla_heads
    lora = rwkv_w2.shape[1]
    g_lora = rwkv_g2.shape[1]
    assert lora == 128 and g_lora <= 512 and dk % LANES == 0

    cols, n_proj = _layout(d_model)
    n_proj_pad = -(-n_proj // 1024) * 1024
    bm = min(1024, t)
    bm_down = min(512, t)
    bm_gu = min(2048, t)
    bt = min(256, t)
    bb = next(n for n in (4, 2, 1) if batch % n == 0)
    consts = _chunk_consts(bb)

    def row(v):
        return v.reshape(1, -1).astype(F32)

    h = x.reshape(t, d_model)
    for l in range(depth):
        xn = _norm_cast(h, row(ffn1_pre_norm[l]), bt)
        hid = _gate_up(xn, ffn1_w_gate[l], ffn1_w_up[l], bm_gu, 256)
        f = _matmul(hid, ffn1_w_down[l].astype(BF16), BF16, bm_down, 512)
        h, u = _post_pre(h, f, row(ffn1_post_norm[l]), row(mix_pre_norm[l]), MACARON_WEIGHT, bt)

        wi = w_in[l]
        o_gla, o_rw, o_gate = 0, 2 * gla_kw + 2 * gla_vw + GLA_GATE_RANK, None
        gq, gk, gv, ggd, gog = jnp.split(wi[:, :o_rw], [gla_kw, 2 * gla_kw, 2 * gla_kw + gla_vw,
                                                       2 * gla_kw + gla_vw + GLA_GATE_RANK], axis=1)
        rw_sizes = [rw_w, lora, rw_w, rw_w, lora, g_lora]
        rw_off = [o_rw]
        for s in rw_sizes:
            rw_off.append(rw_off[-1] + s)
        rr, rwd, rk, rv, rad, rgd = (wi[:, rw_off[i]:rw_off[i + 1]] for i in range(6))
        gates = wi[:, rw_off[-1]:]
        w_proj = jnp.concatenate(
            [piece.astype(BF16) for piece in
             (gq, gk, gv, gog, rr, rk, rv, gates, _pad_cols(rgd, 512), rwd, rad, _pad_cols(ggd, LANES),
              jnp.zeros((d_model, n_proj_pad - n_proj), F32))], axis=1)
        proj = _matmul(u, w_proj, F32, bm, 1024)

        mu = rwkv_shift_mix[l]
        mu_off = [o - o_rw for o in rw_off]
        mu_r, mu_wd, mu_k, mu_v, mu_ad, mu_gd = (mu[mu_off[i]:mu_off[i + 1]] for i in range(6))
        mus = {"r": row(mu_r), "k": row(mu_k), "v": row(mu_v),
               "gd": row(jnp.pad(mu_gd, (0, 512 - g_lora))),
               "wa": row(jnp.concatenate([mu_wd, mu_ad]))}

        proj3 = proj.reshape(batch, seq, n_proj_pad)
        o_gla = _gla(proj3, _pad_rows(gla_gate_up[l], LANES).astype(BF16), row(gla_gate_bias[l]),
                     row(gla_out_norm[l]), consts["tri_rows"], cols, bb, gla_heads, dk, dv)
        o_rwkv = _rwkv(proj3, mus, row(rwkv_w0[l]), rwkv_w2[l].astype(BF16), row(rwkv_a0[l]),
                       rwkv_a2[l].astype(BF16), _pad_rows(rwkv_g2[l], 512).astype(BF16),
                       row(rwkv_k_k[l]), row(rwkv_k_a[l]), row(rwkv_r_k[l]), row(rwkv_ln_w[l]),
                       row(rwkv_ln_b[l]), consts, cols, bb, rw_w, lora)

        merged = _merge(o_gla.reshape(t, gla_vw), o_rwkv.reshape(t, rw_w),
                        w_up_gla[l].astype(BF16), w_up_rwkv[l].astype(BF16), proj,
                        cols["gate_gla"], cols["gate_rwkv"], bm, 512)
        mixed = _matmul(merged, w_out[l].astype(BF16), BF16, bm, 1024)
        h, u = _post_pre(h, mixed, row(mix_post_norm[l]), row(ffn2_pre_norm[l]), 1.0, bt)

        hid = _gate_up(u, ffn2_w_gate[l], ffn2_w_up[l], bm_gu, 256)
        f = _matmul(hid, ffn2_w_down[l].astype(BF16), BF16, bm_down, 512)
        h = _post(h, f, row(ffn2_post_norm[l]), MACARON_WEIGHT, bt)
    return h.reshape(batch, seq, d_model)
```

```python
import functools

import jax
import jax.numpy as jnp
from jax import lax
from jax.experimental import pallas as pl
from jax.experimental.pallas import tpu as pltpu

F32 = jnp.float32
BF16 = jnp.bfloat16

NORM_EPS = 1e-6
RWKV_LN_EPS = 64e-5
MACARON_WEIGHT = 0.5
GLA_GATE_NORM = 16.0
GLA_GATE_RANK = 16
LOG2_E = 1.4426950408889634
RWKV_HEAD = 64
RWKV_GATE_LORA = 480

LANES = 128
CHUNK = 64
SUB_LOG2 = 3
SUB = 1 << SUB_LOG2
GLA_HEADS_PER_STEP = 2
RWKV_GROUP = 4
RWKV_GROUPS_PER_STEP = 4
VMEM_LIMIT = 56 * 1024 * 1024
VMEM_LIMIT_MAX = 60 * 1024 * 1024


def _cparams(sem):
    return pltpu.CompilerParams(dimension_semantics=sem, vmem_limit_bytes=VMEM_LIMIT)


def _dot(a, b):
    return jnp.dot(a.astype(BF16), b.astype(BF16), preferred_element_type=F32)


def _dot_nt(a, b):
    return lax.dot_general(a.astype(BF16), b.astype(BF16), (((1,), (1,)), ((), ())),
                           preferred_element_type=F32)


def _dot_tn(a, b):
    return lax.dot_general(a.astype(BF16), b.astype(BF16), (((0,), (0,)), ((), ())),
                           preferred_element_type=F32)


def _dot_split(m01, x):
    hi = x.astype(BF16)
    lo = (x - hi.astype(F32)).astype(BF16)
    return (jnp.dot(m01, hi, preferred_element_type=F32)
            + jnp.dot(m01, lo, preferred_element_type=F32))


def _rms(x, g):
    ms = jnp.mean(x * x, axis=-1, keepdims=True)
    return x * lax.rsqrt(ms + NORM_EPS) * g


def _sigmoid(x):
    return 1.0 / (1.0 + jnp.exp(-x))


def _softplus(x):
    return jnp.maximum(x, 0.0) + jnp.log(1.0 + jnp.exp(-jnp.abs(x)))


def _norm_cast_kernel(x_ref, g_ref, o_ref):
    o_ref[...] = _rms(x_ref[...], g_ref[...]).astype(o_ref.dtype)


def _norm_cast(x, g, bt):
    t, d = x.shape
    return pl.pallas_call(
        _norm_cast_kernel,
        grid=(t // bt,),
        in_specs=[pl.BlockSpec((bt, d), lambda i: (i, 0)),
                  pl.BlockSpec((1, d), lambda i: (0, 0))],
        out_specs=pl.BlockSpec((bt, d), lambda i: (i, 0)),
        out_shape=jax.ShapeDtypeStruct((t, d), BF16),
        compiler_params=_cparams(("parallel",)),
    )(x, g)


def _post_pre_kernel(h_ref, f_ref, gpost_ref, gpre_ref, hout_ref, u_ref, *, alpha):
    h = h_ref[...] + alpha * _rms(f_ref[...].astype(F32), gpost_ref[...])
    hout_ref[...] = h
    u_ref[...] = _rms(h, gpre_ref[...]).astype(u_ref.dtype)


def _post_pre(h, f, g_post, g_pre, alpha, bt):
    t, d = h.shape
    row = pl.BlockSpec((bt, d), lambda i: (i, 0))
    vec = pl.BlockSpec((1, d), lambda i: (0, 0))
    return pl.pallas_call(
        functools.partial(_post_pre_kernel, alpha=alpha),
        grid=(t // bt,),
        in_specs=[row, row, vec, vec],
        out_specs=[row, row],
        out_shape=[jax.ShapeDtypeStruct((t, d), F32), jax.ShapeDtypeStruct((t, d), BF16)],
        compiler_params=_cparams(("parallel",)),
    )(h, f, g_post, g_pre)


def _post_kernel(h_ref, f_ref, gpost_ref, hout_ref, *, alpha):
    hout_ref[...] = h_ref[...] + alpha * _rms(f_ref[...].astype(F32), gpost_ref[...])


def _post(h, f, g_post, alpha, bt):
    t, d = h.shape
    row = pl.BlockSpec((bt, d), lambda i: (i, 0))
    vec = pl.BlockSpec((1, d), lambda i: (0, 0))
    return pl.pallas_call(
        functools.partial(_post_kernel, alpha=alpha),
        grid=(t // bt,),
        in_specs=[row, row, vec],
        out_specs=row,
        out_shape=jax.ShapeDtypeStruct((t, d), F32),
        compiler_params=_cparams(("parallel",)),
    )(h, f, g_post)


def _matmul_kernel(x_ref, w_ref, o_ref):
    o_ref[...] = jnp.dot(x_ref[...], w_ref[...], preferred_element_type=F32).astype(o_ref.dtype)


def _matmul_cast_kernel(x_ref, w_ref, c_ref, o_ref, c_bf16_ref):
    o_ref[...] = jnp.dot(x_ref[...], w_ref[...], preferred_element_type=F32).astype(o_ref.dtype)
    c_bf16_ref[...] = c_ref[...].astype(BF16)


def _matmul(x, w, out_dtype, bm, bn, cast=None):
    m, k = x.shape
    n = w.shape[1]
    nj = pl.cdiv(n, bn)
    specs = [pl.BlockSpec((bm, k), lambda i, j: (i, 0)), pl.BlockSpec((k, bn), lambda i, j: (0, j))]
    out_spec = pl.BlockSpec((bm, bn), lambda i, j: (i, j))
    out_shape = jax.ShapeDtypeStruct((m, n), out_dtype)
    if cast is None:
        return pl.pallas_call(
            _matmul_kernel, grid=(m // bm, nj), in_specs=specs, out_specs=out_spec, out_shape=out_shape,
            compiler_params=_cparams(("parallel", "arbitrary")),
        )(x, w)
    steps = (m // bm) * nj
    slab = cast.shape[0] // steps
    assert slab * steps == cast.shape[0] and slab % 16 == 0, (cast.shape, steps)
    slab_spec = pl.BlockSpec((slab, cast.shape[1]), lambda i, j: (i * nj + j, 0))
    return pl.pallas_call(
        _matmul_cast_kernel, grid=(m // bm, nj), in_specs=specs + [slab_spec],
        out_specs=[out_spec, slab_spec],
        out_shape=[out_shape, jax.ShapeDtypeStruct(cast.shape, BF16)],
        compiler_params=_cparams(("arbitrary", "arbitrary")),
    )(x, w, cast)


def _gate_up_kernel(x_ref, wg_ref, wu_ref, wd_ref, o_ref, wd_bf16_ref):
    x = x_ref[...]
    g = jnp.dot(x, wg_ref[...].astype(BF16), preferred_element_type=F32)
    u = jnp.dot(x, wu_ref[...].astype(BF16), preferred_element_type=F32)
    o_ref[...] = (g * _sigmoid(g) * u).astype(o_ref.dtype)
    wd_bf16_ref[...] = wd_ref[...].astype(BF16)


def _gate_up(x, wg, wu, wd, bm, bn):
    m, k = x.shape
    n = wg.shape[1]
    nj = pl.cdiv(n, bn)
    steps = (m // bm) * nj
    slab = wd.shape[0] // steps
    assert slab * steps == wd.shape[0] and slab % 16 == 0, (wd.shape, steps)
    wspec = pl.BlockSpec((k, bn), lambda i, j: (0, j))
    slab_spec = pl.BlockSpec((slab, wd.shape[1]), lambda i, j: (i * nj + j, 0))
    return pl.pallas_call(
        _gate_up_kernel,
        grid=(m // bm, nj),
        in_specs=[pl.BlockSpec((bm, k), lambda i, j: (i, 0)), wspec, wspec, slab_spec],
        out_specs=[pl.BlockSpec((bm, bn), lambda i, j: (i, j)), slab_spec],
        out_shape=[jax.ShapeDtypeStruct((m, n), BF16), jax.ShapeDtypeStruct(wd.shape, BF16)],
        compiler_params=pltpu.CompilerParams(dimension_semantics=("arbitrary", "arbitrary"),
                                             vmem_limit_bytes=VMEM_LIMIT_MAX),
    )(x, wg, wu, wd)


def _merge_kernel(og_ref, or_ref, wg_ref, wr_ref, gg_ref, gr_ref, o_ref):
    yg = jnp.dot(og_ref[...], wg_ref[...], preferred_element_type=F32)
    yr = jnp.dot(or_ref[...], wr_ref[...], preferred_element_type=F32)
    o_ref[...] = (_sigmoid(gg_ref[...]) * yg + _sigmoid(gr_ref[...]) * yr).astype(o_ref.dtype)


def _merge(o_gla, o_rwkv, w_up_gla, w_up_rwkv, proj, col_gate_gla, col_gate_rwkv, bm, bn):
    m, k = o_gla.shape
    n = w_up_gla.shape[1]
    xspec = pl.BlockSpec((bm, k), lambda i, j: (i, 0))
    wspec = pl.BlockSpec((k, bn), lambda i, j: (0, j))
    ga, gb = col_gate_gla // bn, col_gate_rwkv // bn
    return pl.pallas_call(
        _merge_kernel,
        grid=(m // bm, n // bn),
        in_specs=[xspec, xspec, wspec, wspec,
                  pl.BlockSpec((bm, bn), lambda i, j: (i, ga + j)),
                  pl.BlockSpec((bm, bn), lambda i, j: (i, gb + j))],
        out_specs=pl.BlockSpec((bm, bn), lambda i, j: (i, j)),
        out_shape=jax.ShapeDtypeStruct((m, n), BF16),
        compiler_params=_cparams(("parallel", "arbitrary")),
    )(o_gla, o_rwkv, w_up_gla, w_up_rwkv, proj, proj)


def _gla_kernel(q_ref, k_ref, v_ref, og_ref, gd_ref, gup_ref, gbias_ref, onorm_ref, tri_ref, sel_ref,
                o_ref, state_ref, *, dk, dv, bb, hq):
    c, nsub = CHUNK, CHUNK // SUB
    rows = bb * c

    @pl.when(pl.program_id(2) == 0)
    def _():
        state_ref[...] = jnp.zeros_like(state_ref)

    def merged(ref, lanes=slice(None)):
        return ref[:, :, lanes].reshape(rows, -1)

    def per_row(x):
        return [x[i * c:(i + 1) * c] for i in range(bb)]

    def chunk_row(x, r):
        return jnp.concatenate(
            [jnp.broadcast_to(x[i * c + r:i * c + r + 1, :], (c, x.shape[-1])) for i in range(bb)], axis=0)

    row = lax.broadcasted_iota(jnp.int32, (rows, c), 0) & (c - 1)
    lane = lax.broadcasted_iota(jnp.int32, (rows, c), 1)
    blk0 = row & -SUB
    d = lane - blk0
    dc = jnp.where(d >= 0, jnp.where(d <= (row & (SUB - 1)), d, -1), -1)
    mask_off = (lane < blk0).astype(F32)
    sub_id = (lax.broadcasted_iota(jnp.int32, (rows, dk), 0) & (c - 1)) >> SUB_LOG2
    gd = merged(gd_ref).astype(BF16)
    tri = tri_ref[...]

    qs, ks, bs, qhat, khat, b_ends = [], [], [], [], [], []
    for h in range(hq):
        lanes = slice(h * dk, (h + 1) * dk)
        x = _dot(gd, gup_ref[:, lanes]) + gbias_ref[:, lanes]
        log_a = -_softplus(-x) * (1.0 / GLA_GATE_NORM)
        b = _dot_split(tri, log_a * LOG2_E)
        q = merged(q_ref, lanes) * (dk ** -0.5)
        k = merged(k_ref, lanes)
        refs = [chunk_row(b, i * SUB - 1) for i in range(1, nsub)]
        bref = jnp.zeros((rows, dk), F32)
        for i in range(1, nsub):
            bref = jnp.where(sub_id == i, refs[i - 1], bref)
        qhat.append(per_row(q * jnp.exp2(b - bref)))
        khat.append([per_row(k * jnp.exp2(jnp.minimum(ref - b, 0.0))) for ref in refs])
        qs.append(q)
        ks.append(k)
        bs.append(b)
        b_ends.append(chunk_row(b, c - 1))

    attn = [jnp.concatenate(
        [jnp.concatenate(
            [jnp.zeros((SUB, c), F32)]
            + [_dot_nt(qhat[h][r][i * SUB:(i + 1) * SUB, :], khat[h][i - 1][r]) for i in range(1, nsub)],
            axis=0)
         for r in range(bb)], axis=0) * mask_off for h in range(hq)]

    for h in range(hq):
        q, k, b = qs[h], ks[h], bs[h]
        k3 = k.reshape(bb * nsub, SUB, dk)
        b3 = b.reshape(bb * nsub, SUB, dk)
        terms = []
        for j in range(SUB):
            kj = jnp.broadcast_to(k3[:, j:j + 1, :], (bb * nsub, SUB, dk)).reshape(rows, dk)
            bj = jnp.broadcast_to(b3[:, j:j + 1, :], (bb * nsub, SUB, dk)).reshape(rows, dk)
            terms.append((q * kj * jnp.exp2(jnp.minimum(b - bj, 0.0))).astype(BF16))
        diag = jnp.dot(jnp.concatenate(terms, axis=1), sel_ref[...], preferred_element_type=F32)
        attn[h] = jnp.where(dc >= 0, diag, attn[h])

    chains = [(h, i) for h in range(hq) for i in range(bb)]
    attn_c = [t for h in range(hq) for t in per_row(attn[h])]
    qe = [t for h in range(hq) for t in per_row(qs[h] * jnp.exp2(bs[h]))]
    k_end = [t for h in range(hq) for t in per_row(ks[h] * jnp.exp2(b_ends[h] - bs[h]))]
    v_c = [v_ref[i, :, h * dv:(h + 1) * dv] for h, i in chains]
    st = [state_ref[i, h] for h, i in chains]
    o = [_dot(a_, v_) + _dot_nt(q_, s_) for a_, v_, q_, s_ in zip(attn_c, v_c, qe, st)]
    for j, (h, i) in enumerate(chains):
        state_ref[i, h] = st[j] * jnp.exp2(b_ends[h][i * c:i * c + 1, :]) + _dot_tn(v_c[j], k_end[j])

    for h in range(hq):
        lanes = slice(h * dv, (h + 1) * dv)
        oh = _rms(jnp.concatenate(o[h * bb:(h + 1) * bb], axis=0), onorm_ref[...])
        og = merged(og_ref, lanes)
        o_ref[:, :, lanes] = (oh * (og * _sigmoid(og))).astype(o_ref.dtype).reshape(bb, c, dv)


def _gla(proj, gate_up, gate_bias, out_norm, tri, cols, bb, heads, dk, dv):
    batch, seq, _ = proj.shape
    nc = seq // CHUNK
    sel = (jnp.repeat(jnp.arange(SUB), dk)[:, None] == (jnp.arange(CHUNK) % SUB)[None, :]).astype(BF16)
    hq = next(m for m in (GLA_HEADS_PER_STEP, 1) if heads % m == 0)
    kw, vw = hq * dk, hq * dv
    cq, ck, cv, cog, cgd = (cols[n] for n in ("gla_q", "gla_k", "gla_v", "gla_og", "gla_gd"))
    return pl.pallas_call(
        functools.partial(_gla_kernel, dk=dk, dv=dv, bb=bb, hq=hq),
        grid=(batch // bb, heads // hq, nc),
        in_specs=[
            pl.BlockSpec((bb, CHUNK, kw), lambda b, h, c: (b, c, cq // kw + h)),
            pl.BlockSpec((bb, CHUNK, kw), lambda b, h, c: (b, c, ck // kw + h)),
            pl.BlockSpec((bb, CHUNK, vw), lambda b, h, c: (b, c, cv // vw + h)),
            pl.BlockSpec((bb, CHUNK, vw), lambda b, h, c: (b, c, cog // vw + h)),
            pl.BlockSpec((bb, CHUNK, LANES), lambda b, h, c: (b, c, cgd // LANES)),
            pl.BlockSpec((LANES, kw), lambda b, h, c: (0, h)),
            pl.BlockSpec((1, kw), lambda b, h, c: (0, h)),
            pl.BlockSpec((1, dv), lambda b, h, c: (0, 0)),
            pl.BlockSpec((bb * CHUNK, bb * CHUNK), lambda b, h, c: (0, 0)),
            pl.BlockSpec((SUB * dk, CHUNK), lambda b, h, c: (0, 0)),
        ],
        out_specs=pl.BlockSpec((bb, CHUNK, vw), lambda b, h, c: (b, c, h)),
        out_shape=jax.ShapeDtypeStruct((batch, seq, heads * dv), BF16),
        scratch_shapes=[pltpu.VMEM((bb, hq, dv, dk), F32)],
        compiler_params=_cparams(("parallel", "parallel", "arbitrary")),
    )(proj, proj, proj, proj, proj, gate_up, gate_bias, out_norm, tri, sel)


def _rwkv_kernel(r_ref, k_ref, v_ref, gd_ref, wa_ref, rp_ref, kp_ref, vp_ref, gdp_ref, wap_ref,
                 mur_ref, muk_ref, muv_ref, mugd_ref, muwa_ref,
                 w0_ref, w2_ref, a0_ref, a2_ref, g2_ref, kk_ref, ka_ref, rk_ref, lnw_ref, lnb_ref,
                 tri_ref, ones_ref, mstrict_ref, mincl_ref, mblk_ref,
                 o_ref, state_ref, *, lora, bb, gq):
    c, g = CHUNK, RWKV_GROUP
    n, rows, gw = g * c, bb * c, g * RWKV_HEAD
    first = pl.program_id(2) == 0

    @pl.when(first)
    def _():
        state_ref[...] = jnp.zeros_like(state_ref)

    keep_prev = jnp.where(first, 0.0, 1.0)

    def per_row(x):
        return [x[i * c:(i + 1) * c] for i in range(bb)]

    def shifted(cur_ref, prev_ref, mu_ref, lanes):
        p = cur_ref[:, :, lanes].reshape(rows, -1)
        width = p.shape[-1]
        prev = jnp.concatenate(
            [jnp.broadcast_to(prev_ref[i, 7:8, lanes] * keep_prev, (c, width)) for i in range(bb)], axis=0)
        is_row0 = (lax.broadcasted_iota(jnp.int32, p.shape, 0) & (c - 1)) == 0
        p_prev = jnp.where(is_row0, prev, pltpu.roll(p, 1, axis=0))
        return p + (p_prev - p) * mu_ref[:, lanes]

    everything = slice(None)
    wa = shifted(wa_ref, wap_ref, muwa_ref, everything)
    tanh_wd = jnp.tanh(wa[:, :lora]).astype(BF16)
    a_down = wa[:, lora:].astype(BF16)
    sig_gd = _sigmoid(shifted(gd_ref, gdp_ref, mugd_ref, everything)).astype(BF16)

    ones_bd = ones_ref[...]
    m_blk = mblk_ref[...]
    m_blk_bf16 = m_blk.astype(BF16)
    tri = tri_ref[...]

    r_all, k2_all, v_all, gate_all = [], [], [], []
    rt, kkt, km, bm, k_end, b_end, v_rows, c_end = [], [], [], [], [], [], [], []
    for q in range(gq):
        lanes = slice(q * gw, (q + 1) * gw)
        r = shifted(r_ref, rp_ref, mur_ref, lanes)
        k = shifted(k_ref, kp_ref, muk_ref, lanes)
        v = shifted(v_ref, vp_ref, muv_ref, lanes)
        w = -_softplus(-(w0_ref[:, lanes] + _dot(tanh_wd, w2_ref[:, lanes]))) - 0.5
        ld = -jnp.exp(w)
        a = _sigmoid(a0_ref[:, lanes] + _dot(a_down, a2_ref[:, lanes]))
        gate_all.append(_dot(sig_gd, g2_ref[:, lanes]))

        kkr = k * kk_ref[:, lanes]
        kk = kkr * lax.rsqrt(jnp.maximum(_dot(kkr * kkr, ones_bd), 1e-24))
        k2 = k * (1.0 + (a - 1.0) * ka_ref[:, lanes])
        bw = a * kk

        cum = _dot_split(tri, ld)
        ends = [cum[(i + 1) * c - 1:(i + 1) * c, :] for i in range(bb)]
        c_end_rows = jnp.concatenate([jnp.broadcast_to(e, (c, gw)) for e in ends], axis=0)
        e_neg = jnp.exp(-cum)
        e_end = jnp.exp(c_end_rows - cum)
        c_end += ends
        rt += per_row(r * jnp.exp(cum))
        kkt += per_row(kk * jnp.exp(cum - ld))
        km += per_row(k2 * e_neg)
        bm += per_row(bw * e_neg)
        k_end += per_row(k2 * e_end)
        b_end += per_row(bw * e_end)
        v_rows += per_row(v)
        r_all.append(r)
        k2_all.append(k2)
        v_all.append(v)

    def stack(x):
        return jnp.concatenate([x.astype(BF16)] * g, axis=0) * m_blk_bf16

    m_strict, m_incl = mstrict_ref[...], mincl_ref[...]
    lhs = [jnp.concatenate([kq, rq], axis=0).astype(BF16) for kq, rq in zip(kkt, rt)]
    aa = [_dot_nt(l, jnp.concatenate([stack(b_), stack(k_)], axis=0))
          for l, b_, k_ in zip(lhs, bm, km)]
    a_ab = [t[:c, :n] * m_strict for t in aa]
    a_kr = [jnp.concatenate([t[:c, n:] * m_strict, t[c:, n:] * m_incl], axis=0) for t in aa]
    a_rb = [t[c:, :n] * m_incl for t in aa]

    eye = m_incl - m_strict
    x = [eye - t for t in a_ab]
    p = [_dot(t, stack(t)) for t in a_ab]
    span = 2
    while span < c:
        span *= 2
        if span < c:
            xp = [_dot(jnp.concatenate([xi, pi], axis=0), stack(pi)) for xi, pi in zip(x, p)]
            x = [xi + t[:c] for xi, t in zip(x, xp)]
            p = [t[c:] for t in xp]
        else:
            x = [xi + _dot(xi, stack(pi)) for xi, pi in zip(x, p)]

    chains = [(q, i) for q in range(gq) for i in range(bb)]
    st = [state_ref[i, q] for q, i in chains]
    sk = [_dot_nt(l, s) for l, s in zip(lhs, st)]
    av = [_dot(t, stack(vi)) for t, vi in zip(a_kr, v_rows)]
    u = [_dot(xi, stack(-(s[:c] + t[:c]))) for xi, s, t in zip(x, sk, av)]
    y = [s[c:] + _dot(t, stack(ui)) + w_[c:] for s, t, ui, w_ in zip(sk, a_rb, u, av)]
    for j, (q, i) in enumerate(chains):
        state_ref[i, q] = st[j] * jnp.exp(c_end[j]) + _dot_tn(
            jnp.concatenate([u[j], v_rows[j]], axis=0),
            jnp.concatenate([b_end[j], k_end[j]], axis=0)) * m_blk

    inv_n = 1.0 / RWKV_HEAD
    for q in range(gq):
        lanes = slice(q * gw, (q + 1) * gw)
        yq = jnp.concatenate(y[q * bb:(q + 1) * bb], axis=0)
        mu = _dot(yq, ones_bd) * inv_n
        dy = yq - mu
        var = _dot(dy * dy, ones_bd) * inv_n
        yn = dy * lax.rsqrt(var + RWKV_LN_EPS) * lnw_ref[:, lanes] + lnb_ref[:, lanes]
        bonus = _dot(r_all[q] * k2_all[q] * rk_ref[:, lanes], ones_bd) * v_all[q]
        o_ref[:, :, lanes] = ((yn + bonus) * gate_all[q]).astype(o_ref.dtype).reshape(bb, c, gw)


def _rwkv(proj, mus, w0, w2, a0, a2, g2, k_k, k_a, r_k, ln_w, ln_b, consts, cols, bb, width, lora):
    batch, seq, _ = proj.shape
    nc = seq // CHUNK
    gw = RWKV_GROUP * RWKV_HEAD
    gq = next(m for m in (RWKV_GROUPS_PER_STEP, 1) if (width // gw) % m == 0)
    sw = gq * gw
    steps = width // sw
    gdw = g2.shape[0]
    waw = 2 * lora

    def prev_rows(c):
        return jnp.maximum(c * (CHUNK // 8) - 1, 0)

    def cur(w_, col, per_group):
        if per_group:
            return pl.BlockSpec((bb, CHUNK, w_), lambda b, q, c: (b, c, col // w_ + q))
        return pl.BlockSpec((bb, CHUNK, w_), lambda b, q, c: (b, c, col // w_))

    def prev(w_, col, per_group):
        if per_group:
            return pl.BlockSpec((bb, 8, w_), lambda b, q, c: (b, prev_rows(c), col // w_ + q))
        return pl.BlockSpec((bb, 8, w_), lambda b, q, c: (b, prev_rows(c), col // w_))

    def gvec():
        return pl.BlockSpec((1, sw), lambda b, q, c: (0, q))

    def full(shape):
        return pl.BlockSpec(shape, lambda b, q, c: (0, 0))

    n = RWKV_GROUP * CHUNK
    in_specs = [
        cur(sw, cols["rw_r"], True), cur(sw, cols["rw_k"], True), cur(sw, cols["rw_v"], True),
        cur(gdw, cols["rw_gd"], False), cur(waw, cols["rw_wa"], False),
        prev(sw, cols["rw_r"], True), prev(sw, cols["rw_k"], True), prev(sw, cols["rw_v"], True),
        prev(gdw, cols["rw_gd"], False), prev(waw, cols["rw_wa"], False),
        gvec(), gvec(), gvec(), full((1, gdw)), full((1, waw)),
        gvec(), pl.BlockSpec((lora, sw), lambda b, q, c: (0, q)),
        gvec(), pl.BlockSpec((lora, sw), lambda b, q, c: (0, q)),
        pl.BlockSpec((gdw, sw), lambda b, q, c: (0, q)),
        gvec(), gvec(), gvec(), gvec(), gvec(),
        full((bb * CHUNK, bb * CHUNK)), full((gw, gw)), full((CHUNK, n)), full((CHUNK, n)), full((n, gw)),
    ]
    return pl.pallas_call(
        functools.partial(_rwkv_kernel, lora=lora, bb=bb, gq=gq),
        grid=(batch // bb, steps, nc),
        in_specs=in_specs,
        out_specs=pl.BlockSpec((bb, CHUNK, sw), lambda b, q, c: (b, c, q)),
        out_shape=jax.ShapeDtypeStruct((batch, seq, width), BF16),
        scratch_shapes=[pltpu.VMEM((bb, gq, gw, gw), F32)],
        compiler_params=_cparams(("parallel", "parallel", "arbitrary")),
    )(proj, proj, proj, proj, proj, proj, proj, proj, proj, proj,
      mus["r"], mus["k"], mus["v"], mus["gd"], mus["wa"],
      w0, w2, a0, a2, g2, k_k, k_a, r_k, ln_w, ln_b,
      consts["tri_rows"], consts["ones_bd"], consts["m_strict"], consts["m_incl"], consts["m_blk"])


def _pad_cols(w, n):
    return jnp.pad(w, ((0, 0), (0, n - w.shape[1])))


def _pad_rows(w, n):
    return jnp.pad(w, ((0, n - w.shape[0]), (0, 0)))


def _layout(d_model):
    kw, vw, rw = d_model // 4, d_model // 2, d_model // 2
    lora = 128
    gdw = 512
    names = [("gla_q", kw), ("gla_k", kw), ("gla_v", vw), ("gla_og", vw),
             ("rw_r", rw), ("rw_k", rw), ("rw_v", rw),
             ("gate_gla", d_model), ("gate_rwkv", d_model),
             ("rw_gd", gdw), ("rw_wa", 2 * lora), ("gla_gd", LANES)]
    cols, off = {}, 0
    for name, w in names:
        assert off % w == 0, (name, off, w)
        cols[name] = off
        off += w
    return cols, off


def _chunk_consts(bb):
    c, g, hn = CHUNK, RWKV_GROUP, RWKV_HEAD
    assert c == hn
    n = g * c
    i = jnp.arange(n)
    ti, tj = jnp.arange(c)[:, None], i[None, :] % c
    lane = jnp.arange(g * hn)
    return {
        "tri_rows": jnp.kron(jnp.eye(bb, dtype=F32), jnp.tril(jnp.ones((c, c), F32))).astype(BF16),
        "ones_bd": ((lane[:, None] // hn) == (lane[None, :] // hn)).astype(BF16),
        "m_strict": (ti > tj).astype(F32),
        "m_incl": (ti >= tj).astype(F32),
        "m_blk": ((i[:, None] // c) == (lane[None, :] // hn)).astype(F32),
    }


def kernel(x, ffn1_pre_norm, ffn1_w_gate, ffn1_w_up, ffn1_w_down, ffn1_post_norm, mix_pre_norm, w_in, gla_gate_up, gla_gate_bias, gla_out_norm, rwkv_shift_mix, rwkv_w0, rwkv_w2, rwkv_a0, rwkv_a2, rwkv_g2, rwkv_k_k, rwkv_k_a, rwkv_r_k, rwkv_ln_w, rwkv_ln_b, w_up_gla, w_up_rwkv, w_out, mix_post_norm, ffn2_pre_norm, ffn2_w_gate, ffn2_w_up, ffn2_w_down, ffn2_post_norm):
    batch, seq, d_model = x.shape
    depth = ffn1_pre_norm.shape[0]
    t = batch * seq
    assert seq % CHUNK == 0 and d_model % 1024 == 0

    gla_kw, gla_vw, rw_w = d_model // 4, d_model // 2, d_model // 2
    gla_heads = max(4, d_model // 512)
    dk, dv = gla_kw // gla_heads, gla_vw // gla_heads
    lora = rwkv_w2.shape[1]
    g_lora = rwkv_g2.shape[1]
    assert lora == 128 and g_lora <= 512 and dk % LANES == 0

    cols, n_proj = _layout(d_model)
    n_proj_pad = -(-n_proj // 1024) * 1024
    bm = min(1024, t)
    bm_down = min(512, t)
    bm_gu = min(2048, t)
    bt = min(256, t)
    bb = next(n for n in (4, 2, 1) if batch % n == 0)
    consts = _chunk_consts(bb)

    def row(v):
        return v.reshape(1, -1).astype(F32)

    h = x.reshape(t, d_model)
    for l in range(depth):
        xn = _norm_cast(h, row(ffn1_pre_norm[l]), bt)
        hid, w_down = _gate_up(xn, ffn1_w_gate[l], ffn1_w_up[l], ffn1_w_down[l], bm_gu, 256)
        f, wi = _matmul(hid, w_down, BF16, bm_down, 512, cast=w_in[l])
        h, u = _post_pre(h, f, row(ffn1_post_norm[l]), row(mix_pre_norm[l]), MACARON_WEIGHT, bt)

        o_gla, o_rw, o_gate = 0, 2 * gla_kw + 2 * gla_vw + GLA_GATE_RANK, None
        gq, gk, gv, ggd, gog = jnp.split(wi[:, :o_rw], [gla_kw, 2 * gla_kw, 2 * gla_kw + gla_vw,
                                                       2 * gla_kw + gla_vw + GLA_GATE_RANK], axis=1)
        rw_sizes = [rw_w, lora, rw_w, rw_w, lora, g_lora]
        rw_off = [o_rw]
        for s in rw_sizes:
            rw_off.append(rw_off[-1] + s)
        rr, rwd, rk, rv, rad, rgd = (wi[:, rw_off[i]:rw_off[i + 1]] for i in range(6))
        gates = wi[:, rw_off[-1]:]
        w_proj = jnp.concatenate(
            [gq, gk, gv, gog, rr, rk, rv, gates, _pad_cols(rgd, 512), rwd, rad, _pad_cols(ggd, LANES),
             jnp.zeros((d_model, n_proj_pad - n_proj), BF16)], axis=1)
        proj = _matmul(u, w_proj, F32, bm, 1024)

        mu = rwkv_shift_mix[l]
        mu_off = [o - o_rw for o in rw_off]
        mu_r, mu_wd, mu_k, mu_v, mu_ad, mu_gd = (mu[mu_off[i]:mu_off[i + 1]] for i in range(6))
        mus = {"r": row(mu_r), "k": row(mu_k), "v": row(mu_v),
               "gd": row(jnp.pad(mu_gd, (0, 512 - g_lora))),
               "wa": row(jnp.concatenate([mu_wd, mu_ad]))}

        proj3 = proj.reshape(batch, seq, n_proj_pad)
        o_gla = _gla(proj3, _pad_rows(gla_gate_up[l], LANES).astype(BF16), row(gla_gate_bias[l]),
                     row(gla_out_norm[l]), consts["tri_rows"], cols, bb, gla_heads, dk, dv)
        o_rwkv = _rwkv(proj3, mus, row(rwkv_w0[l]), rwkv_w2[l].astype(BF16), row(rwkv_a0[l]),
                       rwkv_a2[l].astype(BF16), _pad_rows(rwkv_g2[l], 512).astype(BF16),
                       row(rwkv_k_k[l]), row(rwkv_k_a[l]), row(rwkv_r_k[l]), row(rwkv_ln_w[l]),
                       row(rwkv_ln_b[l]), consts, cols, bb, rw_w, lora)

        merged = _merge(o_gla.reshape(t, gla_vw), o_rwkv.reshape(t, rw_w),
                        w_up_gla[l].astype(BF16), w_up_rwkv[l].astype(BF16), proj,
                        cols["gate_gla"], cols["gate_rwkv"], bm, 512)
        mixed = _matmul(merged, w_out[l].astype(BF16), BF16, bm, 1024)
        h, u = _post_pre(h, mixed, row(mix_post_norm[l]), row(ffn2_pre_norm[l]), 1.0, bt)

        hid, w_down = _gate_up(u, ffn2_w_gate[l], ffn2_w_up[l], ffn2_w_down[l], bm_gu, 256)
        f = _matmul(hid, w_down, BF16, bm_down, 512)
        h = _post(h, f, row(ffn2_post_norm[l]), MACARON_WEIGHT, bt)
    return h.reshape(batch, seq, d_model)
```

```python
import functools

import jax
import jax.numpy as jnp
from jax import lax
from jax.experimental import pallas as pl
from jax.experimental.pallas import tpu as pltpu

F32 = jnp.float32
BF16 = jnp.bfloat16

NORM_EPS = 1e-6
RWKV_LN_EPS = 64e-5
MACARON_WEIGHT = 0.5
GLA_GATE_NORM = 16.0
GLA_GATE_RANK = 16
LOG2_E = 1.4426950408889634
RWKV_HEAD = 64

LANES = 128
CHUNK = 64
SUB_LOG2 = 3
SUB = 1 << SUB_LOG2
GLA_HEADS_PER_STEP = 2
RWKV_GROUP = 4
RWKV_GROUPS_PER_STEP = 4
VMEM_LIMIT = 56 * 1024 * 1024
VMEM_LIMIT_MAX = 60 * 1024 * 1024


def _cparams(sem):
    return pltpu.CompilerParams(dimension_semantics=sem, vmem_limit_bytes=VMEM_LIMIT)


def _dot(a, b):
    return jnp.dot(a.astype(BF16), b.astype(BF16), preferred_element_type=F32)


def _dot_nt(a, b):
    return lax.dot_general(a.astype(BF16), b.astype(BF16), (((1,), (1,)), ((), ())),
                           preferred_element_type=F32)


def _dot_tn(a, b):
    return lax.dot_general(a.astype(BF16), b.astype(BF16), (((0,), (0,)), ((), ())),
                           preferred_element_type=F32)


def _dot_split(m01, x):
    hi = x.astype(BF16)
    lo = (x - hi.astype(F32)).astype(BF16)
    return (jnp.dot(m01, hi, preferred_element_type=F32)
            + jnp.dot(m01, lo, preferred_element_type=F32))


def _rms(x, g):
    ms = jnp.mean(x * x, axis=-1, keepdims=True)
    return x * lax.rsqrt(ms + NORM_EPS) * g


def _sigmoid(x):
    return 1.0 / (1.0 + jnp.exp(-x))


def _softplus(x):
    return jnp.maximum(x, 0.0) + jnp.log(1.0 + jnp.exp(-jnp.abs(x)))


def _norm_cast_kernel(x_ref, g_ref, o_ref):
    o_ref[...] = _rms(x_ref[...], g_ref[...]).astype(o_ref.dtype)


def _norm_cast(x, g, bt):
    t, d = x.shape
    return pl.pallas_call(
        _norm_cast_kernel,
        grid=(t // bt,),
        in_specs=[pl.BlockSpec((bt, d), lambda i: (i, 0)),
                  pl.BlockSpec((1, d), lambda i: (0, 0))],
        out_specs=pl.BlockSpec((bt, d), lambda i: (i, 0)),
        out_shape=jax.ShapeDtypeStruct((t, d), BF16),
        compiler_params=_cparams(("parallel",)),
    )(x, g)


def _post_pre_kernel(h_ref, f_ref, gpost_ref, gpre_ref, hout_ref, u_ref, *, alpha):
    h = h_ref[...] + alpha * _rms(f_ref[...].astype(F32), gpost_ref[...])
    hout_ref[...] = h
    u_ref[...] = _rms(h, gpre_ref[...]).astype(u_ref.dtype)


def _post_pre(h, f, g_post, g_pre, alpha, bt):
    t, d = h.shape
    row = pl.BlockSpec((bt, d), lambda i: (i, 0))
    vec = pl.BlockSpec((1, d), lambda i: (0, 0))
    return pl.pallas_call(
        functools.partial(_post_pre_kernel, alpha=alpha),
        grid=(t // bt,),
        in_specs=[row, row, vec, vec],
        out_specs=[row, row],
        out_shape=[jax.ShapeDtypeStruct((t, d), F32), jax.ShapeDtypeStruct((t, d), BF16)],
        compiler_params=_cparams(("parallel",)),
    )(h, f, g_post, g_pre)


def _post_kernel(h_ref, f_ref, gpost_ref, hout_ref, *, alpha):
    hout_ref[...] = h_ref[...] + alpha * _rms(f_ref[...].astype(F32), gpost_ref[...])


def _post(h, f, g_post, alpha, bt):
    t, d = h.shape
    row = pl.BlockSpec((bt, d), lambda i: (i, 0))
    vec = pl.BlockSpec((1, d), lambda i: (0, 0))
    return pl.pallas_call(
        functools.partial(_post_kernel, alpha=alpha),
        grid=(t // bt,),
        in_specs=[row, row, vec],
        out_specs=row,
        out_shape=jax.ShapeDtypeStruct((t, d), F32),
        compiler_params=_cparams(("parallel",)),
    )(h, f, g_post)


def _matmul_kernel(x_ref, w_ref, o_ref):
    o_ref[...] = jnp.dot(x_ref[...], w_ref[...], preferred_element_type=F32).astype(o_ref.dtype)


def _matmul_nt_kernel(x_ref, w_ref, o_ref):
    o_ref[...] = lax.dot_general(x_ref[...], w_ref[...], (((1,), (1,)), ((), ())),
                                 preferred_element_type=F32).astype(o_ref.dtype)


def _matmul_nt(x, w_t, out_dtype, bm, bn):
    m, k = x.shape
    n = w_t.shape[0]
    return pl.pallas_call(
        _matmul_nt_kernel, grid=(m // bm, n // bn),
        in_specs=[pl.BlockSpec((bm, k), lambda i, j: (i, 0)), pl.BlockSpec((bn, k), lambda i, j: (j, 0))],
        out_specs=pl.BlockSpec((bm, bn), lambda i, j: (i, j)),
        out_shape=jax.ShapeDtypeStruct((m, n), out_dtype),
        compiler_params=_cparams(("parallel", "arbitrary")),
    )(x, w_t)


def _matmul(x, w, out_dtype, bm, bn):
    m, k = x.shape
    n = w.shape[1]
    return pl.pallas_call(
        _matmul_kernel,
        grid=(m // bm, pl.cdiv(n, bn)),
        in_specs=[pl.BlockSpec((bm, k), lambda i, j: (i, 0)),
                  pl.BlockSpec((k, bn), lambda i, j: (0, j))],
        out_specs=pl.BlockSpec((bm, bn), lambda i, j: (i, j)),
        out_shape=jax.ShapeDtypeStruct((m, n), out_dtype),
        compiler_params=_cparams(("parallel", "arbitrary")),
    )(x, w)


def _gate_up_kernel(x_ref, wg_ref, wu_ref, wd_ref, o_ref, wd_bf16_ref):
    x = x_ref[...]
    g = jnp.dot(x, wg_ref[...].astype(BF16), preferred_element_type=F32)
    u = jnp.dot(x, wu_ref[...].astype(BF16), preferred_element_type=F32)
    o_ref[...] = (g * _sigmoid(g) * u).astype(o_ref.dtype)
    wd_bf16_ref[...] = wd_ref[...].astype(BF16)


def _gate_up(x, wg, wu, wd, bm, bn):
    m, k = x.shape
    n = wg.shape[1]
    nj = pl.cdiv(n, bn)
    steps = (m // bm) * nj
    slab = wd.shape[0] // steps
    assert slab * steps == wd.shape[0] and slab % 16 == 0, (wd.shape, steps)
    wspec = pl.BlockSpec((k, bn), lambda i, j: (0, j))
    slab_spec = pl.BlockSpec((slab, wd.shape[1]), lambda i, j: (i * nj + j, 0))
    return pl.pallas_call(
        _gate_up_kernel,
        grid=(m // bm, nj),
        in_specs=[pl.BlockSpec((bm, k), lambda i, j: (i, 0)), wspec, wspec, slab_spec],
        out_specs=[pl.BlockSpec((bm, bn), lambda i, j: (i, j)), slab_spec],
        out_shape=[jax.ShapeDtypeStruct((m, n), BF16), jax.ShapeDtypeStruct(wd.shape, BF16)],
        compiler_params=pltpu.CompilerParams(dimension_semantics=("arbitrary", "arbitrary"),
                                             vmem_limit_bytes=VMEM_LIMIT_MAX),
    )(x, wg, wu, wd)


def _merge_kernel(og_ref, or_ref, wg_ref, wr_ref, gg_ref, gr_ref, o_ref):
    yg = jnp.dot(og_ref[...], wg_ref[...], preferred_element_type=F32)
    yr = jnp.dot(or_ref[...], wr_ref[...], preferred_element_type=F32)
    o_ref[...] = (_sigmoid(gg_ref[...]) * yg + _sigmoid(gr_ref[...]) * yr).astype(o_ref.dtype)


def _merge(o_gla, o_rwkv, w_up_gla, w_up_rwkv, proj, col_gate_gla, col_gate_rwkv, bm, bn):
    m, k = o_gla.shape
    n = w_up_gla.shape[1]
    xspec = pl.BlockSpec((bm, k), lambda i, j: (i, 0))
    wspec = pl.BlockSpec((k, bn), lambda i, j: (0, j))
    ga, gb = col_gate_gla // bn, col_gate_rwkv // bn
    return pl.pallas_call(
        _merge_kernel,
        grid=(m // bm, n // bn),
        in_specs=[xspec, xspec, wspec, wspec,
                  pl.BlockSpec((bm, bn), lambda i, j: (i, ga + j)),
                  pl.BlockSpec((bm, bn), lambda i, j: (i, gb + j))],
        out_specs=pl.BlockSpec((bm, bn), lambda i, j: (i, j)),
        out_shape=jax.ShapeDtypeStruct((m, n), BF16),
        compiler_params=_cparams(("parallel", "arbitrary")),
    )(o_gla, o_rwkv, w_up_gla, w_up_rwkv, proj, proj)


def _gla_kernel(q_ref, k_ref, v_ref, og_ref, gd_ref, gup_ref, gbias_ref, onorm_ref, tri_ref,
                o_ref, state_ref, *, dk, dv, bb, hq):
    c, nsub = CHUNK, CHUNK // SUB
    rows = bb * c

    @pl.when(pl.program_id(2) == 0)
    def _():
        state_ref[...] = jnp.zeros_like(state_ref)

    def merged(ref, lanes=slice(None)):
        return ref[:, :, lanes].reshape(rows, -1)

    def per_row(x):
        return [x[i * c:(i + 1) * c] for i in range(bb)]

    def chunk_row(x, r):
        return jnp.concatenate(
            [jnp.broadcast_to(x[i * c + r:i * c + r + 1, :], (c, x.shape[-1])) for i in range(bb)], axis=0)

    row = lax.broadcasted_iota(jnp.int32, (rows, c), 0) & (c - 1)
    lane = lax.broadcasted_iota(jnp.int32, (rows, c), 1)
    blk0 = row & -SUB
    d = lane - blk0
    dc = jnp.where(d >= 0, jnp.where(d <= (row & (SUB - 1)), d, -1), -1)
    mask_off = (lane < blk0).astype(F32)
    sub_id = (lax.broadcasted_iota(jnp.int32, (rows, dk), 0) & (c - 1)) >> SUB_LOG2
    gd = merged(gd_ref).astype(BF16)
    tri = tri_ref[...]

    qs, ks, bs, qhat, khat, b_ends = [], [], [], [], [], []
    for h in range(hq):
        lanes = slice(h * dk, (h + 1) * dk)
        x = _dot(gd, gup_ref[:, lanes]) + gbias_ref[:, lanes]
        log_a = -_softplus(-x) * (1.0 / GLA_GATE_NORM)
        b = _dot_split(tri, log_a * LOG2_E)
        q = merged(q_ref, lanes) * (dk ** -0.5)
        k = merged(k_ref, lanes)
        refs = [chunk_row(b, i * SUB - 1) for i in range(1, nsub)]
        bref = jnp.zeros((rows, dk), F32)
        for i in range(1, nsub):
            bref = jnp.where(sub_id == i, refs[i - 1], bref)
        qhat.append(per_row(q * jnp.exp2(b - bref)))
        khat.append([per_row(k * jnp.exp2(jnp.minimum(ref - b, 0.0))) for ref in refs])
        qs.append(q)
        ks.append(k)
        bs.append(b)
        b_ends.append(chunk_row(b, c - 1))

    attn = [jnp.concatenate(
        [jnp.concatenate(
            [jnp.zeros((SUB, c), F32)]
            + [_dot_nt(qhat[h][r][i * SUB:(i + 1) * SUB, :], khat[h][i - 1][r]) for i in range(1, nsub)],
            axis=0)
         for r in range(bb)], axis=0) * mask_off for h in range(hq)]

    for h in range(hq):
        q, k, b = qs[h], ks[h], bs[h]
        k3 = k.reshape(bb * nsub, SUB, dk)
        b3 = b.reshape(bb * nsub, SUB, dk)
        for j in range(SUB):
            kj = jnp.broadcast_to(k3[:, j:j + 1, :], (bb * nsub, SUB, dk)).reshape(rows, dk)
            bj = jnp.broadcast_to(b3[:, j:j + 1, :], (bb * nsub, SUB, dk)).reshape(rows, dk)
            col = jnp.sum(q * kj * jnp.exp2(jnp.minimum(b - bj, 0.0)), axis=-1, keepdims=True)
            attn[h] = jnp.where(dc == j, col, attn[h])

    chains = [(h, i) for h in range(hq) for i in range(bb)]
    attn_c = [t for h in range(hq) for t in per_row(attn[h])]
    qe = [t for h in range(hq) for t in per_row(qs[h] * jnp.exp2(bs[h]))]
    k_end = [t for h in range(hq) for t in per_row(ks[h] * jnp.exp2(b_ends[h] - bs[h]))]
    v_c = [v_ref[i, :, h * dv:(h + 1) * dv] for h, i in chains]
    st = [state_ref[i, h] for h, i in chains]
    o = [_dot(a_, v_) + _dot_nt(q_, s_) for a_, v_, q_, s_ in zip(attn_c, v_c, qe, st)]
    for j, (h, i) in enumerate(chains):
        state_ref[i, h] = st[j] * jnp.exp2(b_ends[h][i * c:i * c + 1, :]) + _dot_tn(v_c[j], k_end[j])

    for h in range(hq):
        lanes = slice(h * dv, (h + 1) * dv)
        oh = _rms(jnp.concatenate(o[h * bb:(h + 1) * bb], axis=0), onorm_ref[...])
        og = merged(og_ref, lanes)
        o_ref[:, :, lanes] = (oh * (og * _sigmoid(og))).astype(o_ref.dtype).reshape(bb, c, dv)


def _gla(proj, gate_up, gate_bias, out_norm, tri, cols, bb, heads, dk, dv):
    batch, seq, _ = proj.shape
    nc = seq // CHUNK
    hq = next(m for m in (GLA_HEADS_PER_STEP, 1) if heads % m == 0)
    kw, vw = hq * dk, hq * dv
    cq, ck, cv, cog, cgd = (cols[n] for n in ("gla_q", "gla_k", "gla_v", "gla_og", "gla_gd"))
    return pl.pallas_call(
        functools.partial(_gla_kernel, dk=dk, dv=dv, bb=bb, hq=hq),
        grid=(batch // bb, heads // hq, nc),
        in_specs=[
            pl.BlockSpec((bb, CHUNK, kw), lambda b, h, c: (b, c, cq // kw + h)),
            pl.BlockSpec((bb, CHUNK, kw), lambda b, h, c: (b, c, ck // kw + h)),
            pl.BlockSpec((bb, CHUNK, vw), lambda b, h, c: (b, c, cv // vw + h)),
            pl.BlockSpec((bb, CHUNK, vw), lambda b, h, c: (b, c, cog // vw + h)),
            pl.BlockSpec((bb, CHUNK, LANES), lambda b, h, c: (b, c, cgd // LANES)),
            pl.BlockSpec((LANES, kw), lambda b, h, c: (0, h)),
            pl.BlockSpec((1, kw), lambda b, h, c: (0, h)),
            pl.BlockSpec((1, dv), lambda b, h, c: (0, 0)),
            pl.BlockSpec((bb * CHUNK, bb * CHUNK), lambda b, h, c: (0, 0)),
        ],
        out_specs=pl.BlockSpec((bb, CHUNK, vw), lambda b, h, c: (b, c, h)),
        out_shape=jax.ShapeDtypeStruct((batch, seq, heads * dv), BF16),
        scratch_shapes=[pltpu.VMEM((bb, hq, dv, dk), F32)],
        compiler_params=_cparams(("parallel", "parallel", "arbitrary")),
    )(proj, proj, proj, proj, proj, gate_up, gate_bias, out_norm, tri)


def _rwkv_kernel(r_ref, k_ref, v_ref, gd_ref, wa_ref, rp_ref, kp_ref, vp_ref, gdp_ref, wap_ref,
                 mur_ref, muk_ref, muv_ref, mugd_ref, muwa_ref,
                 w0_ref, w2_ref, a0_ref, a2_ref, g2_ref, kk_ref, ka_ref, rk_ref, lnw_ref, lnb_ref,
                 tri_ref, ones_ref, mstrict_ref, mincl_ref, mblk_ref,
                 o_ref, state_ref, *, lora, bb, gq):
    c, g = CHUNK, RWKV_GROUP
    n, rows, gw = g * c, bb * c, g * RWKV_HEAD
    first = pl.program_id(2) == 0

    @pl.when(first)
    def _():
        state_ref[...] = jnp.zeros_like(state_ref)

    keep_prev = jnp.where(first, 0.0, 1.0)

    def per_row(x):
        return [x[i * c:(i + 1) * c] for i in range(bb)]

    def shifted(cur_ref, prev_ref, mu_ref, lanes):
        p = cur_ref[:, :, lanes].reshape(rows, -1)
        width = p.shape[-1]
        prev = jnp.concatenate(
            [jnp.broadcast_to(prev_ref[i, 7:8, lanes] * keep_prev, (c, width)) for i in range(bb)], axis=0)
        is_row0 = (lax.broadcasted_iota(jnp.int32, p.shape, 0) & (c - 1)) == 0
        p_prev = jnp.where(is_row0, prev, pltpu.roll(p, 1, axis=0))
        return p + (p_prev - p) * mu_ref[:, lanes]

    everything = slice(None)
    wa = shifted(wa_ref, wap_ref, muwa_ref, everything)
    tanh_wd = jnp.tanh(wa[:, :lora]).astype(BF16)
    a_down = wa[:, lora:].astype(BF16)
    sig_gd = _sigmoid(shifted(gd_ref, gdp_ref, mugd_ref, everything)).astype(BF16)

    ones_bd = ones_ref[...]
    m_blk = mblk_ref[...]
    m_blk_bf16 = m_blk.astype(BF16)
    tri = tri_ref[...]

    r_all, k2_all, v_all, gate_all = [], [], [], []
    rt, kkt, km, bm, k_end, b_end, v_rows, c_end = [], [], [], [], [], [], [], []
    for q in range(gq):
        lanes = slice(q * gw, (q + 1) * gw)
        r = shifted(r_ref, rp_ref, mur_ref, lanes)
        k = shifted(k_ref, kp_ref, muk_ref, lanes)
        v = shifted(v_ref, vp_ref, muv_ref, lanes)
        w = -_softplus(-(w0_ref[:, lanes] + _dot(tanh_wd, w2_ref[:, lanes]))) - 0.5
        ld = -jnp.exp(w)
        a = _sigmoid(a0_ref[:, lanes] + _dot(a_down, a2_ref[:, lanes]))
        gate_all.append(_dot(sig_gd, g2_ref[:, lanes]))

        kkr = k * kk_ref[:, lanes]
        kk = kkr * lax.rsqrt(jnp.maximum(_dot(kkr * kkr, ones_bd), 1e-24))
        k2 = k * (1.0 + (a - 1.0) * ka_ref[:, lanes])
        bw = a * kk

        cum = _dot_split(tri, ld)
        ends = [cum[(i + 1) * c - 1:(i + 1) * c, :] for i in range(bb)]
        c_end_rows = jnp.concatenate([jnp.broadcast_to(e, (c, gw)) for e in ends], axis=0)
        e_neg = jnp.exp(-cum)
        e_end = jnp.exp(c_end_rows - cum)
        c_end += ends
        rt += per_row(r * jnp.exp(cum))
        kkt += per_row(kk * jnp.exp(cum - ld))
        km += per_row(k2 * e_neg)
        bm += per_row(bw * e_neg)
        k_end += per_row(k2 * e_end)
        b_end += per_row(bw * e_end)
        v_rows += per_row(v)
        r_all.append(r)
        k2_all.append(k2)
        v_all.append(v)

    def stack(x):
        return jnp.concatenate([x.astype(BF16)] * g, axis=0) * m_blk_bf16

    m_strict, m_incl = mstrict_ref[...], mincl_ref[...]
    lhs = [jnp.concatenate([kq, rq], axis=0).astype(BF16) for kq, rq in zip(kkt, rt)]
    aa = [_dot_nt(l, jnp.concatenate([stack(b_), stack(k_)], axis=0))
          for l, b_, k_ in zip(lhs, bm, km)]
    a_ab = [t[:c, :n] * m_strict for t in aa]
    a_kr = [jnp.concatenate([t[:c, n:] * m_strict, t[c:, n:] * m_incl], axis=0) for t in aa]
    a_rb = [t[c:, :n] * m_incl for t in aa]

    eye = m_incl - m_strict
    x = [eye - t for t in a_ab]
    p = [_dot(t, stack(t)) for t in a_ab]
    span = 2
    while span < c:
        span *= 2
        if span < c:
            xp = [_dot(jnp.concatenate([xi, pi], axis=0), stack(pi)) for xi, pi in zip(x, p)]
            x = [xi + t[:c] for xi, t in zip(x, xp)]
            p = [t[c:] for t in xp]
        else:
            x = [xi + _dot(xi, stack(pi)) for xi, pi in zip(x, p)]

    chains = [(q, i) for q in range(gq) for i in range(bb)]
    st = [state_ref[i, q] for q, i in chains]
    sk = [_dot_nt(l, s) for l, s in zip(lhs, st)]
    av = [_dot(t, stack(vi)) for t, vi in zip(a_kr, v_rows)]
    u = [_dot(xi, stack(-(s[:c] + t[:c]))) for xi, s, t in zip(x, sk, av)]
    y = [s[c:] + _dot(t, stack(ui)) + w_[c:] for s, t, ui, w_ in zip(sk, a_rb, u, av)]
    for j, (q, i) in enumerate(chains):
        state_ref[i, q] = st[j] * jnp.exp(c_end[j]) + _dot_tn(
            jnp.concatenate([u[j], v_rows[j]], axis=0),
            jnp.concatenate([b_end[j], k_end[j]], axis=0)) * m_blk

    inv_n = 1.0 / RWKV_HEAD
    for q in range(gq):
        lanes = slice(q * gw, (q + 1) * gw)
        yq = jnp.concatenate(y[q * bb:(q + 1) * bb], axis=0)
        mu = _dot(yq, ones_bd) * inv_n
        dy = yq - mu
        var = _dot(dy * dy, ones_bd) * inv_n
        yn = dy * lax.rsqrt(var + RWKV_LN_EPS) * lnw_ref[:, lanes] + lnb_ref[:, lanes]
        bonus = _dot(r_all[q] * k2_all[q] * rk_ref[:, lanes], ones_bd) * v_all[q]
        o_ref[:, :, lanes] = ((yn + bonus) * gate_all[q]).astype(o_ref.dtype).reshape(bb, c, gw)


def _rwkv(proj, mus, w0, w2, a0, a2, g2, k_k, k_a, r_k, ln_w, ln_b, consts, cols, bb, width, lora):
    batch, seq, _ = proj.shape
    nc = seq // CHUNK
    gw = RWKV_GROUP * RWKV_HEAD
    gq = next(m for m in (RWKV_GROUPS_PER_STEP, 1) if (width // gw) % m == 0)
    sw = gq * gw
    steps = width // sw
    gdw = g2.shape[0]
    waw = 2 * lora

    def prev_rows(c):
        return jnp.maximum(c * (CHUNK // 8) - 1, 0)

    def cur(w_, col, per_group):
        if per_group:
            return pl.BlockSpec((bb, CHUNK, w_), lambda b, q, c: (b, c, col // w_ + q))
        return pl.BlockSpec((bb, CHUNK, w_), lambda b, q, c: (b, c, col // w_))

    def prev(w_, col, per_group):
        if per_group:
            return pl.BlockSpec((bb, 8, w_), lambda b, q, c: (b, prev_rows(c), col // w_ + q))
        return pl.BlockSpec((bb, 8, w_), lambda b, q, c: (b, prev_rows(c), col // w_))

    def gvec():
        return pl.BlockSpec((1, sw), lambda b, q, c: (0, q))

    def full(shape):
        return pl.BlockSpec(shape, lambda b, q, c: (0, 0))

    n = RWKV_GROUP * CHUNK
    in_specs = [
        cur(sw, cols["rw_r"], True), cur(sw, cols["rw_k"], True), cur(sw, cols["rw_v"], True),
        cur(gdw, cols["rw_gd"], False), cur(waw, cols["rw_wa"], False),
        prev(sw, cols["rw_r"], True), prev(sw, cols["rw_k"], True), prev(sw, cols["rw_v"], True),
        prev(gdw, cols["rw_gd"], False), prev(waw, cols["rw_wa"], False),
        gvec(), gvec(), gvec(), full((1, gdw)), full((1, waw)),
        gvec(), pl.BlockSpec((lora, sw), lambda b, q, c: (0, q)),
        gvec(), pl.BlockSpec((lora, sw), lambda b, q, c: (0, q)),
        pl.BlockSpec((gdw, sw), lambda b, q, c: (0, q)),
        gvec(), gvec(), gvec(), gvec(), gvec(),
        full((bb * CHUNK, bb * CHUNK)), full((gw, gw)), full((CHUNK, n)), full((CHUNK, n)), full((n, gw)),
    ]
    return pl.pallas_call(
        functools.partial(_rwkv_kernel, lora=lora, bb=bb, gq=gq),
        grid=(batch // bb, steps, nc),
        in_specs=in_specs,
        out_specs=pl.BlockSpec((bb, CHUNK, sw), lambda b, q, c: (b, c, q)),
        out_shape=jax.ShapeDtypeStruct((batch, seq, width), BF16),
        scratch_shapes=[pltpu.VMEM((bb, gq, gw, gw), F32)],
        compiler_params=_cparams(("parallel", "parallel", "arbitrary")),
    )(proj, proj, proj, proj, proj, proj, proj, proj, proj, proj,
      mus["r"], mus["k"], mus["v"], mus["gd"], mus["wa"],
      w0, w2, a0, a2, g2, k_k, k_a, r_k, ln_w, ln_b,
      consts["tri_rows"], consts["ones_bd"], consts["m_strict"], consts["m_incl"], consts["m_blk"])


def _pad_rows(w, n):
    return jnp.pad(w, ((0, n - w.shape[0]), (0, 0)))


def _layout(d_model):
    kw, vw, rw = d_model // 4, d_model // 2, d_model // 2
    lora = 128
    gdw = 512
    names = [("gla_q", kw), ("gla_k", kw), ("gla_v", vw), ("gla_og", vw),
             ("rw_r", rw), ("rw_k", rw), ("rw_v", rw),
             ("gate_gla", d_model), ("gate_rwkv", d_model),
             ("rw_gd", gdw), ("rw_wa", 2 * lora), ("gla_gd", LANES)]
    cols, off = {}, 0
    for name, w in names:
        assert off % w == 0, (name, off, w)
        cols[name] = off
        off += w
    return cols, off


def _chunk_consts(bb):
    c, g, hn = CHUNK, RWKV_GROUP, RWKV_HEAD
    assert c == hn
    n = g * c
    i = jnp.arange(n)
    ti, tj = jnp.arange(c)[:, None], i[None, :] % c
    lane = jnp.arange(g * hn)
    return {
        "tri_rows": jnp.kron(jnp.eye(bb, dtype=F32), jnp.tril(jnp.ones((c, c), F32))).astype(BF16),
        "ones_bd": ((lane[:, None] // hn) == (lane[None, :] // hn)).astype(BF16),
        "m_strict": (ti > tj).astype(F32),
        "m_incl": (ti >= tj).astype(F32),
        "m_blk": ((i[:, None] // c) == (lane[None, :] // hn)).astype(F32),
    }


def kernel(x, ffn1_pre_norm, ffn1_w_gate, ffn1_w_up, ffn1_w_down, ffn1_post_norm, mix_pre_norm, w_in, gla_gate_up, gla_gate_bias, gla_out_norm, rwkv_shift_mix, rwkv_w0, rwkv_w2, rwkv_a0, rwkv_a2, rwkv_g2, rwkv_k_k, rwkv_k_a, rwkv_r_k, rwkv_ln_w, rwkv_ln_b, w_up_gla, w_up_rwkv, w_out, mix_post_norm, ffn2_pre_norm, ffn2_w_gate, ffn2_w_up, ffn2_w_down, ffn2_post_norm):
    batch, seq, d_model = x.shape
    depth = ffn1_pre_norm.shape[0]
    t = batch * seq
    assert seq % CHUNK == 0 and d_model % 1024 == 0

    gla_kw, gla_vw, rw_w = d_model // 4, d_model // 2, d_model // 2
    gla_heads = max(4, d_model // 512)
    dk, dv = gla_kw // gla_heads, gla_vw // gla_heads
    lora = rwkv_w2.shape[1]
    g_lora = rwkv_g2.shape[1]
    assert lora == 128 and g_lora <= 512 and dk % LANES == 0

    cols, n_proj = _layout(d_model)
    n_proj_pad = -(-n_proj // 1024) * 1024
    bm = min(1024, t)
    bm_down = min(512, t)
    bm_gu = min(2048, t)
    bt = min(256, t)
    bb = next(n for n in (4, 2, 1) if batch % n == 0)
    consts = _chunk_consts(bb)

    def row(v):
        return v.reshape(1, -1).astype(F32)

    h = x.reshape(t, d_model)
    for l in range(depth):
        xn = _norm_cast(h, row(ffn1_pre_norm[l]), bt)
        hid, w_down = _gate_up(xn, ffn1_w_gate[l], ffn1_w_up[l], ffn1_w_down[l], bm_gu, 256)
        f = _matmul(hid, w_down, BF16, bm_down, 512)
        h, u = _post_pre(h, f, row(ffn1_post_norm[l]), row(mix_pre_norm[l]), MACARON_WEIGHT, bt)

        o_rw = 2 * gla_kw + 2 * gla_vw + GLA_GATE_RANK
        wi = jnp.swapaxes(w_in[l], 0, 1)
        gq, gk, gv, ggd, gog = jnp.split(wi[:o_rw], [gla_kw, 2 * gla_kw, 2 * gla_kw + gla_vw,
                                                    2 * gla_kw + gla_vw + GLA_GATE_RANK], axis=0)
        rw_sizes = [rw_w, lora, rw_w, rw_w, lora, g_lora]
        rw_off = [o_rw]
        for s in rw_sizes:
            rw_off.append(rw_off[-1] + s)
        rr, rwd, rk, rv, rad, rgd = (wi[rw_off[i]:rw_off[i + 1]] for i in range(6))
        gates = wi[rw_off[-1]:]
        w_proj_t = jnp.concatenate(
            [piece.astype(BF16) for piece in
             (gq, gk, gv, gog, rr, rk, rv, gates, _pad_rows(rgd, 512), rwd, rad, _pad_rows(ggd, LANES),
              jnp.zeros((n_proj_pad - n_proj, d_model), F32))], axis=0)
        proj = _matmul_nt(u, w_proj_t, F32, bm, 1024)

        mu = rwkv_shift_mix[l]
        mu_off = [o - o_rw for o in rw_off]
        mu_r, mu_wd, mu_k, mu_v, mu_ad, mu_gd = (mu[mu_off[i]:mu_off[i + 1]] for i in range(6))
        mus = {"r": row(mu_r), "k": row(mu_k), "v": row(mu_v),
               "gd": row(jnp.pad(mu_gd, (0, 512 - g_lora))),
               "wa": row(jnp.concatenate([mu_wd, mu_ad]))}

        proj3 = proj.reshape(batch, seq, n_proj_pad)
        o_gla = _gla(proj3, _pad_rows(gla_gate_up[l], LANES).astype(BF16), row(gla_gate_bias[l]),
                     row(gla_out_norm[l]), consts["tri_rows"], cols, bb, gla_heads, dk, dv)
        o_rwkv = _rwkv(proj3, mus, row(rwkv_w0[l]), rwkv_w2[l].astype(BF16), row(rwkv_a0[l]),
                       rwkv_a2[l].astype(BF16), _pad_rows(rwkv_g2[l], 512).astype(BF16),
                       row(rwkv_k_k[l]), row(rwkv_k_a[l]), row(rwkv_r_k[l]), row(rwkv_ln_w[l]),
                       row(rwkv_ln_b[l]), consts, cols, bb, rw_w, lora)

        merged = _merge(o_gla.reshape(t, gla_vw), o_rwkv.reshape(t, rw_w),
                        w_up_gla[l].astype(BF16), w_up_rwkv[l].astype(BF16), proj,
                        cols["gate_gla"], cols["gate_rwkv"], bm, 512)
        mixed = _matmul(merged, w_out[l].astype(BF16), BF16, bm, 1024)
        h, u = _post_pre(h, mixed, row(mix_post_norm[l]), row(ffn2_pre_norm[l]), 1.0, bt)

        hid, w_down = _gate_up(u, ffn2_w_gate[l], ffn2_w_up[l], ffn2_w_down[l], bm_gu, 256)
        f = _matmul(hid, w_down, BF16, bm_down, 512)
        h = _post(h, f, row(ffn2_post_norm[l]), MACARON_WEIGHT, bt)
    return h.reshape(batch, seq, d_model)
```

```python
import functools

import jax
import jax.numpy as jnp
from jax import lax
from jax.experimental import pallas as pl
from jax.experimental.pallas import tpu as pltpu

F32 = jnp.float32
BF16 = jnp.bfloat16

NORM_EPS = 1e-6
RWKV_LN_EPS = 64e-5
MACARON_WEIGHT = 0.5
GLA_GATE_NORM = 16.0
GLA_GATE_RANK = 16
LOG2_E = 1.4426950408889634
RWKV_HEAD = 64

LANES = 128
CHUNK = 64
PROJ_BN = 256
ROW_ALIGN = 16
SUB_LOG2 = 3
SUB = 1 << SUB_LOG2
GLA_HEADS_PER_STEP = 2
RWKV_GROUP = 4
RWKV_GROUPS_PER_STEP = 4
VMEM_LIMIT = 56 * 1024 * 1024
VMEM_LIMIT_MAX = 60 * 1024 * 1024


def _cparams(sem):
    return pltpu.CompilerParams(dimension_semantics=sem, vmem_limit_bytes=VMEM_LIMIT)


def _dot(a, b):
    return jnp.dot(a.astype(BF16), b.astype(BF16), preferred_element_type=F32)


def _dot_nt(a, b):
    return lax.dot_general(a.astype(BF16), b.astype(BF16), (((1,), (1,)), ((), ())),
                           preferred_element_type=F32)


def _dot_tn(a, b):
    return lax.dot_general(a.astype(BF16), b.astype(BF16), (((0,), (0,)), ((), ())),
                           preferred_element_type=F32)


def _dot_split(m01, x):
    hi = x.astype(BF16)
    lo = (x - hi.astype(F32)).astype(BF16)
    return (jnp.dot(m01, hi, preferred_element_type=F32)
            + jnp.dot(m01, lo, preferred_element_type=F32))


def _rms(x, g):
    ms = jnp.mean(x * x, axis=-1, keepdims=True)
    return x * lax.rsqrt(ms + NORM_EPS) * g


def _sigmoid(x):
    return 1.0 / (1.0 + jnp.exp(-x))


def _softplus(x):
    return jnp.maximum(x, 0.0) + jnp.log(1.0 + jnp.exp(-jnp.abs(x)))


def _norm_cast_kernel(x_ref, g_ref, o_ref):
    o_ref[...] = _rms(x_ref[...], g_ref[...]).astype(o_ref.dtype)


def _norm_cast(x, g, bt):
    t, d = x.shape
    return pl.pallas_call(
        _norm_cast_kernel,
        grid=(t // bt,),
        in_specs=[pl.BlockSpec((bt, d), lambda i: (i, 0)),
                  pl.BlockSpec((1, d), lambda i: (0, 0))],
        out_specs=pl.BlockSpec((bt, d), lambda i: (i, 0)),
        out_shape=jax.ShapeDtypeStruct((t, d), BF16),
        compiler_params=_cparams(("parallel",)),
    )(x, g)


def _post_pre_kernel(h_ref, f_ref, gpost_ref, gpre_ref, hout_ref, u_ref, *, alpha):
    h = h_ref[...] + alpha * _rms(f_ref[...].astype(F32), gpost_ref[...])
    hout_ref[...] = h
    u_ref[...] = _rms(h, gpre_ref[...]).astype(u_ref.dtype)


def _post_pre(h, f, g_post, g_pre, alpha, bt):
    t, d = h.shape
    row = pl.BlockSpec((bt, d), lambda i: (i, 0))
    vec = pl.BlockSpec((1, d), lambda i: (0, 0))
    return pl.pallas_call(
        functools.partial(_post_pre_kernel, alpha=alpha),
        grid=(t // bt,),
        in_specs=[row, row, vec, vec],
        out_specs=[row, row],
        out_shape=[jax.ShapeDtypeStruct((t, d), F32), jax.ShapeDtypeStruct((t, d), BF16)],
        compiler_params=_cparams(("parallel",)),
    )(h, f, g_post, g_pre)


def _post_kernel(h_ref, f_ref, gpost_ref, hout_ref, *, alpha):
    hout_ref[...] = h_ref[...] + alpha * _rms(f_ref[...].astype(F32), gpost_ref[...])


def _post(h, f, g_post, alpha, bt):
    t, d = h.shape
    row = pl.BlockSpec((bt, d), lambda i: (i, 0))
    vec = pl.BlockSpec((1, d), lambda i: (0, 0))
    return pl.pallas_call(
        functools.partial(_post_kernel, alpha=alpha),
        grid=(t // bt,),
        in_specs=[row, row, vec],
        out_specs=row,
        out_shape=jax.ShapeDtypeStruct((t, d), F32),
        compiler_params=_cparams(("parallel",)),
    )(h, f, g_post)


def _matmul_kernel(x_ref, w_ref, o_ref):
    o_ref[...] = jnp.dot(x_ref[...], w_ref[...], preferred_element_type=F32).astype(o_ref.dtype)


def _proj_kernel(rows_ref, x_ref, w_ref, o_ref):
    del rows_ref
    o_ref[...] = lax.dot_general(x_ref[...], w_ref[...].astype(BF16), (((1,), (1,)), ((), ())),
                                 preferred_element_type=F32).astype(o_ref.dtype)


def _proj(x, w_t, row_starts, out_dtype, bm, bn):
    m, k = x.shape
    nblk = row_starts.shape[0]
    return pl.pallas_call(
        _proj_kernel,
        grid_spec=pltpu.PrefetchScalarGridSpec(
            num_scalar_prefetch=1, grid=(m // bm, nblk),
            in_specs=[pl.BlockSpec((bm, k), lambda i, j, rows: (i, 0)),
                      pl.BlockSpec((pl.Element(bn), pl.Element(k)),
                                   lambda i, j, rows: (pl.multiple_of(rows[j], ROW_ALIGN), 0))],
            out_specs=pl.BlockSpec((bm, bn), lambda i, j, rows: (i, j))),
        out_shape=jax.ShapeDtypeStruct((m, nblk * bn), out_dtype),
        compiler_params=_cparams(("parallel", "arbitrary")),
    )(row_starts, x, w_t)


def _matmul(x, w, out_dtype, bm, bn):
    m, k = x.shape
    n = w.shape[1]
    return pl.pallas_call(
        _matmul_kernel,
        grid=(m // bm, pl.cdiv(n, bn)),
        in_specs=[pl.BlockSpec((bm, k), lambda i, j: (i, 0)),
                  pl.BlockSpec((k, bn), lambda i, j: (0, j))],
        out_specs=pl.BlockSpec((bm, bn), lambda i, j: (i, j)),
        out_shape=jax.ShapeDtypeStruct((m, n), out_dtype),
        compiler_params=_cparams(("parallel", "arbitrary")),
    )(x, w)


def _gate_up_kernel(x_ref, wg_ref, wu_ref, wd_ref, o_ref, wd_bf16_ref):
    x = x_ref[...]
    g = jnp.dot(x, wg_ref[...].astype(BF16), preferred_element_type=F32)
    u = jnp.dot(x, wu_ref[...].astype(BF16), preferred_element_type=F32)
    o_ref[...] = (g * _sigmoid(g) * u).astype(o_ref.dtype)
    wd_bf16_ref[...] = wd_ref[...].astype(BF16)


def _gate_up(x, wg, wu, wd, bm, bn):
    m, k = x.shape
    n = wg.shape[1]
    nj = pl.cdiv(n, bn)
    steps = (m // bm) * nj
    slab = wd.shape[0] // steps
    assert slab * steps == wd.shape[0] and slab % 16 == 0, (wd.shape, steps)
    wspec = pl.BlockSpec((k, bn), lambda i, j: (0, j))
    slab_spec = pl.BlockSpec((slab, wd.shape[1]), lambda i, j: (i * nj + j, 0))
    return pl.pallas_call(
        _gate_up_kernel,
        grid=(m // bm, nj),
        in_specs=[pl.BlockSpec((bm, k), lambda i, j: (i, 0)), wspec, wspec, slab_spec],
        out_specs=[pl.BlockSpec((bm, bn), lambda i, j: (i, j)), slab_spec],
        out_shape=[jax.ShapeDtypeStruct((m, n), BF16), jax.ShapeDtypeStruct(wd.shape, BF16)],
        compiler_params=pltpu.CompilerParams(dimension_semantics=("arbitrary", "arbitrary"),
                                             vmem_limit_bytes=VMEM_LIMIT_MAX),
    )(x, wg, wu, wd)


def _merge_kernel(og_ref, or_ref, wg_ref, wr_ref, gg_ref, gr_ref, o_ref):
    yg = jnp.dot(og_ref[...], wg_ref[...], preferred_element_type=F32)
    yr = jnp.dot(or_ref[...], wr_ref[...], preferred_element_type=F32)
    o_ref[...] = (_sigmoid(gg_ref[...]) * yg + _sigmoid(gr_ref[...]) * yr).astype(o_ref.dtype)


def _merge(o_gla, o_rwkv, w_up_gla, w_up_rwkv, proj, col_gate_gla, col_gate_rwkv, bm, bn):
    m, k = o_gla.shape
    n = w_up_gla.shape[1]
    xspec = pl.BlockSpec((bm, k), lambda i, j: (i, 0))
    wspec = pl.BlockSpec((k, bn), lambda i, j: (0, j))
    ga, gb = col_gate_gla // bn, col_gate_rwkv // bn
    return pl.pallas_call(
        _merge_kernel,
        grid=(m // bm, n // bn),
        in_specs=[xspec, xspec, wspec, wspec,
                  pl.BlockSpec((bm, bn), lambda i, j: (i, ga + j)),
                  pl.BlockSpec((bm, bn), lambda i, j: (i, gb + j))],
        out_specs=pl.BlockSpec((bm, bn), lambda i, j: (i, j)),
        out_shape=jax.ShapeDtypeStruct((m, n), BF16),
        compiler_params=_cparams(("parallel", "arbitrary")),
    )(o_gla, o_rwkv, w_up_gla, w_up_rwkv, proj, proj)


def _gla_kernel(q_ref, k_ref, v_ref, og_ref, gd_ref, gup_ref, gbias_ref, onorm_ref, tri_ref,
                o_ref, state_ref, *, dk, dv, bb, hq):
    c, nsub = CHUNK, CHUNK // SUB
    rows = bb * c

    @pl.when(pl.program_id(2) == 0)
    def _():
        state_ref[...] = jnp.zeros_like(state_ref)

    def merged(ref, lanes=slice(None)):
        return ref[:, :, lanes].reshape(rows, -1)

    def per_row(x):
        return [x[i * c:(i + 1) * c] for i in range(bb)]

    def chunk_row(x, r):
        return jnp.concatenate(
            [jnp.broadcast_to(x[i * c + r:i * c + r + 1, :], (c, x.shape[-1])) for i in range(bb)], axis=0)

    row = lax.broadcasted_iota(jnp.int32, (rows, c), 0) & (c - 1)
    lane = lax.broadcasted_iota(jnp.int32, (rows, c), 1)
    blk0 = row & -SUB
    d = lane - blk0
    dc = jnp.where(d >= 0, jnp.where(d <= (row & (SUB - 1)), d, -1), -1)
    mask_off = (lane < blk0).astype(F32)
    sub_id = (lax.broadcasted_iota(jnp.int32, (rows, dk), 0) & (c - 1)) >> SUB_LOG2
    gd = merged(gd_ref).astype(BF16)
    tri = tri_ref[...]

    qs, ks, bs, qhat, khat, b_ends = [], [], [], [], [], []
    for h in range(hq):
        lanes = slice(h * dk, (h + 1) * dk)
        x = _dot(gd, gup_ref[:, lanes]) + gbias_ref[:, lanes]
        log_a = -_softplus(-x) * (1.0 / GLA_GATE_NORM)
        b = _dot_split(tri, log_a * LOG2_E)
        q = merged(q_ref, lanes) * (dk ** -0.5)
        k = merged(k_ref, lanes)
        refs = [chunk_row(b, i * SUB - 1) for i in range(1, nsub)]
        bref = jnp.zeros((rows, dk), F32)
        for i in range(1, nsub):
            bref = jnp.where(sub_id == i, refs[i - 1], bref)
        qhat.append(per_row(q * jnp.exp2(b - bref)))
        khat.append([per_row(k * jnp.exp2(jnp.minimum(ref - b, 0.0))) for ref in refs])
        qs.append(q)
        ks.append(k)
        bs.append(b)
        b_ends.append(chunk_row(b, c - 1))

    attn = [jnp.concatenate(
        [jnp.concatenate(
            [jnp.zeros((SUB, c), F32)]
            + [_dot_nt(qhat[h][r][i * SUB:(i + 1) * SUB, :], khat[h][i - 1][r]) for i in range(1, nsub)],
            axis=0)
         for r in range(bb)], axis=0) * mask_off for h in range(hq)]

    for h in range(hq):
        q, k, b = qs[h], ks[h], bs[h]
        k3 = k.reshape(bb * nsub, SUB, dk)
        b3 = b.reshape(bb * nsub, SUB, dk)
        for j in range(SUB):
            kj = jnp.broadcast_to(k3[:, j:j + 1, :], (bb * nsub, SUB, dk)).reshape(rows, dk)
            bj = jnp.broadcast_to(b3[:, j:j + 1, :], (bb * nsub, SUB, dk)).reshape(rows, dk)
            col = jnp.sum(q * kj * jnp.exp2(jnp.minimum(b - bj, 0.0)), axis=-1, keepdims=True)
            attn[h] = jnp.where(dc == j, col, attn[h])

    chains = [(h, i) for h in range(hq) for i in range(bb)]
    attn_c = [t for h in range(hq) for t in per_row(attn[h])]
    qe = [t for h in range(hq) for t in per_row(qs[h] * jnp.exp2(bs[h]))]
    k_end = [t for h in range(hq) for t in per_row(ks[h] * jnp.exp2(b_ends[h] - bs[h]))]
    v_c = [v_ref[i, :, h * dv:(h + 1) * dv] for h, i in chains]
    st = [state_ref[i, h] for h, i in chains]
    o = [_dot(a_, v_) + _dot_nt(q_, s_) for a_, v_, q_, s_ in zip(attn_c, v_c, qe, st)]
    for j, (h, i) in enumerate(chains):
        state_ref[i, h] = st[j] * jnp.exp2(b_ends[h][i * c:i * c + 1, :]) + _dot_tn(v_c[j], k_end[j])

    for h in range(hq):
        lanes = slice(h * dv, (h + 1) * dv)
        oh = _rms(jnp.concatenate(o[h * bb:(h + 1) * bb], axis=0), onorm_ref[...])
        og = merged(og_ref, lanes)
        o_ref[:, :, lanes] = (oh * (og * _sigmoid(og))).astype(o_ref.dtype).reshape(bb, c, dv)


def _gla(proj, gate_up, gate_bias, out_norm, tri, cols, bb, heads, dk, dv):
    batch, seq, _ = proj.shape
    nc = seq // CHUNK
    hq = next(m for m in (GLA_HEADS_PER_STEP, 1) if heads % m == 0)
    kw, vw = hq * dk, hq * dv
    cq, ck, cv, cog, cgd = (cols[n] for n in ("gla_q", "gla_k", "gla_v", "gla_og", "gla_gd"))
    return pl.pallas_call(
        functools.partial(_gla_kernel, dk=dk, dv=dv, bb=bb, hq=hq),
        grid=(batch // bb, heads // hq, nc),
        in_specs=[
            pl.BlockSpec((bb, CHUNK, kw), lambda b, h, c: (b, c, cq // kw + h)),
            pl.BlockSpec((bb, CHUNK, kw), lambda b, h, c: (b, c, ck // kw + h)),
            pl.BlockSpec((bb, CHUNK, vw), lambda b, h, c: (b, c, cv // vw + h)),
            pl.BlockSpec((bb, CHUNK, vw), lambda b, h, c: (b, c, cog // vw + h)),
            pl.BlockSpec((bb, CHUNK, LANES), lambda b, h, c: (b, c, cgd // LANES)),
            pl.BlockSpec((LANES, kw), lambda b, h, c: (0, h)),
            pl.BlockSpec((1, kw), lambda b, h, c: (0, h)),
            pl.BlockSpec((1, dv), lambda b, h, c: (0, 0)),
            pl.BlockSpec((bb * CHUNK, bb * CHUNK), lambda b, h, c: (0, 0)),
        ],
        out_specs=pl.BlockSpec((bb, CHUNK, vw), lambda b, h, c: (b, c, h)),
        out_shape=jax.ShapeDtypeStruct((batch, seq, heads * dv), BF16),
        scratch_shapes=[pltpu.VMEM((bb, hq, dv, dk), F32)],
        compiler_params=_cparams(("parallel", "parallel", "arbitrary")),
    )(proj, proj, proj, proj, proj, gate_up, gate_bias, out_norm, tri)


def _rwkv_kernel(r_ref, k_ref, v_ref, gd_ref, wd_ref, ad_ref, rp_ref, kp_ref, vp_ref, gdp_ref, wdp_ref, adp_ref,
                 mur_ref, muk_ref, muv_ref, mugd_ref, muwd_ref, muad_ref,
                 w0_ref, w2_ref, a0_ref, a2_ref, g2_ref, kk_ref, ka_ref, rk_ref, lnw_ref, lnb_ref,
                 tri_ref, ones_ref, mstrict_ref, mincl_ref, mblk_ref,
                 o_ref, state_ref, *, bb, gq):
    c, g = CHUNK, RWKV_GROUP
    n, rows, gw = g * c, bb * c, g * RWKV_HEAD
    first = pl.program_id(2) == 0

    @pl.when(first)
    def _():
        state_ref[...] = jnp.zeros_like(state_ref)

    keep_prev = jnp.where(first, 0.0, 1.0)

    def per_row(x):
        return [x[i * c:(i + 1) * c] for i in range(bb)]

    def shifted(cur_ref, prev_ref, mu_ref, lanes):
        p = cur_ref[:, :, lanes].reshape(rows, -1)
        width = p.shape[-1]
        prev = jnp.concatenate(
            [jnp.broadcast_to(prev_ref[i, 7:8, lanes] * keep_prev, (c, width)) for i in range(bb)], axis=0)
        is_row0 = (lax.broadcasted_iota(jnp.int32, p.shape, 0) & (c - 1)) == 0
        p_prev = jnp.where(is_row0, prev, pltpu.roll(p, 1, axis=0))
        return p + (p_prev - p) * mu_ref[:, lanes]

    everything = slice(None)
    tanh_wd = jnp.tanh(shifted(wd_ref, wdp_ref, muwd_ref, everything)).astype(BF16)
    a_down = shifted(ad_ref, adp_ref, muad_ref, everything).astype(BF16)
    sig_gd = _sigmoid(shifted(gd_ref, gdp_ref, mugd_ref, everything)).astype(BF16)

    ones_bd = ones_ref[...]
    m_blk = mblk_ref[...]
    m_blk_bf16 = m_blk.astype(BF16)
    tri = tri_ref[...]

    r_all, k2_all, v_all, gate_all = [], [], [], []
    rt, kkt, km, bm, k_end, b_end, v_rows, c_end = [], [], [], [], [], [], [], []
    for q in range(gq):
        lanes = slice(q * gw, (q + 1) * gw)
        r = shifted(r_ref, rp_ref, mur_ref, lanes)
        k = shifted(k_ref, kp_ref, muk_ref, lanes)
        v = shifted(v_ref, vp_ref, muv_ref, lanes)
        w = -_softplus(-(w0_ref[:, lanes] + _dot(tanh_wd, w2_ref[:, lanes]))) - 0.5
        ld = -jnp.exp(w)
        a = _sigmoid(a0_ref[:, lanes] + _dot(a_down, a2_ref[:, lanes]))
        gate_all.append(_dot(sig_gd, g2_ref[:, lanes]))

        kkr = k * kk_ref[:, lanes]
        kk = kkr * lax.rsqrt(jnp.maximum(_dot(kkr * kkr, ones_bd), 1e-24))
        k2 = k * (1.0 + (a - 1.0) * ka_ref[:, lanes])
        bw = a * kk

        cum = _dot_split(tri, ld)
        ends = [cum[(i + 1) * c - 1:(i + 1) * c, :] for i in range(bb)]
        c_end_rows = jnp.concatenate([jnp.broadcast_to(e, (c, gw)) for e in ends], axis=0)
        e_neg = jnp.exp(-cum)
        e_end = jnp.exp(c_end_rows - cum)
        c_end += ends
        rt += per_row(r * jnp.exp(cum))
        kkt += per_row(kk * jnp.exp(cum - ld))
        km += per_row(k2 * e_neg)
        bm += per_row(bw * e_neg)
        k_end += per_row(k2 * e_end)
        b_end += per_row(bw * e_end)
        v_rows += per_row(v)
        r_all.append(r)
        k2_all.append(k2)
        v_all.append(v)

    def stack(x):
        return jnp.concatenate([x.astype(BF16)] * g, axis=0) * m_blk_bf16

    m_strict, m_incl = mstrict_ref[...], mincl_ref[...]
    lhs = [jnp.concatenate([kq, rq], axis=0).astype(BF16) for kq, rq in zip(kkt, rt)]
    aa = [_dot_nt(l, jnp.concatenate([stack(b_), stack(k_)], axis=0))
          for l, b_, k_ in zip(lhs, bm, km)]
    a_ab = [t[:c, :n] * m_strict for t in aa]
    a_kr = [jnp.concatenate([t[:c, n:] * m_strict, t[c:, n:] * m_incl], axis=0) for t in aa]
    a_rb = [t[c:, :n] * m_incl for t in aa]

    eye = m_incl - m_strict
    x = [eye - t for t in a_ab]
    p = [_dot(t, stack(t)) for t in a_ab]
    span = 2
    while span < c:
        span *= 2
        if span < c:
            xp = [_dot(jnp.concatenate([xi, pi], axis=0), stack(pi)) for xi, pi in zip(x, p)]
            x = [xi + t[:c] for xi, t in zip(x, xp)]
            p = [t[c:] for t in xp]
        else:
            x = [xi + _dot(xi, stack(pi)) for xi, pi in zip(x, p)]

    chains = [(q, i) for q in range(gq) for i in range(bb)]
    st = [state_ref[i, q] for q, i in chains]
    sk = [_dot_nt(l, s) for l, s in zip(lhs, st)]
    av = [_dot(t, stack(vi)) for t, vi in zip(a_kr, v_rows)]
    u = [_dot(xi, stack(-(s[:c] + t[:c]))) for xi, s, t in zip(x, sk, av)]
    y = [s[c:] + _dot(t, stack(ui)) + w_[c:] for s, t, ui, w_ in zip(sk, a_rb, u, av)]
    for j, (q, i) in enumerate(chains):
        state_ref[i, q] = st[j] * jnp.exp(c_end[j]) + _dot_tn(
            jnp.concatenate([u[j], v_rows[j]], axis=0),
            jnp.concatenate([b_end[j], k_end[j]], axis=0)) * m_blk

    inv_n = 1.0 / RWKV_HEAD
    for q in range(gq):
        lanes = slice(q * gw, (q + 1) * gw)
        yq = jnp.concatenate(y[q * bb:(q + 1) * bb], axis=0)
        mu = _dot(yq, ones_bd) * inv_n
        dy = yq - mu
        var = _dot(dy * dy, ones_bd) * inv_n
        yn = dy * lax.rsqrt(var + RWKV_LN_EPS) * lnw_ref[:, lanes] + lnb_ref[:, lanes]
        bonus = _dot(r_all[q] * k2_all[q] * rk_ref[:, lanes], ones_bd) * v_all[q]
        o_ref[:, :, lanes] = ((yn + bonus) * gate_all[q]).astype(o_ref.dtype).reshape(bb, c, gw)


def _rwkv(proj, mus, w0, w2, a0, a2, g2, k_k, k_a, r_k, ln_w, ln_b, consts, cols, bb, width, lora):
    batch, seq, _ = proj.shape
    nc = seq // CHUNK
    gw = RWKV_GROUP * RWKV_HEAD
    gq = next(m for m in (RWKV_GROUPS_PER_STEP, 1) if (width // gw) % m == 0)
    sw = gq * gw
    steps = width // sw
    gdw = g2.shape[0]

    def prev_rows(c):
        return jnp.maximum(c * (CHUNK // 8) - 1, 0)

    def cur(w_, col, per_group):
        if per_group:
            return pl.BlockSpec((bb, CHUNK, w_), lambda b, q, c: (b, c, col // w_ + q))
        return pl.BlockSpec((bb, CHUNK, w_), lambda b, q, c: (b, c, col // w_))

    def prev(w_, col, per_group):
        if per_group:
            return pl.BlockSpec((bb, 8, w_), lambda b, q, c: (b, prev_rows(c), col // w_ + q))
        return pl.BlockSpec((bb, 8, w_), lambda b, q, c: (b, prev_rows(c), col // w_))

    def gvec():
        return pl.BlockSpec((1, sw), lambda b, q, c: (0, q))

    def full(shape):
        return pl.BlockSpec(shape, lambda b, q, c: (0, 0))

    n = RWKV_GROUP * CHUNK
    in_specs = [
        cur(sw, cols["rw_r"], True), cur(sw, cols["rw_k"], True), cur(sw, cols["rw_v"], True),
        cur(gdw, cols["rw_gd"], False), cur(lora, cols["rw_wd"], False), cur(lora, cols["rw_ad"], False),
        prev(sw, cols["rw_r"], True), prev(sw, cols["rw_k"], True), prev(sw, cols["rw_v"], True),
        prev(gdw, cols["rw_gd"], False), prev(lora, cols["rw_wd"], False), prev(lora, cols["rw_ad"], False),
        gvec(), gvec(), gvec(), full((1, gdw)), full((1, lora)), full((1, lora)),
        gvec(), pl.BlockSpec((lora, sw), lambda b, q, c: (0, q)),
        gvec(), pl.BlockSpec((lora, sw), lambda b, q, c: (0, q)),
        pl.BlockSpec((gdw, sw), lambda b, q, c: (0, q)),
        gvec(), gvec(), gvec(), gvec(), gvec(),
        full((bb * CHUNK, bb * CHUNK)), full((gw, gw)), full((CHUNK, n)), full((CHUNK, n)), full((n, gw)),
    ]
    return pl.pallas_call(
        functools.partial(_rwkv_kernel, bb=bb, gq=gq),
        grid=(batch // bb, steps, nc),
        in_specs=in_specs,
        out_specs=pl.BlockSpec((bb, CHUNK, sw), lambda b, q, c: (b, c, q)),
        out_shape=jax.ShapeDtypeStruct((batch, seq, width), BF16),
        scratch_shapes=[pltpu.VMEM((bb, gq, gw, gw), F32)],
        compiler_params=_cparams(("parallel", "parallel", "arbitrary")),
    )(*([proj] * 12),
      mus["r"], mus["k"], mus["v"], mus["gd"], mus["wd"], mus["ad"],
      w0, w2, a0, a2, g2, k_k, k_a, r_k, ln_w, ln_b,
      consts["tri_rows"], consts["ones_bd"], consts["m_strict"], consts["m_incl"], consts["m_blk"])


def _pad_rows(w, n):
    return jnp.pad(w, ((0, n - w.shape[0]), (0, 0)))


def _layout(d_model, lora, g_lora):
    kw, vw, rw = d_model // 4, d_model // 2, d_model // 2
    o_rw = 2 * kw + 2 * vw + GLA_GATE_RANK
    o_gate = o_rw + 3 * rw + 2 * lora + g_lora
    segments = [("gla_q", kw, 0), ("gla_k", kw, kw), ("gla_v", vw, 2 * kw),
                ("gla_og", vw, 2 * kw + vw + GLA_GATE_RANK),
                ("rw_r", rw, o_rw), ("rw_k", rw, o_rw + rw + lora), ("rw_v", rw, o_rw + 2 * rw + lora),
                ("gate_gla", d_model, o_gate), ("gate_rwkv", d_model, o_gate + d_model),
                ("rw_gd", 2 * PROJ_BN, o_rw + 3 * rw + 2 * lora),
                ("rw_wd", PROJ_BN, o_rw + rw), ("rw_ad", PROJ_BN, o_rw + 3 * rw + lora),
                ("gla_gd", PROJ_BN, 2 * kw + vw)]
    n_src = o_gate + 2 * d_model
    cols, row_starts, off = {}, [], 0
    for name, w, src in segments:
        assert off % w == 0 and w % PROJ_BN == 0 and src % ROW_ALIGN == 0 and src + w <= n_src, (name, off, w, src)
        cols[name] = off
        row_starts += [src + b for b in range(0, w, PROJ_BN)]
        off += w
    return cols, off, row_starts


def _chunk_consts(bb):
    c, g, hn = CHUNK, RWKV_GROUP, RWKV_HEAD
    assert c == hn
    n = g * c
    i = jnp.arange(n)
    ti, tj = jnp.arange(c)[:, None], i[None, :] % c
    lane = jnp.arange(g * hn)
    return {
        "tri_rows": jnp.kron(jnp.eye(bb, dtype=F32), jnp.tril(jnp.ones((c, c), F32))).astype(BF16),
        "ones_bd": ((lane[:, None] // hn) == (lane[None, :] // hn)).astype(BF16),
        "m_strict": (ti > tj).astype(F32),
        "m_incl": (ti >= tj).astype(F32),
        "m_blk": ((i[:, None] // c) == (lane[None, :] // hn)).astype(F32),
    }


def kernel(x, ffn1_pre_norm, ffn1_w_gate, ffn1_w_up, ffn1_w_down, ffn1_post_norm, mix_pre_norm, w_in, gla_gate_up, gla_gate_bias, gla_out_norm, rwkv_shift_mix, rwkv_w0, rwkv_w2, rwkv_a0, rwkv_a2, rwkv_g2, rwkv_k_k, rwkv_k_a, rwkv_r_k, rwkv_ln_w, rwkv_ln_b, w_up_gla, w_up_rwkv, w_out, mix_post_norm, ffn2_pre_norm, ffn2_w_gate, ffn2_w_up, ffn2_w_down, ffn2_post_norm):
    batch, seq, d_model = x.shape
    depth = ffn1_pre_norm.shape[0]
    t = batch * seq
    assert seq % CHUNK == 0 and d_model % 1024 == 0

    gla_kw, gla_vw, rw_w = d_model // 4, d_model // 2, d_model // 2
    gla_heads = max(4, d_model // 512)
    dk, dv = gla_kw // gla_heads, gla_vw // gla_heads
    lora = rwkv_w2.shape[1]
    g_lora = rwkv_g2.shape[1]
    assert lora == LANES and g_lora <= 2 * PROJ_BN and dk % LANES == 0

    cols, n_proj, row_starts = _layout(d_model, lora, g_lora)
    bm = min(1024, t)
    bm_down = min(512, t)
    bm_gu = min(2048, t)
    bt = min(256, t)
    bb = next(n for n in (4, 2, 1) if batch % n == 0)
    consts = _chunk_consts(bb)

    def row(v):
        return v.reshape(1, -1).astype(F32)

    h = x.reshape(t, d_model)
    for l in range(depth):
        xn = _norm_cast(h, row(ffn1_pre_norm[l]), bt)
        hid, w_down = _gate_up(xn, ffn1_w_gate[l], ffn1_w_up[l], ffn1_w_down[l], bm_gu, 256)
        f = _matmul(hid, w_down, BF16, bm_down, 512)
        h, u = _post_pre(h, f, row(ffn1_post_norm[l]), row(mix_pre_norm[l]), MACARON_WEIGHT, bt)

        proj = _proj(u, jnp.swapaxes(w_in[l], 0, 1), jnp.asarray(row_starts, jnp.int32), F32, bm_gu, PROJ_BN)

        mu = rwkv_shift_mix[l]
        mu_off = [0]
        for s in (rw_w, lora, rw_w, rw_w, lora, g_lora):
            mu_off.append(mu_off[-1] + s)
        mu_r, mu_wd, mu_k, mu_v, mu_ad, mu_gd = (mu[mu_off[i]:mu_off[i + 1]] for i in range(6))
        mus = {"r": row(mu_r), "k": row(mu_k), "v": row(mu_v),
               "gd": row(jnp.pad(mu_gd, (0, 2 * PROJ_BN - g_lora))), "wd": row(mu_wd), "ad": row(mu_ad)}

        proj3 = proj.reshape(batch, seq, n_proj)
        o_gla = _gla(proj3, _pad_rows(gla_gate_up[l], LANES).astype(BF16), row(gla_gate_bias[l]),
                     row(gla_out_norm[l]), consts["tri_rows"], cols, bb, gla_heads, dk, dv)
        o_rwkv = _rwkv(proj3, mus, row(rwkv_w0[l]), rwkv_w2[l].astype(BF16), row(rwkv_a0[l]),
                       rwkv_a2[l].astype(BF16), _pad_rows(rwkv_g2[l], 2 * PROJ_BN).astype(BF16),
                       row(rwkv_k_k[l]), row(rwkv_k_a[l]), row(rwkv_r_k[l]), row(rwkv_ln_w[l]),
                       row(rwkv_ln_b[l]), consts, cols, bb, rw_w, lora)

        merged = _merge(o_gla.reshape(t, gla_vw), o_rwkv.reshape(t, rw_w),
                        w_up_gla[l].astype(BF16), w_up_rwkv[l].astype(BF16), proj,
                        cols["gate_gla"], cols["gate_rwkv"], bm, 512)
        mixed = _matmul(merged, w_out[l].astype(BF16), BF16, bm, 1024)
        h, u = _post_pre(h, mixed, row(mix_post_norm[l]), row(ffn2_pre_norm[l]), 1.0, bt)

        hid, w_down = _gate_up(u, ffn2_w_gate[l], ffn2_w_up[l], ffn2_w_down[l], bm_gu, 256)
        f = _matmul(hid, w_down, BF16, bm_down, 512)
        h = _post(h, f, row(ffn2_post_norm[l]), MACARON_WEIGHT, bt)
    return h.reshape(batch, seq, d_model)
```

```python
import functools

import jax
import jax.numpy as jnp
from jax import lax
from jax.experimental import pallas as pl
from jax.experimental.pallas import tpu as pltpu

F32 = jnp.float32
BF16 = jnp.bfloat16

NORM_EPS = 1e-6
RWKV_LN_EPS = 64e-5
MACARON_WEIGHT = 0.5
GLA_GATE_NORM = 16.0
GLA_GATE_RANK = 16
LOG2_E = 1.4426950408889634
RWKV_HEAD = 64

LANES = 128
ROW_TILE = 16
CHUNK = 64
SUB_LOG2 = 3
SUB = 1 << SUB_LOG2
GLA_HEADS_PER_STEP = 2
RWKV_GROUP = 4
RWKV_GROUPS_PER_STEP = 4
VMEM_LIMIT = 56 * 1024 * 1024
VMEM_LIMIT_MAX = 60 * 1024 * 1024


def _cparams(sem):
    return pltpu.CompilerParams(dimension_semantics=sem, vmem_limit_bytes=VMEM_LIMIT)


def _dot(a, b):
    return jnp.dot(a.astype(BF16), b.astype(BF16), preferred_element_type=F32)


def _dot_nt(a, b):
    return lax.dot_general(a.astype(BF16), b.astype(BF16), (((1,), (1,)), ((), ())),
                           preferred_element_type=F32)


def _dot_tn(a, b):
    return lax.dot_general(a.astype(BF16), b.astype(BF16), (((0,), (0,)), ((), ())),
                           preferred_element_type=F32)


def _dot_split(m01, x):
    hi = x.astype(BF16)
    lo = (x - hi.astype(F32)).astype(BF16)
    return (jnp.dot(m01, hi, preferred_element_type=F32)
            + jnp.dot(m01, lo, preferred_element_type=F32))


def _rms(x, g):
    ms = jnp.mean(x * x, axis=-1, keepdims=True)
    return x * lax.rsqrt(ms + NORM_EPS) * g


def _sigmoid(x):
    return 1.0 / (1.0 + jnp.exp(-x))


def _softplus(x):
    return jnp.maximum(x, 0.0) + jnp.log(1.0 + jnp.exp(-jnp.abs(x)))


def _norm_cast_kernel(x_ref, g_ref, o_ref):
    o_ref[...] = _rms(x_ref[...], g_ref[...]).astype(o_ref.dtype)


def _norm_cast(x, g, bt):
    t, d = x.shape
    return pl.pallas_call(
        _norm_cast_kernel,
        grid=(t // bt,),
        in_specs=[pl.BlockSpec((bt, d), lambda i: (i, 0)),
                  pl.BlockSpec((1, d), lambda i: (0, 0))],
        out_specs=pl.BlockSpec((bt, d), lambda i: (i, 0)),
        out_shape=jax.ShapeDtypeStruct((t, d), BF16),
        compiler_params=_cparams(("parallel",)),
    )(x, g)


def _post_pre_kernel(h_ref, f_ref, gpost_ref, gpre_ref, hout_ref, u_ref, *, alpha):
    h = h_ref[...] + alpha * _rms(f_ref[...].astype(F32), gpost_ref[...])
    hout_ref[...] = h
    u_ref[...] = _rms(h, gpre_ref[...]).astype(u_ref.dtype)


def _post_pre(h, f, g_post, g_pre, alpha, bt):
    t, d = h.shape
    row = pl.BlockSpec((bt, d), lambda i: (i, 0))
    vec = pl.BlockSpec((1, d), lambda i: (0, 0))
    return pl.pallas_call(
        functools.partial(_post_pre_kernel, alpha=alpha),
        grid=(t // bt,),
        in_specs=[row, row, vec, vec],
        out_specs=[row, row],
        out_shape=[jax.ShapeDtypeStruct((t, d), F32), jax.ShapeDtypeStruct((t, d), BF16)],
        compiler_params=_cparams(("parallel",)),
    )(h, f, g_post, g_pre)


def _post_kernel(h_ref, f_ref, gpost_ref, hout_ref, *, alpha):
    hout_ref[...] = h_ref[...] + alpha * _rms(f_ref[...].astype(F32), gpost_ref[...])


def _post(h, f, g_post, alpha, bt):
    t, d = h.shape
    row = pl.BlockSpec((bt, d), lambda i: (i, 0))
    vec = pl.BlockSpec((1, d), lambda i: (0, 0))
    return pl.pallas_call(
        functools.partial(_post_kernel, alpha=alpha),
        grid=(t // bt,),
        in_specs=[row, row, vec],
        out_specs=row,
        out_shape=jax.ShapeDtypeStruct((t, d), F32),
        compiler_params=_cparams(("parallel",)),
    )(h, f, g_post)


def _matmul_kernel(x_ref, w_ref, o_ref):
    o_ref[...] = jnp.dot(x_ref[...], w_ref[...], preferred_element_type=F32).astype(o_ref.dtype)


def _matmul_nt_kernel(x_ref, w_ref, o_ref):
    o_ref[...] = lax.dot_general(x_ref[...], w_ref[...], (((1,), (1,)), ((), ())),
                                 preferred_element_type=F32).astype(o_ref.dtype)


def _matmul_nt(x, w_t, out_dtype, bm, bn):
    m, k = x.shape
    n = w_t.shape[0]
    return pl.pallas_call(
        _matmul_nt_kernel, grid=(m // bm, n // bn),
        in_specs=[pl.BlockSpec((bm, k), lambda i, j: (i, 0)), pl.BlockSpec((bn, k), lambda i, j: (j, 0))],
        out_specs=pl.BlockSpec((bm, bn), lambda i, j: (i, j)),
        out_shape=jax.ShapeDtypeStruct((m, n), out_dtype),
        compiler_params=_cparams(("parallel", "arbitrary")),
    )(x, w_t)


def _matmul(x, w, out_dtype, bm, bn):
    m, k = x.shape
    n = w.shape[1]
    return pl.pallas_call(
        _matmul_kernel,
        grid=(m // bm, pl.cdiv(n, bn)),
        in_specs=[pl.BlockSpec((bm, k), lambda i, j: (i, 0)),
                  pl.BlockSpec((k, bn), lambda i, j: (0, j))],
        out_specs=pl.BlockSpec((bm, bn), lambda i, j: (i, j)),
        out_shape=jax.ShapeDtypeStruct((m, n), out_dtype),
        compiler_params=_cparams(("parallel", "arbitrary")),
    )(x, w)


def _matmul_cast_kernel(x_ref, w_ref, c_ref, o_ref, c_bf16_ref):
    o_ref[...] = jnp.dot(x_ref[...], w_ref[...], preferred_element_type=F32).astype(o_ref.dtype)
    c_bf16_ref[...] = c_ref[...].astype(BF16)


def _matmul_and_cast(x, w, out_dtype, bm, bn, other):
    m, k = x.shape
    n = w.shape[1]
    nj = pl.cdiv(n, bn)
    steps = (m // bm) * nj
    rows = other.shape[0]
    slab = -(-rows // (steps * ROW_TILE)) * ROW_TILE
    last = pl.cdiv(rows, slab) - 1
    slab_spec = pl.BlockSpec((slab, other.shape[1]), lambda i, j: (jnp.minimum(i * nj + j, last), 0))
    return pl.pallas_call(
        _matmul_cast_kernel,
        grid=(m // bm, nj),
        in_specs=[pl.BlockSpec((bm, k), lambda i, j: (i, 0)),
                  pl.BlockSpec((k, bn), lambda i, j: (0, j)), slab_spec],
        out_specs=[pl.BlockSpec((bm, bn), lambda i, j: (i, j)), slab_spec],
        out_shape=[jax.ShapeDtypeStruct((m, n), out_dtype), jax.ShapeDtypeStruct(other.shape, BF16)],
        compiler_params=_cparams(("arbitrary", "arbitrary")),
    )(x, w, other)


def _gate_up_kernel(x_ref, wg_ref, wu_ref, wd_ref, o_ref, wd_bf16_ref):
    x = x_ref[...]
    g = jnp.dot(x, wg_ref[...].astype(BF16), preferred_element_type=F32)
    u = jnp.dot(x, wu_ref[...].astype(BF16), preferred_element_type=F32)
    o_ref[...] = (g * _sigmoid(g) * u).astype(o_ref.dtype)
    wd_bf16_ref[...] = wd_ref[...].astype(BF16)


def _gate_up(x, wg, wu, wd, bm, bn):
    m, k = x.shape
    n = wg.shape[1]
    nj = pl.cdiv(n, bn)
    steps = (m // bm) * nj
    slab = wd.shape[0] // steps
    assert slab * steps == wd.shape[0] and slab % 16 == 0, (wd.shape, steps)
    wspec = pl.BlockSpec((k, bn), lambda i, j: (0, j))
    slab_spec = pl.BlockSpec((slab, wd.shape[1]), lambda i, j: (i * nj + j, 0))
    return pl.pallas_call(
        _gate_up_kernel,
        grid=(m // bm, nj),
        in_specs=[pl.BlockSpec((bm, k), lambda i, j: (i, 0)), wspec, wspec, slab_spec],
        out_specs=[pl.BlockSpec((bm, bn), lambda i, j: (i, j)), slab_spec],
        out_shape=[jax.ShapeDtypeStruct((m, n), BF16), jax.ShapeDtypeStruct(wd.shape, BF16)],
        compiler_params=pltpu.CompilerParams(dimension_semantics=("arbitrary", "arbitrary"),
                                             vmem_limit_bytes=VMEM_LIMIT_MAX),
    )(x, wg, wu, wd)


def _merge_kernel(og_ref, or_ref, wg_ref, wr_ref, gg_ref, gr_ref, o_ref):
    yg = jnp.dot(og_ref[...], wg_ref[...], preferred_element_type=F32)
    yr = jnp.dot(or_ref[...], wr_ref[...], preferred_element_type=F32)
    o_ref[...] = (_sigmoid(gg_ref[...]) * yg + _sigmoid(gr_ref[...]) * yr).astype(o_ref.dtype)


def _merge(o_gla, o_rwkv, w_up_gla, w_up_rwkv, proj, col_gate_gla, col_gate_rwkv, bm, bn):
    m, k = o_gla.shape
    n = w_up_gla.shape[1]
    xspec = pl.BlockSpec((bm, k), lambda i, j: (i, 0))
    wspec = pl.BlockSpec((k, bn), lambda i, j: (0, j))
    ga, gb = col_gate_gla // bn, col_gate_rwkv // bn
    return pl.pallas_call(
        _merge_kernel,
        grid=(m // bm, n // bn),
        in_specs=[xspec, xspec, wspec, wspec,
                  pl.BlockSpec((bm, bn), lambda i, j: (i, ga + j)),
                  pl.BlockSpec((bm, bn), lambda i, j: (i, gb + j))],
        out_specs=pl.BlockSpec((bm, bn), lambda i, j: (i, j)),
        out_shape=jax.ShapeDtypeStruct((m, n), BF16),
        compiler_params=_cparams(("parallel", "arbitrary")),
    )(o_gla, o_rwkv, w_up_gla, w_up_rwkv, proj, proj)


def _gla_kernel(q_ref, k_ref, v_ref, og_ref, gd_ref, gup_ref, gbias_ref, onorm_ref, tri_ref,
                o_ref, state_ref, *, dk, dv, bb, hq):
    c, nsub = CHUNK, CHUNK // SUB
    rows = bb * c

    @pl.when(pl.program_id(2) == 0)
    def _():
        state_ref[...] = jnp.zeros_like(state_ref)

    def merged(ref, lanes=slice(None)):
        return ref[:, :, lanes].reshape(rows, -1)

    def per_row(x):
        return [x[i * c:(i + 1) * c] for i in range(bb)]

    def chunk_row(x, r):
        return jnp.concatenate(
            [jnp.broadcast_to(x[i * c + r:i * c + r + 1, :], (c, x.shape[-1])) for i in range(bb)], axis=0)

    row = lax.broadcasted_iota(jnp.int32, (rows, c), 0) & (c - 1)
    lane = lax.broadcasted_iota(jnp.int32, (rows, c), 1)
    blk0 = row & -SUB
    d = lane - blk0
    dc = jnp.where(d >= 0, jnp.where(d <= (row & (SUB - 1)), d, -1), -1)
    mask_off = (lane < blk0).astype(F32)
    sub_id = (lax.broadcasted_iota(jnp.int32, (rows, dk), 0) & (c - 1)) >> SUB_LOG2
    gd = merged(gd_ref).astype(BF16)
    tri = tri_ref[...]

    qs, ks, bs, qhat, khat, b_ends = [], [], [], [], [], []
    for h in range(hq):
        lanes = slice(h * dk, (h + 1) * dk)
        x = _dot(gd, gup_ref[:, lanes]) + gbias_ref[:, lanes]
        log_a = -_softplus(-x) * (1.0 / GLA_GATE_NORM)
        b = _dot_split(tri, log_a * LOG2_E)
        q = merged(q_ref, lanes) * (dk ** -0.5)
        k = merged(k_ref, lanes)
        refs = [chunk_row(b, i * SUB - 1) for i in range(1, nsub)]
        bref = jnp.zeros((rows, dk), F32)
        for i in range(1, nsub):
            bref = jnp.where(sub_id == i, refs[i - 1], bref)
        qhat.append(per_row(q * jnp.exp2(b - bref)))
        khat.append([per_row(k * jnp.exp2(jnp.minimum(ref - b, 0.0))) for ref in refs])
        qs.append(q)
        ks.append(k)
        bs.append(b)
        b_ends.append(chunk_row(b, c - 1))

    attn = [jnp.concatenate(
        [jnp.concatenate(
            [jnp.zeros((SUB, c), F32)]
            + [_dot_nt(qhat[h][r][i * SUB:(i + 1) * SUB, :], khat[h][i - 1][r]) for i in range(1, nsub)],
            axis=0)
         for r in range(bb)], axis=0) * mask_off for h in range(hq)]

    for h in range(hq):
        q, k, b = qs[h], ks[h], bs[h]
        k3 = k.reshape(bb * nsub, SUB, dk)
        b3 = b.reshape(bb * nsub, SUB, dk)
        for j in range(SUB):
            kj = jnp.broadcast_to(k3[:, j:j + 1, :], (bb * nsub, SUB, dk)).reshape(rows, dk)
            bj = jnp.broadcast_to(b3[:, j:j + 1, :], (bb * nsub, SUB, dk)).reshape(rows, dk)
            col = jnp.sum(q * kj * jnp.exp2(jnp.minimum(b - bj, 0.0)), axis=-1, keepdims=True)
            attn[h] = jnp.where(dc == j, col, attn[h])

    chains = [(h, i) for h in range(hq) for i in range(bb)]
    attn_c = [t for h in range(hq) for t in per_row(attn[h])]
    qe = [t for h in range(hq) for t in per_row(qs[h] * jnp.exp2(bs[h]))]
    k_end = [t for h in range(hq) for t in per_row(ks[h] * jnp.exp2(b_ends[h] - bs[h]))]
    v_c = [v_ref[i, :, h * dv:(h + 1) * dv] for h, i in chains]
    st = [state_ref[i, h] for h, i in chains]
    o = [_dot(a_, v_) + _dot_nt(q_, s_) for a_, v_, q_, s_ in zip(attn_c, v_c, qe, st)]
    for j, (h, i) in enumerate(chains):
        state_ref[i, h] = st[j] * jnp.exp2(b_ends[h][i * c:i * c + 1, :]) + _dot_tn(v_c[j], k_end[j])

    for h in range(hq):
        lanes = slice(h * dv, (h + 1) * dv)
        oh = _rms(jnp.concatenate(o[h * bb:(h + 1) * bb], axis=0), onorm_ref[...])
        og = merged(og_ref, lanes)
        o_ref[:, :, lanes] = (oh * (og * _sigmoid(og))).astype(o_ref.dtype).reshape(bb, c, dv)


def _gla(proj, gate_up, gate_bias, out_norm, tri, cols, bb, heads, dk, dv):
    batch, seq, _ = proj.shape
    nc = seq // CHUNK
    hq = next(m for m in (GLA_HEADS_PER_STEP, 1) if heads % m == 0)
    kw, vw = hq * dk, hq * dv
    cq, ck, cv, cog, cgd = (cols[n] for n in ("gla_q", "gla_k", "gla_v", "gla_og", "gla_gd"))
    return pl.pallas_call(
        functools.partial(_gla_kernel, dk=dk, dv=dv, bb=bb, hq=hq),
        grid=(batch // bb, heads // hq, nc),
        in_specs=[
            pl.BlockSpec((bb, CHUNK, kw), lambda b, h, c: (b, c, cq // kw + h)),
            pl.BlockSpec((bb, CHUNK, kw), lambda b, h, c: (b, c, ck // kw + h)),
            pl.BlockSpec((bb, CHUNK, vw), lambda b, h, c: (b, c, cv // vw + h)),
            pl.BlockSpec((bb, CHUNK, vw), lambda b, h, c: (b, c, cog // vw + h)),
            pl.BlockSpec((bb, CHUNK, LANES), lambda b, h, c: (b, c, cgd // LANES)),
            pl.BlockSpec((LANES, kw), lambda b, h, c: (0, h)),
            pl.BlockSpec((1, kw), lambda b, h, c: (0, h)),
            pl.BlockSpec((1, dv), lambda b, h, c: (0, 0)),
            pl.BlockSpec((bb * CHUNK, bb * CHUNK), lambda b, h, c: (0, 0)),
        ],
        out_specs=pl.BlockSpec((bb, CHUNK, vw), lambda b, h, c: (b, c, h)),
        out_shape=jax.ShapeDtypeStruct((batch, seq, heads * dv), BF16),
        scratch_shapes=[pltpu.VMEM((bb, hq, dv, dk), F32)],
        compiler_params=_cparams(("parallel", "parallel", "arbitrary")),
    )(proj, proj, proj, proj, proj, gate_up, gate_bias, out_norm, tri)


def _rwkv_kernel(r_ref, k_ref, v_ref, gd_ref, wa_ref, rp_ref, kp_ref, vp_ref, gdp_ref, wap_ref,
                 mur_ref, muk_ref, muv_ref, mugd_ref, muwa_ref,
                 w0_ref, w2_ref, a0_ref, a2_ref, g2_ref, kk_ref, ka_ref, rk_ref, lnw_ref, lnb_ref,
                 tri_ref, ones_ref, mstrict_ref, mincl_ref, mblk_ref,
                 o_ref, state_ref, *, lora, bb, gq):
    c, g = CHUNK, RWKV_GROUP
    n, rows, gw = g * c, bb * c, g * RWKV_HEAD
    first = pl.program_id(2) == 0

    @pl.when(first)
    def _():
        state_ref[...] = jnp.zeros_like(state_ref)

    keep_prev = jnp.where(first, 0.0, 1.0)

    def per_row(x):
        return [x[i * c:(i + 1) * c] for i in range(bb)]

    def shifted(cur_ref, prev_ref, mu_ref, lanes):
        p = cur_ref[:, :, lanes].reshape(rows, -1)
        width = p.shape[-1]
        prev = jnp.concatenate(
            [jnp.broadcast_to(prev_ref[i, 7:8, lanes] * keep_prev, (c, width)) for i in range(bb)], axis=0)
        is_row0 = (lax.broadcasted_iota(jnp.int32, p.shape, 0) & (c - 1)) == 0
        p_prev = jnp.where(is_row0, prev, pltpu.roll(p, 1, axis=0))
        return p + (p_prev - p) * mu_ref[:, lanes]

    everything = slice(None)
    wa = shifted(wa_ref, wap_ref, muwa_ref, everything)
    tanh_wd = jnp.tanh(wa[:, :lora]).astype(BF16)
    a_down = wa[:, lora:].astype(BF16)
    sig_gd = _sigmoid(shifted(gd_ref, gdp_ref, mugd_ref, everything)).astype(BF16)

    ones_bd = ones_ref[...]
    m_blk = mblk_ref[...]
    m_blk_bf16 = m_blk.astype(BF16)
    tri = tri_ref[...]

    r_all, k2_all, v_all, gate_all = [], [], [], []
    rt, kkt, km, bm, k_end, b_end, v_rows, c_end = [], [], [], [], [], [], [], []
    for q in range(gq):
        lanes = slice(q * gw, (q + 1) * gw)
        r = shifted(r_ref, rp_ref, mur_ref, lanes)
        k = shifted(k_ref, kp_ref, muk_ref, lanes)
        v = shifted(v_ref, vp_ref, muv_ref, lanes)
        w = -_softplus(-(w0_ref[:, lanes] + _dot(tanh_wd, w2_ref[:, lanes]))) - 0.5
        ld = -jnp.exp(w)
        a = _sigmoid(a0_ref[:, lanes] + _dot(a_down, a2_ref[:, lanes]))
        gate_all.append(_dot(sig_gd, g2_ref[:, lanes]))

        kkr = k * kk_ref[:, lanes]
        kk = kkr * lax.rsqrt(jnp.maximum(_dot(kkr * kkr, ones_bd), 1e-24))
        k2 = k * (1.0 + (a - 1.0) * ka_ref[:, lanes])
        bw = a * kk

        cum = _dot_split(tri, ld)
        ends = [cum[(i + 1) * c - 1:(i + 1) * c, :] for i in range(bb)]
        c_end_rows = jnp.concatenate([jnp.broadcast_to(e, (c, gw)) for e in ends], axis=0)
        e_neg = jnp.exp(-cum)
        e_end = jnp.exp(c_end_rows - cum)
        c_end += ends
        rt += per_row(r * jnp.exp(cum))
        kkt += per_row(kk * jnp.exp(cum - ld))
        km += per_row(k2 * e_neg)
        bm += per_row(bw * e_neg)
        k_end += per_row(k2 * e_end)
        b_end += per_row(bw * e_end)
        v_rows += per_row(v)
        r_all.append(r)
        k2_all.append(k2)
        v_all.append(v)

    def stack(x):
        return jnp.concatenate([x.astype(BF16)] * g, axis=0) * m_blk_bf16

    m_strict, m_incl = mstrict_ref[...], mincl_ref[...]
    lhs = [jnp.concatenate([kq, rq], axis=0).astype(BF16) for kq, rq in zip(kkt, rt)]
    aa = [_dot_nt(l, jnp.concatenate([stack(b_), stack(k_)], axis=0))
          for l, b_, k_ in zip(lhs, bm, km)]
    a_ab = [t[:c, :n] * m_strict for t in aa]
    a_kr = [jnp.concatenate([t[:c, n:] * m_strict, t[c:, n:] * m_incl], axis=0) for t in aa]
    a_rb = [t[c:, :n] * m_incl for t in aa]

    eye = m_incl - m_strict
    x = [eye - t for t in a_ab]
    p = [_dot(t, stack(t)) for t in a_ab]
    span = 2
    while span < c:
        span *= 2
        if span < c:
            xp = [_dot(jnp.concatenate([xi, pi], axis=0), stack(pi)) for xi, pi in zip(x, p)]
            x = [xi + t[:c] for xi, t in zip(x, xp)]
            p = [t[c:] for t in xp]
        else:
            x = [xi + _dot(xi, stack(pi)) for xi, pi in zip(x, p)]

    chains = [(q, i) for q in range(gq) for i in range(bb)]
    st = [state_ref[i, q] for q, i in chains]
    sk = [_dot_nt(l, s) for l, s in zip(lhs, st)]
    av = [_dot(t, stack(vi)) for t, vi in zip(a_kr, v_rows)]
    u = [_dot(xi, stack(-(s[:c] + t[:c]))) for xi, s, t in zip(x, sk, av)]
    y = [s[c:] + _dot(t, stack(ui)) + w_[c:] for s, t, ui, w_ in zip(sk, a_rb, u, av)]
    for j, (q, i) in enumerate(chains):
        state_ref[i, q] = st[j] * jnp.exp(c_end[j]) + _dot_tn(
            jnp.concatenate([u[j], v_rows[j]], axis=0),
            jnp.concatenate([b_end[j], k_end[j]], axis=0)) * m_blk

    inv_n = 1.0 / RWKV_HEAD
    for q in range(gq):
        lanes = slice(q * gw, (q + 1) * gw)
        yq = jnp.concatenate(y[q * bb:(q + 1) * bb], axis=0)
        mu = _dot(yq, ones_bd) * inv_n
        dy = yq - mu
        var = _dot(dy * dy, ones_bd) * inv_n
        yn = dy * lax.rsqrt(var + RWKV_LN_EPS) * lnw_ref[:, lanes] + lnb_ref[:, lanes]
        bonus = _dot(r_all[q] * k2_all[q] * rk_ref[:, lanes], ones_bd) * v_all[q]
        o_ref[:, :, lanes] = ((yn + bonus) * gate_all[q]).astype(o_ref.dtype).reshape(bb, c, gw)


def _rwkv(proj, mus, w0, w2, a0, a2, g2, k_k, k_a, r_k, ln_w, ln_b, consts, cols, bb, width, lora):
    batch, seq, _ = proj.shape
    nc = seq // CHUNK
    gw = RWKV_GROUP * RWKV_HEAD
    gq = next(m for m in (RWKV_GROUPS_PER_STEP, 1) if (width // gw) % m == 0)
    sw = gq * gw
    steps = width // sw
    gdw = g2.shape[0]
    waw = 2 * lora

    def prev_rows(c):
        return jnp.maximum(c * (CHUNK // 8) - 1, 0)

    def cur(w_, col, per_group):
        if per_group:
            return pl.BlockSpec((bb, CHUNK, w_), lambda b, q, c: (b, c, col // w_ + q))
        return pl.BlockSpec((bb, CHUNK, w_), lambda b, q, c: (b, c, col // w_))

    def prev(w_, col, per_group):
        if per_group:
            return pl.BlockSpec((bb, 8, w_), lambda b, q, c: (b, prev_rows(c), col // w_ + q))
        return pl.BlockSpec((bb, 8, w_), lambda b, q, c: (b, prev_rows(c), col // w_))

    def gvec():
        return pl.BlockSpec((1, sw), lambda b, q, c: (0, q))

    def full(shape):
        return pl.BlockSpec(shape, lambda b, q, c: (0, 0))

    n = RWKV_GROUP * CHUNK
    in_specs = [
        cur(sw, cols["rw_r"], True), cur(sw, cols["rw_k"], True), cur(sw, cols["rw_v"], True),
        cur(gdw, cols["rw_gd"], False), cur(waw, cols["rw_wa"], False),
        prev(sw, cols["rw_r"], True), prev(sw, cols["rw_k"], True), prev(sw, cols["rw_v"], True),
        prev(gdw, cols["rw_gd"], False), prev(waw, cols["rw_wa"], False),
        gvec(), gvec(), gvec(), full((1, gdw)), full((1, waw)),
        gvec(), pl.BlockSpec((lora, sw), lambda b, q, c: (0, q)),
        gvec(), pl.BlockSpec((lora, sw), lambda b, q, c: (0, q)),
        pl.BlockSpec((gdw, sw), lambda b, q, c: (0, q)),
        gvec(), gvec(), gvec(), gvec(), gvec(),
        full((bb * CHUNK, bb * CHUNK)), full((gw, gw)), full((CHUNK, n)), full((CHUNK, n)), full((n, gw)),
    ]
    return pl.pallas_call(
        functools.partial(_rwkv_kernel, lora=lora, bb=bb, gq=gq),
        grid=(batch // bb, steps, nc),
        in_specs=in_specs,
        out_specs=pl.BlockSpec((bb, CHUNK, sw), lambda b, q, c: (b, c, q)),
        out_shape=jax.ShapeDtypeStruct((batch, seq, width), BF16),
        scratch_shapes=[pltpu.VMEM((bb, gq, gw, gw), F32)],
        compiler_params=_cparams(("parallel", "parallel", "arbitrary")),
    )(proj, proj, proj, proj, proj, proj, proj, proj, proj, proj,
      mus["r"], mus["k"], mus["v"], mus["gd"], mus["wa"],
      w0, w2, a0, a2, g2, k_k, k_a, r_k, ln_w, ln_b,
      consts["tri_rows"], consts["ones_bd"], consts["m_strict"], consts["m_incl"], consts["m_blk"])


def _pad_rows(w, n):
    return jnp.pad(w, ((0, n - w.shape[0]), (0, 0)))


def _layout(d_model):
    kw, vw, rw = d_model // 4, d_model // 2, d_model // 2
    lora = 128
    gdw = 512
    names = [("gla_q", kw), ("gla_k", kw), ("gla_v", vw), ("gla_og", vw),
             ("rw_r", rw), ("rw_k", rw), ("rw_v", rw),
             ("gate_gla", d_model), ("gate_rwkv", d_model),
             ("rw_gd", gdw), ("rw_wa", 2 * lora), ("gla_gd", LANES)]
    cols, off = {}, 0
    for name, w in names:
        assert off % w == 0, (name, off, w)
        cols[name] = off
        off += w
    return cols, off


def _chunk_consts(bb):
    c, g, hn = CHUNK, RWKV_GROUP, RWKV_HEAD
    assert c == hn
    n = g * c
    i = jnp.arange(n)
    ti, tj = jnp.arange(c)[:, None], i[None, :] % c
    lane = jnp.arange(g * hn)
    return {
        "tri_rows": jnp.kron(jnp.eye(bb, dtype=F32), jnp.tril(jnp.ones((c, c), F32))).astype(BF16),
        "ones_bd": ((lane[:, None] // hn) == (lane[None, :] // hn)).astype(BF16),
        "m_strict": (ti > tj).astype(F32),
        "m_incl": (ti >= tj).astype(F32),
        "m_blk": ((i[:, None] // c) == (lane[None, :] // hn)).astype(F32),
    }


def kernel(x, ffn1_pre_norm, ffn1_w_gate, ffn1_w_up, ffn1_w_down, ffn1_post_norm, mix_pre_norm, w_in, gla_gate_up, gla_gate_bias, gla_out_norm, rwkv_shift_mix, rwkv_w0, rwkv_w2, rwkv_a0, rwkv_a2, rwkv_g2, rwkv_k_k, rwkv_k_a, rwkv_r_k, rwkv_ln_w, rwkv_ln_b, w_up_gla, w_up_rwkv, w_out, mix_post_norm, ffn2_pre_norm, ffn2_w_gate, ffn2_w_up, ffn2_w_down, ffn2_post_norm):
    batch, seq, d_model = x.shape
    depth = ffn1_pre_norm.shape[0]
    t = batch * seq
    assert seq % CHUNK == 0 and d_model % 1024 == 0

    gla_kw, gla_vw, rw_w = d_model // 4, d_model // 2, d_model // 2
    gla_heads = max(4, d_model // 512)
    dk, dv = gla_kw // gla_heads, gla_vw // gla_heads
    lora = rwkv_w2.shape[1]
    g_lora = rwkv_g2.shape[1]
    assert lora == 128 and g_lora <= 512 and dk % LANES == 0

    cols, n_proj = _layout(d_model)
    n_proj_pad = -(-n_proj // 1024) * 1024
    bm = min(1024, t)
    bm_down = min(512, t)
    bm_gu = min(2048, t)
    bt = min(256, t)
    bb = next(n for n in (4, 2, 1) if batch % n == 0)
    consts = _chunk_consts(bb)

    def row(v):
        return v.reshape(1, -1).astype(F32)

    h = x.reshape(t, d_model)
    for l in range(depth):
        xn = _norm_cast(h, row(ffn1_pre_norm[l]), bt)
        hid, w_down = _gate_up(xn, ffn1_w_gate[l], ffn1_w_up[l], ffn1_w_down[l], bm_gu, 256)
        f, wi = _matmul_and_cast(hid, w_down, BF16, bm_down, 512, jnp.swapaxes(w_in[l], 0, 1))
        h, u = _post_pre(h, f, row(ffn1_post_norm[l]), row(mix_pre_norm[l]), MACARON_WEIGHT, bt)

        o_rw = 2 * gla_kw + 2 * gla_vw + GLA_GATE_RANK
        gq, gk, gv, ggd, gog = jnp.split(wi[:o_rw], [gla_kw, 2 * gla_kw, 2 * gla_kw + gla_vw,
                                                    2 * gla_kw + gla_vw + GLA_GATE_RANK], axis=0)
        rw_sizes = [rw_w, lora, rw_w, rw_w, lora, g_lora]
        rw_off = [o_rw]
        for s in rw_sizes:
            rw_off.append(rw_off[-1] + s)
        rr, rwd, rk, rv, rad, rgd = (wi[rw_off[i]:rw_off[i + 1]] for i in range(6))
        gates = wi[rw_off[-1]:]
        w_proj_t = jnp.concatenate(
            [gq, gk, gv, gog, rr, rk, rv, gates, _pad_rows(rgd, 512), rwd, rad, _pad_rows(ggd, LANES),
             jnp.zeros((n_proj_pad - n_proj, d_model), BF16)], axis=0)
        proj = _matmul_nt(u, w_proj_t, F32, bm, 1024)

        mu = rwkv_shift_mix[l]
        mu_off = [o - o_rw for o in rw_off]
        mu_r, mu_wd, mu_k, mu_v, mu_ad, mu_gd = (mu[mu_off[i]:mu_off[i + 1]] for i in range(6))
        mus = {"r": row(mu_r), "k": row(mu_k), "v": row(mu_v),
               "gd": row(jnp.pad(mu_gd, (0, 512 - g_lora))),
               "wa": row(jnp.concatenate([mu_wd, mu_ad]))}

        proj3 = proj.reshape(batch, seq, n_proj_pad)
        o_gla = _gla(proj3, _pad_rows(gla_gate_up[l], LANES).astype(BF16), row(gla_gate_bias[l]),
                     row(gla_out_norm[l]), consts["tri_rows"], cols, bb, gla_heads, dk, dv)
        o_rwkv = _rwkv(proj3, mus, row(rwkv_w0[l]), rwkv_w2[l].astype(BF16), row(rwkv_a0[l]),
                       rwkv_a2[l].astype(BF16), _pad_rows(rwkv_g2[l], 512).astype(BF16),
                       row(rwkv_k_k[l]), row(rwkv_k_a[l]), row(rwkv_r_k[l]), row(rwkv_ln_w[l]),
                       row(rwkv_ln_b[l]), consts, cols, bb, rw_w, lora)

        merged = _merge(o_gla.reshape(t, gla_vw), o_rwkv.reshape(t, rw_w),
                        w_up_gla[l].astype(BF16), w_up_rwkv[l].astype(BF16), proj,
                        cols["gate_gla"], cols["gate_rwkv"], bm, 512)
        mixed = _matmul(merged, w_out[l].astype(BF16), BF16, bm, 1024)
        h, u = _post_pre(h, mixed, row(mix_post_norm[l]), row(ffn2_pre_norm[l]), 1.0, bt)

        hid, w_down = _gate_up(u, ffn2_w_gate[l], ffn2_w_up[l], ffn2_w_down[l], bm_gu, 256)
        f = _matmul(hid, w_down, BF16, bm_down, 512)
        h = _post(h, f, row(ffn2_post_norm[l]), MACARON_WEIGHT, bt)
    return h.reshape(batch, seq, d_model)
```

```python
import functools

import jax
import jax.numpy as jnp
from jax import lax
from jax.experimental import pallas as pl
from jax.experimental.pallas import tpu as pltpu

F32 = jnp.float32
BF16 = jnp.bfloat16

NORM_EPS = 1e-6
RWKV_LN_EPS = 64e-5
MACARON_WEIGHT = 0.5
GLA_GATE_NORM = 16.0
GLA_GATE_RANK = 16
LOG2_E = 1.4426950408889634
RWKV_HEAD = 64

LANES = 128
ROW_TILE = 16
CHUNK = 64
PROJ_BN = 256
PROJ_BLOCKS_PER_STEP = 5
SUB_LOG2 = 3
SUB = 1 << SUB_LOG2
GLA_HEADS_PER_STEP = 2
RWKV_GROUP = 4
RWKV_GROUPS_PER_STEP = 4
VMEM_LIMIT = 56 * 1024 * 1024
VMEM_LIMIT_MAX = 60 * 1024 * 1024


def _cparams(sem):
    return pltpu.CompilerParams(dimension_semantics=sem, vmem_limit_bytes=VMEM_LIMIT)


def _dot(a, b):
    return jnp.dot(a.astype(BF16), b.astype(BF16), preferred_element_type=F32)


def _dot_nt(a, b):
    return lax.dot_general(a.astype(BF16), b.astype(BF16), (((1,), (1,)), ((), ())),
                           preferred_element_type=F32)


def _dot_tn(a, b):
    return lax.dot_general(a.astype(BF16), b.astype(BF16), (((0,), (0,)), ((), ())),
                           preferred_element_type=F32)


def _dot_split(m01, x):
    hi = x.astype(BF16)
    lo = (x - hi.astype(F32)).astype(BF16)
    return (jnp.dot(m01, hi, preferred_element_type=F32)
            + jnp.dot(m01, lo, preferred_element_type=F32))


def _rms(x, g):
    ms = jnp.mean(x * x, axis=-1, keepdims=True)
    return x * lax.rsqrt(ms + NORM_EPS) * g


def _sigmoid(x):
    return 1.0 / (1.0 + jnp.exp(-x))


def _softplus(x):
    return jnp.maximum(x, 0.0) + jnp.log(1.0 + jnp.exp(-jnp.abs(x)))


def _norm_cast_kernel(x_ref, g_ref, o_ref):
    o_ref[...] = _rms(x_ref[...], g_ref[...]).astype(o_ref.dtype)


def _norm_cast(x, g, bt):
    t, d = x.shape
    return pl.pallas_call(
        _norm_cast_kernel,
        grid=(t // bt,),
        in_specs=[pl.BlockSpec((bt, d), lambda i: (i, 0)),
                  pl.BlockSpec((1, d), lambda i: (0, 0))],
        out_specs=pl.BlockSpec((bt, d), lambda i: (i, 0)),
        out_shape=jax.ShapeDtypeStruct((t, d), BF16),
        compiler_params=_cparams(("parallel",)),
    )(x, g)


def _post_pre_kernel(h_ref, f_ref, gpost_ref, gpre_ref, hout_ref, u_ref, *, alpha):
    h = h_ref[...] + alpha * _rms(f_ref[...].astype(F32), gpost_ref[...])
    hout_ref[...] = h
    u_ref[...] = _rms(h, gpre_ref[...]).astype(u_ref.dtype)


def _post_pre(h, f, g_post, g_pre, alpha, bt):
    t, d = h.shape
    row = pl.BlockSpec((bt, d), lambda i: (i, 0))
    vec = pl.BlockSpec((1, d), lambda i: (0, 0))
    return pl.pallas_call(
        functools.partial(_post_pre_kernel, alpha=alpha),
        grid=(t // bt,),
        in_specs=[row, row, vec, vec],
        out_specs=[row, row],
        out_shape=[jax.ShapeDtypeStruct((t, d), F32), jax.ShapeDtypeStruct((t, d), BF16)],
        compiler_params=_cparams(("parallel",)),
    )(h, f, g_post, g_pre)


def _post_kernel(h_ref, f_ref, gpost_ref, hout_ref, *, alpha):
    hout_ref[...] = h_ref[...] + alpha * _rms(f_ref[...].astype(F32), gpost_ref[...])


def _post(h, f, g_post, alpha, bt):
    t, d = h.shape
    row = pl.BlockSpec((bt, d), lambda i: (i, 0))
    vec = pl.BlockSpec((1, d), lambda i: (0, 0))
    return pl.pallas_call(
        functools.partial(_post_kernel, alpha=alpha),
        grid=(t // bt,),
        in_specs=[row, row, vec],
        out_specs=row,
        out_shape=jax.ShapeDtypeStruct((t, d), F32),
        compiler_params=_cparams(("parallel",)),
    )(h, f, g_post)


def _matmul_kernel(x_ref, w_ref, o_ref):
    o_ref[...] = jnp.dot(x_ref[...], w_ref[...], preferred_element_type=F32).astype(o_ref.dtype)


def _proj_kernel(rows_ref, x_ref, *refs):
    del rows_ref
    w_refs, o_ref = refs[:-1], refs[-1]
    x = x_ref[...]
    bn = w_refs[0].shape[0]
    for b, w_ref in enumerate(w_refs):
        o_ref[:, b * bn:(b + 1) * bn] = lax.dot_general(
            x, w_ref[...], (((1,), (1,)), ((), ())), preferred_element_type=F32).astype(o_ref.dtype)


def _proj(x, w_t, row_starts, out_dtype, bm, bn, per_step):
    m, k = x.shape
    nblk = row_starts.shape[0]
    assert nblk % per_step == 0

    def w_spec(b):
        return pl.BlockSpec((pl.Element(bn), pl.Element(k)),
                            lambda i, j, rows: (pl.multiple_of(rows[j * per_step + b], ROW_TILE), 0))

    return pl.pallas_call(
        _proj_kernel,
        grid_spec=pltpu.PrefetchScalarGridSpec(
            num_scalar_prefetch=1, grid=(m // bm, nblk // per_step),
            in_specs=[pl.BlockSpec((bm, k), lambda i, j, rows: (i, 0))] + [w_spec(b) for b in range(per_step)],
            out_specs=pl.BlockSpec((bm, per_step * bn), lambda i, j, rows: (i, j))),
        out_shape=jax.ShapeDtypeStruct((m, nblk * bn), out_dtype),
        compiler_params=_cparams(("parallel", "arbitrary")),
    )(row_starts, x, *([w_t] * per_step))


def _matmul(x, w, out_dtype, bm, bn):
    m, k = x.shape
    n = w.shape[1]
    return pl.pallas_call(
        _matmul_kernel,
        grid=(m // bm, pl.cdiv(n, bn)),
        in_specs=[pl.BlockSpec((bm, k), lambda i, j: (i, 0)),
                  pl.BlockSpec((k, bn), lambda i, j: (0, j))],
        out_specs=pl.BlockSpec((bm, bn), lambda i, j: (i, j)),
        out_shape=jax.ShapeDtypeStruct((m, n), out_dtype),
        compiler_params=_cparams(("parallel", "arbitrary")),
    )(x, w)


def _matmul_cast_kernel(x_ref, w_ref, c_ref, o_ref, c_bf16_ref):
    o_ref[...] = jnp.dot(x_ref[...], w_ref[...], preferred_element_type=F32).astype(o_ref.dtype)
    c_bf16_ref[...] = c_ref[...].astype(BF16)


def _matmul_and_cast(x, w, out_dtype, bm, bn, other):
    m, k = x.shape
    n = w.shape[1]
    nj = pl.cdiv(n, bn)
    steps = (m // bm) * nj
    rows = other.shape[0]
    slab = -(-rows // (steps * ROW_TILE)) * ROW_TILE
    last = pl.cdiv(rows, slab) - 1
    slab_spec = pl.BlockSpec((slab, other.shape[1]), lambda i, j: (jnp.minimum(i * nj + j, last), 0))
    return pl.pallas_call(
        _matmul_cast_kernel,
        grid=(m // bm, nj),
        in_specs=[pl.BlockSpec((bm, k), lambda i, j: (i, 0)),
                  pl.BlockSpec((k, bn), lambda i, j: (0, j)), slab_spec],
        out_specs=[pl.BlockSpec((bm, bn), lambda i, j: (i, j)), slab_spec],
        out_shape=[jax.ShapeDtypeStruct((m, n), out_dtype), jax.ShapeDtypeStruct(other.shape, BF16)],
        compiler_params=_cparams(("arbitrary", "arbitrary")),
    )(x, w, other)


def _gate_up_kernel(x_ref, wg_ref, wu_ref, wd_ref, o_ref, wd_bf16_ref):
    x = x_ref[...]
    g = jnp.dot(x, wg_ref[...].astype(BF16), preferred_element_type=F32)
    u = jnp.dot(x, wu_ref[...].astype(BF16), preferred_element_type=F32)
    o_ref[...] = (g * _sigmoid(g) * u).astype(o_ref.dtype)
    wd_bf16_ref[...] = wd_ref[...].astype(BF16)


def _gate_up(x, wg, wu, wd, bm, bn):
    m, k = x.shape
    n = wg.shape[1]
    nj = pl.cdiv(n, bn)
    steps = (m // bm) * nj
    slab = wd.shape[0] // steps
    assert slab * steps == wd.shape[0] and slab % 16 == 0, (wd.shape, steps)
    wspec = pl.BlockSpec((k, bn), lambda i, j: (0, j))
    slab_spec = pl.BlockSpec((slab, wd.shape[1]), lambda i, j: (i * nj + j, 0))
    return pl.pallas_call(
        _gate_up_kernel,
        grid=(m // bm, nj),
        in_specs=[pl.BlockSpec((bm, k), lambda i, j: (i, 0)), wspec, wspec, slab_spec],
        out_specs=[pl.BlockSpec((bm, bn), lambda i, j: (i, j)), slab_spec],
        out_shape=[jax.ShapeDtypeStruct((m, n), BF16), jax.ShapeDtypeStruct(wd.shape, BF16)],
        compiler_params=pltpu.CompilerParams(dimension_semantics=("arbitrary", "arbitrary"),
                                             vmem_limit_bytes=VMEM_LIMIT_MAX),
    )(x, wg, wu, wd)


def _merge_kernel(og_ref, or_ref, wg_ref, wr_ref, gg_ref, gr_ref, o_ref):
    yg = jnp.dot(og_ref[...], wg_ref[...], preferred_element_type=F32)
    yr = jnp.dot(or_ref[...], wr_ref[...], preferred_element_type=F32)
    o_ref[...] = (_sigmoid(gg_ref[...]) * yg + _sigmoid(gr_ref[...]) * yr).astype(o_ref.dtype)


def _merge(o_gla, o_rwkv, w_up_gla, w_up_rwkv, proj, col_gate_gla, col_gate_rwkv, bm, bn):
    m, k = o_gla.shape
    n = w_up_gla.shape[1]
    xspec = pl.BlockSpec((bm, k), lambda i, j: (i, 0))
    wspec = pl.BlockSpec((k, bn), lambda i, j: (0, j))
    ga, gb = col_gate_gla // bn, col_gate_rwkv // bn
    return pl.pallas_call(
        _merge_kernel,
        grid=(m // bm, n // bn),
        in_specs=[xspec, xspec, wspec, wspec,
                  pl.BlockSpec((bm, bn), lambda i, j: (i, ga + j)),
                  pl.BlockSpec((bm, bn), lambda i, j: (i, gb + j))],
        out_specs=pl.BlockSpec((bm, bn), lambda i, j: (i, j)),
        out_shape=jax.ShapeDtypeStruct((m, n), BF16),
        compiler_params=_cparams(("parallel", "arbitrary")),
    )(o_gla, o_rwkv, w_up_gla, w_up_rwkv, proj, proj)


def _gla_kernel(q_ref, k_ref, v_ref, og_ref, gd_ref, gup_ref, gbias_ref, onorm_ref, tri_ref,
                o_ref, state_ref, *, dk, dv, bb, hq):
    c, nsub = CHUNK, CHUNK // SUB
    rows = bb * c

    @pl.when(pl.program_id(2) == 0)
    def _():
        state_ref[...] = jnp.zeros_like(state_ref)

    def merged(ref, lanes=slice(None)):
        return ref[:, :, lanes].reshape(rows, -1)

    def per_row(x):
        return [x[i * c:(i + 1) * c] for i in range(bb)]

    def chunk_row(x, r):
        return jnp.concatenate(
            [jnp.broadcast_to(x[i * c + r:i * c + r + 1, :], (c, x.shape[-1])) for i in range(bb)], axis=0)

    row = lax.broadcasted_iota(jnp.int32, (rows, c), 0) & (c - 1)
    lane = lax.broadcasted_iota(jnp.int32, (rows, c), 1)
    blk0 = row & -SUB
    d = lane - blk0
    dc = jnp.where(d >= 0, jnp.where(d <= (row & (SUB - 1)), d, -1), -1)
    mask_off = (lane < blk0).astype(F32)
    sub_id = (lax.broadcasted_iota(jnp.int32, (rows, dk), 0) & (c - 1)) >> SUB_LOG2
    gd = merged(gd_ref).astype(BF16)
    tri = tri_ref[...]

    qs, ks, bs, qhat, khat, b_ends = [], [], [], [], [], []
    for h in range(hq):
        lanes = slice(h * dk, (h + 1) * dk)
        x = _dot(gd, gup_ref[:, lanes]) + gbias_ref[:, lanes]
        log_a = -_softplus(-x) * (1.0 / GLA_GATE_NORM)
        b = _dot_split(tri, log_a * LOG2_E)
        q = merged(q_ref, lanes) * (dk ** -0.5)
        k = merged(k_ref, lanes)
        refs = [chunk_row(b, i * SUB - 1) for i in range(1, nsub)]
        bref = jnp.zeros((rows, dk), F32)
        for i in range(1, nsub):
            bref = jnp.where(sub_id == i, refs[i - 1], bref)
        qhat.append(per_row(q * jnp.exp2(b - bref)))
        khat.append([per_row(k * jnp.exp2(jnp.minimum(ref - b, 0.0))) for ref in refs])
        qs.append(q)
        ks.append(k)
        bs.append(b)
        b_ends.append(chunk_row(b, c - 1))

    attn = [jnp.concatenate(
        [jnp.concatenate(
            [jnp.zeros((SUB, c), F32)]
            + [_dot_nt(qhat[h][r][i * SUB:(i + 1) * SUB, :], khat[h][i - 1][r]) for i in range(1, nsub)],
            axis=0)
         for r in range(bb)], axis=0) * mask_off for h in range(hq)]

    for h in range(hq):
        q, k, b = qs[h], ks[h], bs[h]
        k3 = k.reshape(bb * nsub, SUB, dk)
        b3 = b.reshape(bb * nsub, SUB, dk)
        for j in range(SUB):
            kj = jnp.broadcast_to(k3[:, j:j + 1, :], (bb * nsub, SUB, dk)).reshape(rows, dk)
            bj = jnp.broadcast_to(b3[:, j:j + 1, :], (bb * nsub, SUB, dk)).reshape(rows, dk)
            col = jnp.sum(q * kj * jnp.exp2(jnp.minimum(b - bj, 0.0)), axis=-1, keepdims=True)
            attn[h] = jnp.where(dc == j, col, attn[h])

    chains = [(h, i) for h in range(hq) for i in range(bb)]
    attn_c = [t for h in range(hq) for t in per_row(attn[h])]
    qe = [t for h in range(hq) for t in per_row(qs[h] * jnp.exp2(bs[h]))]
    k_end = [t for h in range(hq) for t in per_row(ks[h] * jnp.exp2(b_ends[h] - bs[h]))]
    v_c = [v_ref[i, :, h * dv:(h + 1) * dv] for h, i in chains]
    st = [state_ref[i, h] for h, i in chains]
    o = [_dot(a_, v_) + _dot_nt(q_, s_) for a_, v_, q_, s_ in zip(attn_c, v_c, qe, st)]
    for j, (h, i) in enumerate(chains):
        state_ref[i, h] = st[j] * jnp.exp2(b_ends[h][i * c:i * c + 1, :]) + _dot_tn(v_c[j], k_end[j])

    for h in range(hq):
        lanes = slice(h * dv, (h + 1) * dv)
        oh = _rms(jnp.concatenate(o[h * bb:(h + 1) * bb], axis=0), onorm_ref[...])
        og = merged(og_ref, lanes)
        o_ref[:, :, lanes] = (oh * (og * _sigmoid(og))).astype(o_ref.dtype).reshape(bb, c, dv)


def _gla(proj, gate_up, gate_bias, out_norm, tri, cols, bb, heads, dk, dv):
    batch, seq, _ = proj.shape
    nc = seq // CHUNK
    hq = next(m for m in (GLA_HEADS_PER_STEP, 1) if heads % m == 0)
    kw, vw = hq * dk, hq * dv
    cq, ck, cv, cog, cgd = (cols[n] for n in ("gla_q", "gla_k", "gla_v", "gla_og", "gla_gd"))
    return pl.pallas_call(
        functools.partial(_gla_kernel, dk=dk, dv=dv, bb=bb, hq=hq),
        grid=(batch // bb, heads // hq, nc),
        in_specs=[
            pl.BlockSpec((bb, CHUNK, kw), lambda b, h, c: (b, c, cq // kw + h)),
            pl.BlockSpec((bb, CHUNK, kw), lambda b, h, c: (b, c, ck // kw + h)),
            pl.BlockSpec((bb, CHUNK, vw), lambda b, h, c: (b, c, cv // vw + h)),
            pl.BlockSpec((bb, CHUNK, vw), lambda b, h, c: (b, c, cog // vw + h)),
            pl.BlockSpec((bb, CHUNK, LANES), lambda b, h, c: (b, c, cgd // LANES)),
            pl.BlockSpec((LANES, kw), lambda b, h, c: (0, h)),
            pl.BlockSpec((1, kw), lambda b, h, c: (0, h)),
            pl.BlockSpec((1, dv), lambda b, h, c: (0, 0)),
            pl.BlockSpec((bb * CHUNK, bb * CHUNK), lambda b, h, c: (0, 0)),
        ],
        out_specs=pl.BlockSpec((bb, CHUNK, vw), lambda b, h, c: (b, c, h)),
        out_shape=jax.ShapeDtypeStruct((batch, seq, heads * dv), BF16),
        scratch_shapes=[pltpu.VMEM((bb, hq, dv, dk), F32)],
        compiler_params=_cparams(("parallel", "parallel", "arbitrary")),
    )(proj, proj, proj, proj, proj, gate_up, gate_bias, out_norm, tri)


def _rwkv_kernel(r_ref, k_ref, v_ref, gd_ref, wd_ref, ad_ref, rp_ref, kp_ref, vp_ref, gdp_ref, wdp_ref, adp_ref,
                 mur_ref, muk_ref, muv_ref, mugd_ref, muwd_ref, muad_ref,
                 w0_ref, w2_ref, a0_ref, a2_ref, g2_ref, kk_ref, ka_ref, rk_ref, lnw_ref, lnb_ref,
                 tri_ref, ones_ref, mstrict_ref, mincl_ref, mblk_ref,
                 o_ref, state_ref, *, bb, gq):
    c, g = CHUNK, RWKV_GROUP
    n, rows, gw = g * c, bb * c, g * RWKV_HEAD
    first = pl.program_id(2) == 0

    @pl.when(first)
    def _():
        state_ref[...] = jnp.zeros_like(state_ref)

    keep_prev = jnp.where(first, 0.0, 1.0)

    def per_row(x):
        return [x[i * c:(i + 1) * c] for i in range(bb)]

    def shifted(cur_ref, prev_ref, mu_ref, lanes):
        p = cur_ref[:, :, lanes].reshape(rows, -1)
        width = p.shape[-1]
        prev = jnp.concatenate(
            [jnp.broadcast_to(prev_ref[i, 7:8, lanes] * keep_prev, (c, width)) for i in range(bb)], axis=0)
        is_row0 = (lax.broadcasted_iota(jnp.int32, p.shape, 0) & (c - 1)) == 0
        p_prev = jnp.where(is_row0, prev, pltpu.roll(p, 1, axis=0))
        return p + (p_prev - p) * mu_ref[:, lanes]

    everything = slice(None)
    tanh_wd = jnp.tanh(shifted(wd_ref, wdp_ref, muwd_ref, everything)).astype(BF16)
    a_down = shifted(ad_ref, adp_ref, muad_ref, everything).astype(BF16)
    sig_gd = _sigmoid(shifted(gd_ref, gdp_ref, mugd_ref, everything)).astype(BF16)

    ones_bd = ones_ref[...]
    m_blk = mblk_ref[...]
    m_blk_bf16 = m_blk.astype(BF16)
    tri = tri_ref[...]

    r_all, k2_all, v_all, gate_all = [], [], [], []
    rt, kkt, km, bm, k_end, b_end, v_rows, c_end = [], [], [], [], [], [], [], []
    for q in range(gq):
        lanes = slice(q * gw, (q + 1) * gw)
        r = shifted(r_ref, rp_ref, mur_ref, lanes)
        k = shifted(k_ref, kp_ref, muk_ref, lanes)
        v = shifted(v_ref, vp_ref, muv_ref, lanes)
        w = -_softplus(-(w0_ref[:, lanes] + _dot(tanh_wd, w2_ref[:, lanes]))) - 0.5
        ld = -jnp.exp(w)
        a = _sigmoid(a0_ref[:, lanes] + _dot(a_down, a2_ref[:, lanes]))
        gate_all.append(_dot(sig_gd, g2_ref[:, lanes]))

        kkr = k * kk_ref[:, lanes]
        kk = kkr * lax.rsqrt(jnp.maximum(_dot(kkr * kkr, ones_bd), 1e-24))
        k2 = k * (1.0 + (a - 1.0) * ka_ref[:, lanes])
        bw = a * kk

        cum = _dot_split(tri, ld)
        ends = [cum[(i + 1) * c - 1:(i + 1) * c, :] for i in range(bb)]
        c_end_rows = jnp.concatenate([jnp.broadcast_to(e, (c, gw)) for e in ends], axis=0)
        e_neg = jnp.exp(-cum)
        e_end = jnp.exp(c_end_rows - cum)
        c_end += ends
        rt += per_row(r * jnp.exp(cum))
        kkt += per_row(kk * jnp.exp(cum - ld))
        km += per_row(k2 * e_neg)
        bm += per_row(bw * e_neg)
        k_end += per_row(k2 * e_end)
        b_end += per_row(bw * e_end)
        v_rows += per_row(v)
        r_all.append(r)
        k2_all.append(k2)
        v_all.append(v)

    def stack(x):
        return jnp.concatenate([x.astype(BF16)] * g, axis=0) * m_blk_bf16

    m_strict, m_incl = mstrict_ref[...], mincl_ref[...]
    lhs = [jnp.concatenate([kq, rq], axis=0).astype(BF16) for kq, rq in zip(kkt, rt)]
    aa = [_dot_nt(l, jnp.concatenate([stack(b_), stack(k_)], axis=0))
          for l, b_, k_ in zip(lhs, bm, km)]
    a_ab = [t[:c, :n] * m_strict for t in aa]
    a_kr = [jnp.concatenate([t[:c, n:] * m_strict, t[c:, n:] * m_incl], axis=0) for t in aa]
    a_rb = [t[c:, :n] * m_incl for t in aa]

    eye = m_incl - m_strict
    x = [eye - t for t in a_ab]
    p = [_dot(t, stack(t)) for t in a_ab]
    span = 2
    while span < c:
        span *= 2
        if span < c:
            xp = [_dot(jnp.concatenate([xi, pi], axis=0), stack(pi)) for xi, pi in zip(x, p)]
            x = [xi + t[:c] for xi, t in zip(x, xp)]
            p = [t[c:] for t in xp]
        else:
            x = [xi + _dot(xi, stack(pi)) for xi, pi in zip(x, p)]

    chains = [(q, i) for q in range(gq) for i in range(bb)]
    st = [state_ref[i, q] for q, i in chains]
    sk = [_dot_nt(l, s) for l, s in zip(lhs, st)]
    av = [_dot(t, stack(vi)) for t, vi in zip(a_kr, v_rows)]
    u = [_dot(xi, stack(-(s[:c] + t[:c]))) for xi, s, t in zip(x, sk, av)]
    y = [s[c:] + _dot(t, stack(ui)) + w_[c:] for s, t, ui, w_ in zip(sk, a_rb, u, av)]
    for j, (q, i) in enumerate(chains):
        state_ref[i, q] = st[j] * jnp.exp(c_end[j]) + _dot_tn(
            jnp.concatenate([u[j], v_rows[j]], axis=0),
            jnp.concatenate([b_end[j], k_end[j]], axis=0)) * m_blk

    inv_n = 1.0 / RWKV_HEAD
    for q in range(gq):
        lanes = slice(q * gw, (q + 1) * gw)
        yq = jnp.concatenate(y[q * bb:(q + 1) * bb], axis=0)
        mu = _dot(yq, ones_bd) * inv_n
        dy = yq - mu
        var = _dot(dy * dy, ones_bd) * inv_n
        yn = dy * lax.rsqrt(var + RWKV_LN_EPS) * lnw_ref[:, lanes] + lnb_ref[:, lanes]
        bonus = _dot(r_all[q] * k2_all[q] * rk_ref[:, lanes], ones_bd) * v_all[q]
        o_ref[:, :, lanes] = ((yn + bonus) * gate_all[q]).astype(o_ref.dtype).reshape(bb, c, gw)


def _rwkv(proj, mus, w0, w2, a0, a2, g2, k_k, k_a, r_k, ln_w, ln_b, consts, cols, bb, width, lora):
    batch, seq, _ = proj.shape
    nc = seq // CHUNK
    gw = RWKV_GROUP * RWKV_HEAD
    gq = next(m for m in (RWKV_GROUPS_PER_STEP, 1) if (width // gw) % m == 0)
    sw = gq * gw
    steps = width // sw
    gdw = g2.shape[0]

    def prev_rows(c):
        return jnp.maximum(c * (CHUNK // 8) - 1, 0)

    def cur(w_, col, per_group):
        if per_group:
            return pl.BlockSpec((bb, CHUNK, w_), lambda b, q, c: (b, c, col // w_ + q))
        return pl.BlockSpec((bb, CHUNK, w_), lambda b, q, c: (b, c, col // w_))

    def prev(w_, col, per_group):
        if per_group:
            return pl.BlockSpec((bb, 8, w_), lambda b, q, c: (b, prev_rows(c), col // w_ + q))
        return pl.BlockSpec((bb, 8, w_), lambda b, q, c: (b, prev_rows(c), col // w_))

    def gvec():
        return pl.BlockSpec((1, sw), lambda b, q, c: (0, q))

    def full(shape):
        return pl.BlockSpec(shape, lambda b, q, c: (0, 0))

    n = RWKV_GROUP * CHUNK
    in_specs = [
        cur(sw, cols["rw_r"], True), cur(sw, cols["rw_k"], True), cur(sw, cols["rw_v"], True),
        cur(gdw, cols["rw_gd"], False), cur(lora, cols["rw_wd"], False), cur(lora, cols["rw_ad"], False),
        prev(sw, cols["rw_r"], True), prev(sw, cols["rw_k"], True), prev(sw, cols["rw_v"], True),
        prev(gdw, cols["rw_gd"], False), prev(lora, cols["rw_wd"], False), prev(lora, cols["rw_ad"], False),
        gvec(), gvec(), gvec(), full((1, gdw)), full((1, lora)), full((1, lora)),
        gvec(), pl.BlockSpec((lora, sw), lambda b, q, c: (0, q)),
        gvec(), pl.BlockSpec((lora, sw), lambda b, q, c: (0, q)),
        pl.BlockSpec((gdw, sw), lambda b, q, c: (0, q)),
        gvec(), gvec(), gvec(), gvec(), gvec(),
        full((bb * CHUNK, bb * CHUNK)), full((gw, gw)), full((CHUNK, n)), full((CHUNK, n)), full((n, gw)),
    ]
    return pl.pallas_call(
        functools.partial(_rwkv_kernel, bb=bb, gq=gq),
        grid=(batch // bb, steps, nc),
        in_specs=in_specs,
        out_specs=pl.BlockSpec((bb, CHUNK, sw), lambda b, q, c: (b, c, q)),
        out_shape=jax.ShapeDtypeStruct((batch, seq, width), BF16),
        scratch_shapes=[pltpu.VMEM((bb, gq, gw, gw), F32)],
        compiler_params=_cparams(("parallel", "parallel", "arbitrary")),
    )(*([proj] * 12),
      mus["r"], mus["k"], mus["v"], mus["gd"], mus["wd"], mus["ad"],
      w0, w2, a0, a2, g2, k_k, k_a, r_k, ln_w, ln_b,
      consts["tri_rows"], consts["ones_bd"], consts["m_strict"], consts["m_incl"], consts["m_blk"])


def _pad_rows(w, n):
    return jnp.pad(w, ((0, n - w.shape[0]), (0, 0)))


def _layout(d_model, lora, g_lora):
    kw, vw, rw = d_model // 4, d_model // 2, d_model // 2
    o_rw = 2 * kw + 2 * vw + GLA_GATE_RANK
    o_gate = o_rw + 3 * rw + 2 * lora + g_lora
    segments = [("gla_q", kw, 0), ("gla_k", kw, kw), ("gla_v", vw, 2 * kw),
                ("gla_og", vw, 2 * kw + vw + GLA_GATE_RANK),
                ("rw_r", rw, o_rw), ("rw_k", rw, o_rw + rw + lora), ("rw_v", rw, o_rw + 2 * rw + lora),
                ("gate_gla", d_model, o_gate), ("gate_rwkv", d_model, o_gate + d_model),
                ("rw_gd", 2 * PROJ_BN, o_rw + 3 * rw + 2 * lora),
                ("rw_wd", PROJ_BN, o_rw + rw), ("rw_ad", PROJ_BN, o_rw + 3 * rw + lora),
                ("gla_gd", PROJ_BN, 2 * kw + vw)]
    n_src = o_gate + 2 * d_model
    cols, row_starts, off = {}, [], 0
    for name, w, src in segments:
        assert off % w == 0 and w % PROJ_BN == 0 and src % ROW_TILE == 0 and src + w <= n_src, (name, off, w, src)
        cols[name] = off
        row_starts += [src + b for b in range(0, w, PROJ_BN)]
        off += w
    return cols, off, row_starts


def _chunk_consts(bb):
    c, g, hn = CHUNK, RWKV_GROUP, RWKV_HEAD
    assert c == hn
    n = g * c
    i = jnp.arange(n)
    ti, tj = jnp.arange(c)[:, None], i[None, :] % c
    lane = jnp.arange(g * hn)
    return {
        "tri_rows": jnp.kron(jnp.eye(bb, dtype=F32), jnp.tril(jnp.ones((c, c), F32))).astype(BF16),
        "ones_bd": ((lane[:, None] // hn) == (lane[None, :] // hn)).astype(BF16),
        "m_strict": (ti > tj).astype(F32),
        "m_incl": (ti >= tj).astype(F32),
        "m_blk": ((i[:, None] // c) == (lane[None, :] // hn)).astype(F32),
    }


def kernel(x, ffn1_pre_norm, ffn1_w_gate, ffn1_w_up, ffn1_w_down, ffn1_post_norm, mix_pre_norm, w_in, gla_gate_up, gla_gate_bias, gla_out_norm, rwkv_shift_mix, rwkv_w0, rwkv_w2, rwkv_a0, rwkv_a2, rwkv_g2, rwkv_k_k, rwkv_k_a, rwkv_r_k, rwkv_ln_w, rwkv_ln_b, w_up_gla, w_up_rwkv, w_out, mix_post_norm, ffn2_pre_norm, ffn2_w_gate, ffn2_w_up, ffn2_w_down, ffn2_post_norm):
    batch, seq, d_model = x.shape
    depth = ffn1_pre_norm.shape[0]
    t = batch * seq
    assert seq % CHUNK == 0 and d_model % 1024 == 0

    gla_kw, gla_vw, rw_w = d_model // 4, d_model // 2, d_model // 2
    gla_heads = max(4, d_model // 512)
    dk, dv = gla_kw // gla_heads, gla_vw // gla_heads
    lora = rwkv_w2.shape[1]
    g_lora = rwkv_g2.shape[1]
    assert lora == LANES and g_lora <= 2 * PROJ_BN and dk % LANES == 0

    cols, n_proj, row_starts = _layout(d_model, lora, g_lora)
    bm = min(1024, t)
    bm_down = min(512, t)
    bm_gu = min(2048, t)
    bt = min(256, t)
    bb = next(n for n in (4, 2, 1) if batch % n == 0)
    consts = _chunk_consts(bb)

    def row(v):
        return v.reshape(1, -1).astype(F32)

    h = x.reshape(t, d_model)
    for l in range(depth):
        xn = _norm_cast(h, row(ffn1_pre_norm[l]), bt)
        hid, w_down = _gate_up(xn, ffn1_w_gate[l], ffn1_w_up[l], ffn1_w_down[l], bm_gu, 256)
        f, w_in_t = _matmul_and_cast(hid, w_down, BF16, bm_down, 512, jnp.swapaxes(w_in[l], 0, 1))
        h, u = _post_pre(h, f, row(ffn1_post_norm[l]), row(mix_pre_norm[l]), MACARON_WEIGHT, bt)

        proj = _proj(u, w_in_t, jnp.asarray(row_starts, jnp.int32), F32, bm, PROJ_BN, PROJ_BLOCKS_PER_STEP)

        mu = rwkv_shift_mix[l]
        mu_off = [0]
        for s in (rw_w, lora, rw_w, rw_w, lora, g_lora):
            mu_off.append(mu_off[-1] + s)
        mu_r, mu_wd, mu_k, mu_v, mu_ad, mu_gd = (mu[mu_off[i]:mu_off[i + 1]] for i in range(6))
        mus = {"r": row(mu_r), "k": row(mu_k), "v": row(mu_v),
               "gd": row(jnp.pad(mu_gd, (0, 2 * PROJ_BN - g_lora))), "wd": row(mu_wd), "ad": row(mu_ad)}

        proj3 = proj.reshape(batch, seq, n_proj)
        o_gla = _gla(proj3, _pad_rows(gla_gate_up[l], LANES).astype(BF16), row(gla_gate_bias[l]),
                     row(gla_out_norm[l]), consts["tri_rows"], cols, bb, gla_heads, dk, dv)
        o_rwkv = _rwkv(proj3, mus, row(rwkv_w0[l]), rwkv_w2[l].astype(BF16), row(rwkv_a0[l]),
                       rwkv_a2[l].astype(BF16), _pad_rows(rwkv_g2[l], 2 * PROJ_BN).astype(BF16),
                       row(rwkv_k_k[l]), row(rwkv_k_a[l]), row(rwkv_r_k[l]), row(rwkv_ln_w[l]),
                       row(rwkv_ln_b[l]), consts, cols, bb, rw_w, lora)

        merged = _merge(o_gla.reshape(t, gla_vw), o_rwkv.reshape(t, rw_w),
                        w_up_gla[l].astype(BF16), w_up_rwkv[l].astype(BF16), proj,
                        cols["gate_gla"], cols["gate_rwkv"], bm, 512)
        mixed = _matmul(merged, w_out[l].astype(BF16), BF16, bm, 1024)
        h, u = _post_pre(h, mixed, row(mix_post_norm[l]), row(ffn2_pre_norm[l]), 1.0, bt)

        hid, w_down = _gate_up(u, ffn2_w_gate[l], ffn2_w_up[l], ffn2_w_down[l], bm_gu, 256)
        f = _matmul(hid, w_down, BF16, bm_down, 512)
        h = _post(h, f, row(ffn2_post_norm[l]), MACARON_WEIGHT, bt)
    return h.reshape(batch, seq, d_model)
```

```python
import functools

import jax
import jax.numpy as jnp
from jax import lax
from jax.experimental import pallas as pl
from jax.experimental.pallas import tpu as pltpu

F32 = jnp.float32
BF16 = jnp.bfloat16

NORM_EPS = 1e-6
RWKV_LN_EPS = 64e-5
MACARON_WEIGHT = 0.5
GLA_GATE_NORM = 16.0
GLA_GATE_RANK = 16
LOG2_E = 1.4426950408889634
RWKV_HEAD = 64

LANES = 128
ROW_TILE = 16
CHUNK = 64
PROJ_BN = 256
PROJ_BLOCKS_PER_STEP = 5
SUB_LOG2 = 3
SUB = 1 << SUB_LOG2
GLA_HEADS_PER_STEP = 4
RWKV_GROUP = 4
RWKV_GROUPS_PER_STEP = 4
VMEM_LIMIT = 56 * 1024 * 1024
VMEM_LIMIT_MAX = 60 * 1024 * 1024


def _cparams(sem):
    return pltpu.CompilerParams(dimension_semantics=sem, vmem_limit_bytes=VMEM_LIMIT)


def _dot(a, b):
    return jnp.dot(a.astype(BF16), b.astype(BF16), preferred_element_type=F32)


def _dot_nt(a, b):
    return lax.dot_general(a.astype(BF16), b.astype(BF16), (((1,), (1,)), ((), ())),
                           preferred_element_type=F32)


def _dot_tn(a, b):
    return lax.dot_general(a.astype(BF16), b.astype(BF16), (((0,), (0,)), ((), ())),
                           preferred_element_type=F32)


def _dot_split(m01, x):
    hi = x.astype(BF16)
    lo = (x - hi.astype(F32)).astype(BF16)
    return (jnp.dot(m01, hi, preferred_element_type=F32)
            + jnp.dot(m01, lo, preferred_element_type=F32))


def _rms(x, g):
    ms = jnp.mean(x * x, axis=-1, keepdims=True)
    return x * lax.rsqrt(ms + NORM_EPS) * g


def _sigmoid(x):
    return 1.0 / (1.0 + jnp.exp(-x))


def _softplus(x):
    return jnp.maximum(x, 0.0) + jnp.log(1.0 + jnp.exp(-jnp.abs(x)))


def _norm_cast_kernel(x_ref, g_ref, o_ref):
    o_ref[...] = _rms(x_ref[...], g_ref[...]).astype(o_ref.dtype)


def _norm_cast(x, g, bt):
    t, d = x.shape
    return pl.pallas_call(
        _norm_cast_kernel,
        grid=(t // bt,),
        in_specs=[pl.BlockSpec((bt, d), lambda i: (i, 0)),
                  pl.BlockSpec((1, d), lambda i: (0, 0))],
        out_specs=pl.BlockSpec((bt, d), lambda i: (i, 0)),
        out_shape=jax.ShapeDtypeStruct((t, d), BF16),
        compiler_params=_cparams(("parallel",)),
    )(x, g)


def _post_pre_kernel(h_ref, f_ref, gpost_ref, gpre_ref, hout_ref, u_ref, *, alpha):
    h = h_ref[...] + alpha * _rms(f_ref[...].astype(F32), gpost_ref[...])
    hout_ref[...] = h
    u_ref[...] = _rms(h, gpre_ref[...]).astype(u_ref.dtype)


def _post_pre(h, f, g_post, g_pre, alpha, bt):
    t, d = h.shape
    row = pl.BlockSpec((bt, d), lambda i: (i, 0))
    vec = pl.BlockSpec((1, d), lambda i: (0, 0))
    return pl.pallas_call(
        functools.partial(_post_pre_kernel, alpha=alpha),
        grid=(t // bt,),
        in_specs=[row, row, vec, vec],
        out_specs=[row, row],
        out_shape=[jax.ShapeDtypeStruct((t, d), F32), jax.ShapeDtypeStruct((t, d), BF16)],
        compiler_params=pltpu.CompilerParams(dimension_semantics=("parallel",), vmem_limit_bytes=VMEM_LIMIT_MAX),
    )(h, f, g_post, g_pre)


def _post_kernel(h_ref, f_ref, gpost_ref, hout_ref, *, alpha):
    hout_ref[...] = h_ref[...] + alpha * _rms(f_ref[...].astype(F32), gpost_ref[...])


def _post(h, f, g_post, alpha, bt):
    t, d = h.shape
    row = pl.BlockSpec((bt, d), lambda i: (i, 0))
    vec = pl.BlockSpec((1, d), lambda i: (0, 0))
    return pl.pallas_call(
        functools.partial(_post_kernel, alpha=alpha),
        grid=(t // bt,),
        in_specs=[row, row, vec],
        out_specs=row,
        out_shape=jax.ShapeDtypeStruct((t, d), F32),
        compiler_params=_cparams(("parallel",)),
    )(h, f, g_post)


def _matmul_kernel(x_ref, w_ref, o_ref):
    o_ref[...] = jnp.dot(x_ref[...], w_ref[...], preferred_element_type=F32).astype(o_ref.dtype)


def _proj_kernel(rows_ref, x_ref, *refs):
    del rows_ref
    w_refs, o_ref = refs[:-1], refs[-1]
    x = x_ref[...]
    bn = w_refs[0].shape[0]
    for b, w_ref in enumerate(w_refs):
        o_ref[:, b * bn:(b + 1) * bn] = lax.dot_general(
            x, w_ref[...], (((1,), (1,)), ((), ())), preferred_element_type=F32).astype(o_ref.dtype)


def _proj(x, w_t, row_starts, out_dtype, bm, bn, per_step):
    m, k = x.shape
    nblk = row_starts.shape[0]
    assert nblk % per_step == 0

    def w_spec(b):
        return pl.BlockSpec((pl.Element(bn), pl.Element(k)),
                            lambda i, j, rows: (pl.multiple_of(rows[j * per_step + b], ROW_TILE), 0))

    return pl.pallas_call(
        _proj_kernel,
        grid_spec=pltpu.PrefetchScalarGridSpec(
            num_scalar_prefetch=1, grid=(m // bm, nblk // per_step),
            in_specs=[pl.BlockSpec((bm, k), lambda i, j, rows: (i, 0))] + [w_spec(b) for b in range(per_step)],
            out_specs=pl.BlockSpec((bm, per_step * bn), lambda i, j, rows: (i, j))),
        out_shape=jax.ShapeDtypeStruct((m, nblk * bn), out_dtype),
        compiler_params=_cparams(("parallel", "arbitrary")),
    )(row_starts, x, *([w_t] * per_step))


def _matmul(x, w, out_dtype, bm, bn):
    m, k = x.shape
    n = w.shape[1]
    return pl.pallas_call(
        _matmul_kernel,
        grid=(m // bm, pl.cdiv(n, bn)),
        in_specs=[pl.BlockSpec((bm, k), lambda i, j: (i, 0)),
                  pl.BlockSpec((k, bn), lambda i, j: (0, j))],
        out_specs=pl.BlockSpec((bm, bn), lambda i, j: (i, j)),
        out_shape=jax.ShapeDtypeStruct((m, n), out_dtype),
        compiler_params=_cparams(("parallel", "arbitrary")),
    )(x, w)


def _matmul_cast_kernel(x_ref, w_ref, c_ref, o_ref, c_bf16_ref):
    o_ref[...] = jnp.dot(x_ref[...], w_ref[...], preferred_element_type=F32).astype(o_ref.dtype)
    c_bf16_ref[...] = c_ref[...].astype(BF16)


def _matmul_and_cast(x, w, out_dtype, bm, bn, other):
    m, k = x.shape
    n = w.shape[1]
    nj = pl.cdiv(n, bn)
    steps = (m // bm) * nj
    rows = other.shape[0]
    slab = -(-rows // (steps * ROW_TILE)) * ROW_TILE
    last = pl.cdiv(rows, slab) - 1
    slab_spec = pl.BlockSpec((slab, other.shape[1]), lambda i, j: (jnp.minimum(i * nj + j, last), 0))
    return pl.pallas_call(
        _matmul_cast_kernel,
        grid=(m // bm, nj),
        in_specs=[pl.BlockSpec((bm, k), lambda i, j: (i, 0)),
                  pl.BlockSpec((k, bn), lambda i, j: (0, j)), slab_spec],
        out_specs=[pl.BlockSpec((bm, bn), lambda i, j: (i, j)), slab_spec],
        out_shape=[jax.ShapeDtypeStruct((m, n), out_dtype), jax.ShapeDtypeStruct(other.shape, BF16)],
        compiler_params=_cparams(("arbitrary", "arbitrary")),
    )(x, w, other)


def _gate_up_kernel(x_ref, wg_ref, wu_ref, wd_ref, o_ref, wd_bf16_ref):
    x = x_ref[...]
    g = jnp.dot(x, wg_ref[...].astype(BF16), preferred_element_type=F32)
    u = jnp.dot(x, wu_ref[...].astype(BF16), preferred_element_type=F32)
    o_ref[...] = (g * _sigmoid(g) * u).astype(o_ref.dtype)
    wd_bf16_ref[...] = wd_ref[...].astype(BF16)


def _gate_up(x, wg, wu, wd, bm, bn):
    m, k = x.shape
    n = wg.shape[1]
    nj = pl.cdiv(n, bn)
    steps = (m // bm) * nj
    slab = wd.shape[0] // steps
    assert slab * steps == wd.shape[0] and slab % 16 == 0, (wd.shape, steps)
    wspec = pl.BlockSpec((k, bn), lambda i, j: (0, j))
    slab_spec = pl.BlockSpec((slab, wd.shape[1]), lambda i, j: (i * nj + j, 0))
    return pl.pallas_call(
        _gate_up_kernel,
        grid=(m // bm, nj),
        in_specs=[pl.BlockSpec((bm, k), lambda i, j: (i, 0)), wspec, wspec, slab_spec],
        out_specs=[pl.BlockSpec((bm, bn), lambda i, j: (i, j)), slab_spec],
        out_shape=[jax.ShapeDtypeStruct((m, n), BF16), jax.ShapeDtypeStruct(wd.shape, BF16)],
        compiler_params=pltpu.CompilerParams(dimension_semantics=("arbitrary", "arbitrary"),
                                             vmem_limit_bytes=VMEM_LIMIT_MAX),
    )(x, wg, wu, wd)


def _merge_kernel(og_ref, or_ref, wg_ref, wr_ref, gg_ref, gr_ref, o_ref):
    yg = jnp.dot(og_ref[...], wg_ref[...], preferred_element_type=F32)
    yr = jnp.dot(or_ref[...], wr_ref[...], preferred_element_type=F32)
    o_ref[...] = (_sigmoid(gg_ref[...]) * yg + _sigmoid(gr_ref[...]) * yr).astype(o_ref.dtype)


def _merge(o_gla, o_rwkv, w_up_gla, w_up_rwkv, proj, col_gate_gla, col_gate_rwkv, bm, bn):
    m, k = o_gla.shape
    n = w_up_gla.shape[1]
    xspec = pl.BlockSpec((bm, k), lambda i, j: (i, 0))
    wspec = pl.BlockSpec((k, bn), lambda i, j: (0, j))
    ga, gb = col_gate_gla // bn, col_gate_rwkv // bn
    return pl.pallas_call(
        _merge_kernel,
        grid=(m // bm, n // bn),
        in_specs=[xspec, xspec, wspec, wspec,
                  pl.BlockSpec((bm, bn), lambda i, j: (i, ga + j)),
                  pl.BlockSpec((bm, bn), lambda i, j: (i, gb + j))],
        out_specs=pl.BlockSpec((bm, bn), lambda i, j: (i, j)),
        out_shape=jax.ShapeDtypeStruct((m, n), BF16),
        compiler_params=_cparams(("parallel", "arbitrary")),
    )(o_gla, o_rwkv, w_up_gla, w_up_rwkv, proj, proj)


def _gla_kernel(q_ref, k_ref, v_ref, og_ref, gd_ref, gup_ref, gbias_ref, onorm_ref, tri_ref,
                o_ref, state_ref, *, dk, dv, bb, hq):
    c, nsub = CHUNK, CHUNK // SUB
    rows = bb * c

    @pl.when(pl.program_id(2) == 0)
    def _():
        state_ref[...] = jnp.zeros_like(state_ref)

    def merged(ref, lanes=slice(None)):
        return ref[:, :, lanes].reshape(rows, -1)

    def per_row(x):
        return [x[i * c:(i + 1) * c] for i in range(bb)]

    def chunk_row(x, r):
        return jnp.concatenate(
            [jnp.broadcast_to(x[i * c + r:i * c + r + 1, :], (c, x.shape[-1])) for i in range(bb)], axis=0)

    row = lax.broadcasted_iota(jnp.int32, (rows, c), 0) & (c - 1)
    lane = lax.broadcasted_iota(jnp.int32, (rows, c), 1)
    blk0 = row & -SUB
    d = lane - blk0
    dc = jnp.where(d >= 0, jnp.where(d <= (row & (SUB - 1)), d, -1), -1)
    mask_off = (lane < blk0).astype(F32)
    sub_id = (lax.broadcasted_iota(jnp.int32, (rows, dk), 0) & (c - 1)) >> SUB_LOG2
    gd = merged(gd_ref).astype(BF16)
    tri = tri_ref[...]

    qs, ks, bs, qhat, khat, b_ends = [], [], [], [], [], []
    for h in range(hq):
        lanes = slice(h * dk, (h + 1) * dk)
        x = _dot(gd, gup_ref[:, lanes]) + gbias_ref[:, lanes]
        log_a = -_softplus(-x) * (1.0 / GLA_GATE_NORM)
        b = _dot_split(tri, log_a * LOG2_E)
        q = merged(q_ref, lanes) * (dk ** -0.5)
        k = merged(k_ref, lanes)
        refs = [chunk_row(b, i * SUB - 1) for i in range(1, nsub)]
        bref = jnp.zeros((rows, dk), F32)
        for i in range(1, nsub):
            bref = jnp.where(sub_id == i, refs[i - 1], bref)
        qhat.append(per_row(q * jnp.exp2(b - bref)))
        khat.append([per_row(k * jnp.exp2(jnp.minimum(ref - b, 0.0))) for ref in refs])
        qs.append(q)
        ks.append(k)
        bs.append(b)
        b_ends.append(chunk_row(b, c - 1))

    attn = [jnp.concatenate(
        [jnp.concatenate(
            [jnp.zeros((SUB, c), F32)]
            + [_dot_nt(qhat[h][r][i * SUB:(i + 1) * SUB, :], khat[h][i - 1][r]) for i in range(1, nsub)],
            axis=0)
         for r in range(bb)], axis=0) * mask_off for h in range(hq)]

    for h in range(hq):
        q, k, b = qs[h], ks[h], bs[h]
        k3 = k.reshape(bb * nsub, SUB, dk)
        b3 = b.reshape(bb * nsub, SUB, dk)
        for j in range(SUB):
            kj = jnp.broadcast_to(k3[:, j:j + 1, :], (bb * nsub, SUB, dk)).reshape(rows, dk)
            bj = jnp.broadcast_to(b3[:, j:j + 1, :], (bb * nsub, SUB, dk)).reshape(rows, dk)
            col = jnp.sum(q * kj * jnp.exp2(jnp.minimum(b - bj, 0.0)), axis=-1, keepdims=True)
            attn[h] = jnp.where(dc == j, col, attn[h])

    chains = [(h, i) for h in range(hq) for i in range(bb)]
    attn_c = [t for h in range(hq) for t in per_row(attn[h])]
    qe = [t for h in range(hq) for t in per_row(qs[h] * jnp.exp2(bs[h]))]
    k_end = [t for h in range(hq) for t in per_row(ks[h] * jnp.exp2(b_ends[h] - bs[h]))]
    v_c = [v_ref[i, :, h * dv:(h + 1) * dv] for h, i in chains]
    st = [state_ref[i, h] for h, i in chains]
    o = [_dot(a_, v_) + _dot_nt(q_, s_) for a_, v_, q_, s_ in zip(attn_c, v_c, qe, st)]
    for j, (h, i) in enumerate(chains):
        state_ref[i, h] = st[j] * jnp.exp2(b_ends[h][i * c:i * c + 1, :]) + _dot_tn(v_c[j], k_end[j])

    for h in range(hq):
        lanes = slice(h * dv, (h + 1) * dv)
        oh = _rms(jnp.concatenate(o[h * bb:(h + 1) * bb], axis=0), onorm_ref[...])
        og = merged(og_ref, lanes)
        o_ref[:, :, lanes] = (oh * (og * _sigmoid(og))).astype(o_ref.dtype).reshape(bb, c, dv)


def _gla(proj, gate_up, gate_bias, out_norm, tri, cols, bb, heads, dk, dv):
    batch, seq, _ = proj.shape
    nc = seq // CHUNK
    hq = next(m for m in (GLA_HEADS_PER_STEP, 1) if heads % m == 0)
    kw, vw = hq * dk, hq * dv
    cq, ck, cv, cog, cgd = (cols[n] for n in ("gla_q", "gla_k", "gla_v", "gla_og", "gla_gd"))
    return pl.pallas_call(
        functools.partial(_gla_kernel, dk=dk, dv=dv, bb=bb, hq=hq),
        grid=(batch // bb, heads // hq, nc),
        in_specs=[
            pl.BlockSpec((bb, CHUNK, kw), lambda b, h, c: (b, c, cq // kw + h)),
            pl.BlockSpec((bb, CHUNK, kw), lambda b, h, c: (b, c, ck // kw + h)),
            pl.BlockSpec((bb, CHUNK, vw), lambda b, h, c: (b, c, cv // vw + h)),
            pl.BlockSpec((bb, CHUNK, vw), lambda b, h, c: (b, c, cog // vw + h)),
            pl.BlockSpec((bb, CHUNK, LANES), lambda b, h, c: (b, c, cgd // LANES)),
            pl.BlockSpec((LANES, kw), lambda b, h, c: (0, h)),
            pl.BlockSpec((1, kw), lambda b, h, c: (0, h)),
            pl.BlockSpec((1, dv), lambda b, h, c: (0, 0)),
            pl.BlockSpec((bb * CHUNK, bb * CHUNK), lambda b, h, c: (0, 0)),
        ],
        out_specs=pl.BlockSpec((bb, CHUNK, vw), lambda b, h, c: (b, c, h)),
        out_shape=jax.ShapeDtypeStruct((batch, seq, heads * dv), BF16),
        scratch_shapes=[pltpu.VMEM((bb, hq, dv, dk), F32)],
        compiler_params=_cparams(("parallel", "parallel", "arbitrary")),
    )(proj, proj, proj, proj, proj, gate_up, gate_bias, out_norm, tri)


def _rwkv_kernel(r_ref, k_ref, v_ref, gd_ref, wd_ref, ad_ref, rp_ref, kp_ref, vp_ref, gdp_ref, wdp_ref, adp_ref,
                 mur_ref, muk_ref, muv_ref, mugd_ref, muwd_ref, muad_ref,
                 w0_ref, w2_ref, a0_ref, a2_ref, g2_ref, kk_ref, ka_ref, rk_ref, lnw_ref, lnb_ref,
                 tri_ref, ones_ref, mstrict_ref, mincl_ref, mblk_ref,
                 o_ref, state_ref, *, bb, gq):
    c, g = CHUNK, RWKV_GROUP
    n, rows, gw = g * c, bb * c, g * RWKV_HEAD
    first = pl.program_id(2) == 0

    @pl.when(first)
    def _():
        state_ref[...] = jnp.zeros_like(state_ref)

    keep_prev = jnp.where(first, 0.0, 1.0)

    def per_row(x):
        return [x[i * c:(i + 1) * c] for i in range(bb)]

    def shifted(cur_ref, prev_ref, mu_ref, lanes):
        p = cur_ref[:, :, lanes].reshape(rows, -1)
        width = p.shape[-1]
        prev = jnp.concatenate(
            [jnp.broadcast_to(prev_ref[i, 7:8, lanes] * keep_prev, (c, width)) for i in range(bb)], axis=0)
        is_row0 = (lax.broadcasted_iota(jnp.int32, p.shape, 0) & (c - 1)) == 0
        p_prev = jnp.where(is_row0, prev, pltpu.roll(p, 1, axis=0))
        return p + (p_prev - p) * mu_ref[:, lanes]

    everything = slice(None)
    tanh_wd = jnp.tanh(shifted(wd_ref, wdp_ref, muwd_ref, everything)).astype(BF16)
    a_down = shifted(ad_ref, adp_ref, muad_ref, everything).astype(BF16)
    sig_gd = _sigmoid(shifted(gd_ref, gdp_ref, mugd_ref, everything)).astype(BF16)

    ones_bd = ones_ref[...]
    m_blk = mblk_ref[...]
    m_blk_bf16 = m_blk.astype(BF16)
    tri = tri_ref[...]

    bonus_all, gate_all = [], []
    rt, kkt, km, bm, k_end, b_end, v_rows, c_end = [], [], [], [], [], [], [], []
    for q in range(gq):
        lanes = slice(q * gw, (q + 1) * gw)
        r = shifted(r_ref, rp_ref, mur_ref, lanes)
        k = shifted(k_ref, kp_ref, muk_ref, lanes)
        v = shifted(v_ref, vp_ref, muv_ref, lanes)
        w = -_softplus(-(w0_ref[:, lanes] + _dot(tanh_wd, w2_ref[:, lanes]))) - 0.5
        ld = -jnp.exp(w)
        a = _sigmoid(a0_ref[:, lanes] + _dot(a_down, a2_ref[:, lanes]))
        gate_all.append(_dot(sig_gd, g2_ref[:, lanes]))

        kkr = k * kk_ref[:, lanes]
        kk = kkr * lax.rsqrt(jnp.maximum(_dot(kkr * kkr, ones_bd), 1e-24))
        k2 = k * (1.0 + (a - 1.0) * ka_ref[:, lanes])
        bw = a * kk

        cum = _dot_split(tri, ld)
        ends = [cum[(i + 1) * c - 1:(i + 1) * c, :] for i in range(bb)]
        c_end_rows = jnp.concatenate([jnp.broadcast_to(e, (c, gw)) for e in ends], axis=0)
        e_neg = jnp.exp(-cum)
        e_end = jnp.exp(c_end_rows - cum)
        c_end += ends
        rt += per_row(r * jnp.exp(cum))
        kkt += per_row(kk * jnp.exp(cum - ld))
        km += per_row(k2 * e_neg)
        bm += per_row(bw * e_neg)
        k_end += per_row(k2 * e_end)
        b_end += per_row(bw * e_end)
        v_rows += per_row(v)
        bonus_all.append(_dot(r * k2 * rk_ref[:, lanes], ones_bd) * v)

    def stack(x):
        return jnp.concatenate([x.astype(BF16)] * g, axis=0) * m_blk_bf16

    m_strict, m_incl = mstrict_ref[...], mincl_ref[...]
    lhs = [jnp.concatenate([kq, rq], axis=0).astype(BF16) for kq, rq in zip(kkt, rt)]
    aa = [_dot_nt(l, jnp.concatenate([stack(b_), stack(k_)], axis=0))
          for l, b_, k_ in zip(lhs, bm, km)]
    a_ab = [t[:c, :n] * m_strict for t in aa]
    a_kr = [jnp.concatenate([t[:c, n:] * m_strict, t[c:, n:] * m_incl], axis=0) for t in aa]
    a_rb = [t[c:, :n] * m_incl for t in aa]

    eye = m_incl - m_strict
    x = [eye - t for t in a_ab]
    p = [_dot(t, stack(t)) for t in a_ab]
    span = 2
    while span < c:
        span *= 2
        if span < c:
            xp = [_dot(jnp.concatenate([xi, pi], axis=0), stack(pi)) for xi, pi in zip(x, p)]
            x = [xi + t[:c] for xi, t in zip(x, xp)]
            p = [t[c:] for t in xp]
        else:
            x = [xi + _dot(xi, stack(pi)) for xi, pi in zip(x, p)]

    chains = [(q, i) for q in range(gq) for i in range(bb)]
    st = [state_ref[i, q] for q, i in chains]
    sk = [_dot_nt(l, s) for l, s in zip(lhs, st)]
    av = [_dot(t, stack(vi)) for t, vi in zip(a_kr, v_rows)]
    u = [_dot(xi, stack(-(s[:c] + t[:c]))) for xi, s, t in zip(x, sk, av)]
    y = [s[c:] + _dot(t, stack(ui)) + w_[c:] for s, t, ui, w_ in zip(sk, a_rb, u, av)]
    for j, (q, i) in enumerate(chains):
        state_ref[i, q] = st[j] * jnp.exp(c_end[j]) + _dot_tn(
            jnp.concatenate([u[j], v_rows[j]], axis=0),
            jnp.concatenate([b_end[j], k_end[j]], axis=0)) * m_blk

    inv_n = 1.0 / RWKV_HEAD
    for q in range(gq):
        lanes = slice(q * gw, (q + 1) * gw)
        yq = jnp.concatenate(y[q * bb:(q + 1) * bb], axis=0)
        mu = _dot(yq, ones_bd) * inv_n
        dy = yq - mu
        var = _dot(dy * dy, ones_bd) * inv_n
        yn = dy * lax.rsqrt(var + RWKV_LN_EPS) * lnw_ref[:, lanes] + lnb_ref[:, lanes]
        o_ref[:, :, lanes] = ((yn + bonus_all[q]) * gate_all[q]).astype(o_ref.dtype).reshape(bb, c, gw)


def _rwkv(proj, mus, w0, w2, a0, a2, g2, k_k, k_a, r_k, ln_w, ln_b, consts, cols, bb, width, lora):
    batch, seq, _ = proj.shape
    nc = seq // CHUNK
    gw = RWKV_GROUP * RWKV_HEAD
    gq = next(m for m in (RWKV_GROUPS_PER_STEP, 1) if (width // gw) % m == 0)
    sw = gq * gw
    steps = width // sw
    gdw = g2.shape[0]

    def prev_rows(c):
        return jnp.maximum(c * (CHUNK // 8) - 1, 0)

    def cur(w_, col, per_group):
        if per_group:
            return pl.BlockSpec((bb, CHUNK, w_), lambda b, q, c: (b, c, col // w_ + q))
        return pl.BlockSpec((bb, CHUNK, w_), lambda b, q, c: (b, c, col // w_))

    def prev(w_, col, per_group):
        if per_group:
            return pl.BlockSpec((bb, 8, w_), lambda b, q, c: (b, prev_rows(c), col // w_ + q))
        return pl.BlockSpec((bb, 8, w_), lambda b, q, c: (b, prev_rows(c), col // w_))

    def gvec():
        return pl.BlockSpec((1, sw), lambda b, q, c: (0, q))

    def full(shape):
        return pl.BlockSpec(shape, lambda b, q, c: (0, 0))

    n = RWKV_GROUP * CHUNK
    in_specs = [
        cur(sw, cols["rw_r"], True), cur(sw, cols["rw_k"], True), cur(sw, cols["rw_v"], True),
        cur(gdw, cols["rw_gd"], False), cur(lora, cols["rw_wd"], False), cur(lora, cols["rw_ad"], False),
        prev(sw, cols["rw_r"], True), prev(sw, cols["rw_k"], True), prev(sw, cols["rw_v"], True),
        prev(gdw, cols["rw_gd"], False), prev(lora, cols["rw_wd"], False), prev(lora, cols["rw_ad"], False),
        gvec(), gvec(), gvec(), full((1, gdw)), full((1, lora)), full((1, lora)),
        gvec(), pl.BlockSpec((lora, sw), lambda b, q, c: (0, q)),
        gvec(), pl.BlockSpec((lora, sw), lambda b, q, c: (0, q)),
        pl.BlockSpec((gdw, sw), lambda b, q, c: (0, q)),
        gvec(), gvec(), gvec(), gvec(), gvec(),
        full((bb * CHUNK, bb * CHUNK)), full((gw, gw)), full((CHUNK, n)), full((CHUNK, n)), full((n, gw)),
    ]
    return pl.pallas_call(
        functools.partial(_rwkv_kernel, bb=bb, gq=gq),
        grid=(batch // bb, steps, nc),
        in_specs=in_specs,
        out_specs=pl.BlockSpec((bb, CHUNK, sw), lambda b, q, c: (b, c, q)),
        out_shape=jax.ShapeDtypeStruct((batch, seq, width), BF16),
        scratch_shapes=[pltpu.VMEM((bb, gq, gw, gw), F32)],
        compiler_params=_cparams(("parallel", "parallel", "arbitrary")),
    )(*([proj] * 12),
      mus["r"], mus["k"], mus["v"], mus["gd"], mus["wd"], mus["ad"],
      w0, w2, a0, a2, g2, k_k, k_a, r_k, ln_w, ln_b,
      consts["tri_rows"], consts["ones_bd"], consts["m_strict"], consts["m_incl"], consts["m_blk"])


def _pad_rows(w, n):
    return jnp.pad(w, ((0, n - w.shape[0]), (0, 0)))


def _layout(d_model, lora, g_lora):
    kw, vw, rw = d_model // 4, d_model // 2, d_model // 2
    o_rw = 2 * kw + 2 * vw + GLA_GATE_RANK
    o_gate = o_rw + 3 * rw + 2 * lora + g_lora
    segments = [("gla_q", kw, 0), ("gla_k", kw, kw), ("gla_v", vw, 2 * kw),
                ("gla_og", vw, 2 * kw + vw + GLA_GATE_RANK),
                ("rw_r", rw, o_rw), ("rw_k", rw, o_rw + rw + lora), ("rw_v", rw, o_rw + 2 * rw + lora),
                ("gate_gla", d_model, o_gate), ("gate_rwkv", d_model, o_gate + d_model),
                ("rw_gd", 2 * PROJ_BN, o_rw + 3 * rw + 2 * lora),
                ("rw_wd", PROJ_BN, o_rw + rw), ("rw_ad", PROJ_BN, o_rw + 3 * rw + lora),
                ("gla_gd", PROJ_BN, 2 * kw + vw)]
    n_src = o_gate + 2 * d_model
    cols, row_starts, off = {}, [], 0
    for name, w, src in segments:
        assert off % w == 0 and w % PROJ_BN == 0 and src % ROW_TILE == 0 and src + w <= n_src, (name, off, w, src)
        cols[name] = off
        row_starts += [src + b for b in range(0, w, PROJ_BN)]
        off += w
    return cols, off, row_starts


def _chunk_consts(bb):
    c, g, hn = CHUNK, RWKV_GROUP, RWKV_HEAD
    assert c == hn
    n = g * c
    i = jnp.arange(n)
    ti, tj = jnp.arange(c)[:, None], i[None, :] % c
    lane = jnp.arange(g * hn)
    return {
        "tri_rows": jnp.kron(jnp.eye(bb, dtype=F32), jnp.tril(jnp.ones((c, c), F32))).astype(BF16),
        "ones_bd": ((lane[:, None] // hn) == (lane[None, :] // hn)).astype(BF16),
        "m_strict": (ti > tj).astype(F32),
        "m_incl": (ti >= tj).astype(F32),
        "m_blk": ((i[:, None] // c) == (lane[None, :] // hn)).astype(F32),
    }


def kernel(x, ffn1_pre_norm, ffn1_w_gate, ffn1_w_up, ffn1_w_down, ffn1_post_norm, mix_pre_norm, w_in, gla_gate_up, gla_gate_bias, gla_out_norm, rwkv_shift_mix, rwkv_w0, rwkv_w2, rwkv_a0, rwkv_a2, rwkv_g2, rwkv_k_k, rwkv_k_a, rwkv_r_k, rwkv_ln_w, rwkv_ln_b, w_up_gla, w_up_rwkv, w_out, mix_post_norm, ffn2_pre_norm, ffn2_w_gate, ffn2_w_up, ffn2_w_down, ffn2_post_norm):
    batch, seq, d_model = x.shape
    depth = ffn1_pre_norm.shape[0]
    t = batch * seq
    assert seq % CHUNK == 0 and d_model % 1024 == 0

    gla_kw, gla_vw, rw_w = d_model // 4, d_model // 2, d_model // 2
    gla_heads = max(4, d_model // 512)
    dk, dv = gla_kw // gla_heads, gla_vw // gla_heads
    lora = rwkv_w2.shape[1]
    g_lora = rwkv_g2.shape[1]
    assert lora == LANES and g_lora <= 2 * PROJ_BN and dk % LANES == 0

    cols, n_proj, row_starts = _layout(d_model, lora, g_lora)
    bm = min(1024, t)
    bm_down = min(512, t)
    bm_gu = min(2048, t)
    bt = min(512, t)
    bb = next(n for n in (4, 2, 1) if batch % n == 0)
    consts = _chunk_consts(bb)

    def row(v):
        return v.reshape(1, -1).astype(F32)

    h = x.reshape(t, d_model)
    for l in range(depth):
        xn = _norm_cast(h, row(ffn1_pre_norm[l]), bt)
        hid, w_down = _gate_up(xn, ffn1_w_gate[l], ffn1_w_up[l], ffn1_w_down[l], bm_gu, 256)
        f, w_in_t = _matmul_and_cast(hid, w_down, BF16, bm_down, 512, jnp.swapaxes(w_in[l], 0, 1))
        h, u = _post_pre(h, f, row(ffn1_post_norm[l]), row(mix_pre_norm[l]), MACARON_WEIGHT, bt)

        proj = _proj(u, w_in_t, jnp.asarray(row_starts, jnp.int32), F32, bm, PROJ_BN, PROJ_BLOCKS_PER_STEP)

        mu = rwkv_shift_mix[l]
        mu_off = [0]
        for s in (rw_w, lora, rw_w, rw_w, lora, g_lora):
            mu_off.append(mu_off[-1] + s)
        mu_r, mu_wd, mu_k, mu_v, mu_ad, mu_gd = (mu[mu_off[i]:mu_off[i + 1]] for i in range(6))
        mus = {"r": row(mu_r), "k": row(mu_k), "v": row(mu_v),
               "gd": row(jnp.pad(mu_gd, (0, 2 * PROJ_BN - g_lora))), "wd": row(mu_wd), "ad": row(mu_ad)}

        proj3 = proj.reshape(batch, seq, n_proj)
        o_gla = _gla(proj3, _pad_rows(gla_gate_up[l], LANES).astype(BF16), row(gla_gate_bias[l]),
                     row(gla_out_norm[l]), consts["tri_rows"], cols, bb, gla_heads, dk, dv)
        o_rwkv = _rwkv(proj3, mus, row(rwkv_w0[l]), rwkv_w2[l].astype(BF16), row(rwkv_a0[l]),
                       rwkv_a2[l].astype(BF16), _pad_rows(rwkv_g2[l], 2 * PROJ_BN).astype(BF16),
                       row(rwkv_k_k[l]), row(rwkv_k_a[l]), row(rwkv_r_k[l]), row(rwkv_ln_w[l]),
                       row(rwkv_ln_b[l]), consts, cols, bb, rw_w, lora)

        merged = _merge(o_gla.reshape(t, gla_vw), o_rwkv.reshape(t, rw_w),
                        w_up_gla[l].astype(BF16), w_up_rwkv[l].astype(BF16), proj,
                        cols["gate_gla"], cols["gate_rwkv"], bm, 512)
        mixed = _matmul(merged, w_out[l].astype(BF16), BF16, bm, 1024)
        h, u = _post_pre(h, mixed, row(mix_post_norm[l]), row(ffn2_pre_norm[l]), 1.0, bt)

        hid, w_down = _gate_up(u, ffn2_w_gate[l], ffn2_w_up[l], ffn2_w_down[l], bm_gu, 256)
        f = _matmul(hid, w_down, BF16, bm_down, 512)
        h = _post(h, f, row(ffn2_post_norm[l]), MACARON_WEIGHT, bt)
    return h.reshape(batch, seq, d_model)
```
